```python
import jax, jax.numpy as jnp
from jax import lax
import numpy as np

D_MODEL = 1024
BATCH = 8
SEQ = 2048
DEPTH = 2
DEC_BATCH = 128
DEC_SEQ = 4
PAST_LEN = 16384
PAGE_SIZE = 128

N_EVEN = (DEPTH + 1) // 2
N_ODD = DEPTH // 2
CHUNK = 128
EPS = 1e-6
SGU_GROUPS = 4
SGU_DIM = 128
SGU_WIDTH = SGU_GROUPS * SGU_DIM
SC_WIDTH = 512
SC_CONV = 3
EVEN_IN = 2 * SGU_WIDTH + 3 * SC_WIDTH
EVEN_OUT = SGU_WIDTH + SC_WIDTH
RET_HEADS = D_MODEL // 256
RET_DK = 256
RET_DV = 512
ODD_IN = 2 * RET_HEADS * RET_DK + 2 * RET_HEADS * RET_DV
ODD_OUT = RET_HEADS * RET_DV
ROPE_BASE = 10000.0
N_MEM = 256
X_HEADS = 4
X_HD = D_MODEL // X_HEADS
D_FF = 2816
FFN_CONV = 3

kernel_name = 'hybrid_sgu_shortconv_retention_decoder_step'


def rms_norm(x, g):
    xf = x.astype(jnp.float32)
    y = xf * lax.rsqrt(jnp.mean(xf * xf, axis=-1, keepdims=True) + EPS)
    return (y * g.astype(jnp.float32)).astype(x.dtype)


def causal_dwconv(z, prev, w):
    K = w.shape[0]
    T = z.shape[1]
    zp = jnp.concatenate([prev.astype(z.dtype), z], axis=1)
    y = sum(w[j] * zp[:, j:j + T] for j in range(K))
    return y, zp[:, zp.shape[1] - (K - 1):]


def rope(x, pos):
    half = x.shape[-1] // 2
    inv = ROPE_BASE ** (-jnp.arange(half, dtype=jnp.float32) / half)
    ang = pos.astype(jnp.float32)[:, None] * inv[None, :]
    cos = jnp.cos(ang)[None, :, None, :]
    sin = jnp.sin(ang)[None, :, None, :]
    x1, x2 = x[..., :half], x[..., half:]
    return jnp.concatenate([x1 * cos - x2 * sin, x1 * sin + x2 * cos], axis=-1)


def spatial_gating(u, v, w_s, b_s):
    B, T, G, C = v.shape
    L = min(T, CHUNK)
    n = T // L
    mask = jnp.tril(jnp.ones((L, L), dtype=bool))
    w = jnp.where(mask[None], w_s[:, :L, :L], 0).astype(v.dtype)
    vc = v.reshape(B, n, L, G, C)
    mixed = jnp.einsum('gts,bnsgc->bntgc', w, vc) + b_s[:, :L].T.astype(v.dtype)[None, None, :, :, None]
    return u * mixed.reshape(B, T, G, C)


def retention_block(q, k, v, s_prev, log_g):
    L = q.shape[1]
    i = jnp.arange(L, dtype=jnp.float32)
    diff = i[:, None] - i[None, :]
    dmat = jnp.where(diff[None] >= 0, jnp.exp(log_g[:, None, None] * jnp.maximum(diff, 0.0)[None]), 0.0)
    scores = jnp.einsum('bihd,bjhd->bhij', q, k) * dmat[None]
    o = jnp.einsum('bhij,bjhe->bihe', scores, v)
    q_decay = jnp.exp(log_g[None, :] * (i[:, None] + 1.0))
    o = o + jnp.einsum('bihd,bhde->bihe', q, s_prev) * q_decay[None, :, :, None]
    k_decay = jnp.exp(log_g[None, :] * (L - 1.0 - i)[:, None])
    s_new = jnp.exp(log_g * L)[None, :, None, None] * s_prev + jnp.einsum('bjhd,bjhe->bhde', k * k_decay[None, :, :, None], v)
    return o, s_new


def retention(q, k, v, s0, log_g):
    B, T = q.shape[0], q.shape[1]
    L = min(T, CHUNK)
    n = T // L

    def split(a):
        return jnp.moveaxis(a.reshape(B, n, L, *a.shape[2:]), 1, 0)

    def step(s, qkv):
        qc, kc, vc = qkv
        o, s = retention_block(qc, kc, vc, s, log_g)
        return s, o

    s_T, o = lax.scan(step, s0, (split(q), split(k), split(v)))
    o = jnp.moveaxis(o, 0, 1).reshape(B, T, *o.shape[3:])
    return o, s_T


def setup_inputs(seed: int = 0) -> dict:
    key = jax.random.key(seed)
    keys = jax.random.split(key, 64)
    ctr = [0]

    def nk():
        ctr[0] += 1
        return keys[ctr[0] - 1]

    def nrm(shape, scale=1.0):
        return jax.random.normal(nk(), shape, jnp.float32) * scale

    def gain(shape):
        return 1.0 + 0.05 * nrm(shape)

    D = D_MODEL
    return {
        'x_prompt': nrm((BATCH, SEQ, D)),
        'x_sample': nrm((DEC_BATCH, DEC_SEQ, D)),
        'mem_prompt': nrm((BATCH, N_MEM, D)),
        'cache_mem_k': nrm((DEPTH, DEC_BATCH, N_MEM, X_HEADS, X_HD)),
        'cache_mem_v': nrm((DEPTH, DEC_BATCH, N_MEM, X_HEADS, X_HD)),
        'state_shortconv': nrm((N_EVEN, DEC_BATCH, SC_CONV - 1, SC_WIDTH)),
        'state_retention': nrm((N_ODD, DEC_BATCH, RET_HEADS, RET_DK, RET_DV)),
        'state_ffn_conv': nrm((DEPTH, DEC_BATCH, FFN_CONV - 1, 2 * D_FF)),
        'norm_mix_pre': gain((DEPTH, D)),
        'norm_mix_post': gain((DEPTH, D)),
        'w_in_even': nrm((N_EVEN, D, EVEN_IN), D ** -0.5),
        'sgu_vnorm': gain((N_EVEN, SGU_WIDTH)),
        'sgu_w': nrm((N_EVEN, SGU_GROUPS, CHUNK, CHUNK), CHUNK ** -0.5),
        'sgu_b': 1.0 + 0.1 * nrm((N_EVEN, SGU_GROUPS, CHUNK)),
        'conv_short': nrm((N_EVEN, SC_CONV, SC_WIDTH), SC_CONV ** -0.5),
        'w_out_even': nrm((N_EVEN, EVEN_OUT, D), EVEN_OUT ** -0.5),
        'w_in_odd': nrm((N_ODD, D, ODD_IN), D ** -0.5),
        'w_out_odd': nrm((N_ODD, ODD_OUT, D), ODD_OUT ** -0.5),
        'norm_x_pre': gain((DEPTH, D)),
        'norm_x_post': gain((DEPTH, D)),
        'norm_mem': gain((DEPTH, D)),
        'w_xq': nrm((DEPTH, D, D), D ** -0.5),
        'w_xk': nrm((DEPTH, D, D), D ** -0.5),
        'w_xv': nrm((DEPTH, D, D), D ** -0.5),
        'w_xo': nrm((DEPTH, D, D), D ** -0.5),
        'norm_ffn_pre': gain((DEPTH, D)),
        'norm_ffn_post': gain((DEPTH, D)),
        'w_ffn_up': nrm((DEPTH, D, 2 * D_FF), D ** -0.5),
        'conv_ffn': nrm((DEPTH, FFN_CONV, 2 * D_FF), FFN_CONV ** -0.5),
        'w_ffn_down': nrm((DEPTH, D_FF, D), D_FF ** -0.5),
    }


def reference(x_prompt, x_sample, mem_prompt, cache_mem_k, cache_mem_v, state_shortconv, state_retention, state_ffn_conv,
              norm_mix_pre, norm_mix_post, w_in_even, sgu_vnorm, sgu_w, sgu_b, conv_short, w_out_even, w_in_odd, w_out_odd,
              norm_x_pre, norm_x_post, norm_mem, w_xq, w_xk, w_xv, w_xo, norm_ffn_pre, norm_ffn_post, w_ffn_up, conv_ffn, w_ffn_down):
    f32 = jnp.float32

    def even_mixer(e, h, conv_prev):
        B, T, _ = h.shape
        z = h @ w_in_even[e]
        u, v, bg, cg, hin = jnp.split(z, [SGU_WIDTH, 2 * SGU_WIDTH, 2 * SGU_WIDTH + SC_WIDTH, 2 * SGU_WIDTH + 2 * SC_WIDTH], axis=-1)
        u = jax.nn.gelu(u, approximate=True)
        v = rms_norm(jax.nn.gelu(v, approximate=True), sgu_vnorm[e])
        a = spatial_gating(u.reshape(B, T, SGU_GROUPS, SGU_DIM), v.reshape(B, T, SGU_GROUPS, SGU_DIM), sgu_w[e], sgu_b[e]).reshape(B, T, SGU_WIDTH)
        cz, conv_new = causal_dwconv(cg * hin, conv_prev, conv_short[e])
        out = jnp.concatenate([a, bg * cz], axis=-1) @ w_out_even[e]
        return out, conv_new, v

    def odd_mixer(o, h, s_prev, pos):
        B, T, _ = h.shape
        nq = RET_HEADS * RET_DK
        nv = RET_HEADS * RET_DV
        z = h @ w_in_odd[o]
        q, k, v, g = jnp.split(z, [nq, 2 * nq, 2 * nq + nv], axis=-1)
        q = rope(q.reshape(B, T, RET_HEADS, RET_DK).astype(f32), pos) * (RET_DK ** -0.5)
        k = rope(k.reshape(B, T, RET_HEADS, RET_DK).astype(f32), pos)
        v = v.reshape(B, T, RET_HEADS, RET_DV).astype(f32)
        log_g = jnp.log1p(-jnp.exp2(-5.0 - jnp.arange(RET_HEADS, dtype=f32)))
        r, s_new = retention(q, k, v, s_prev.astype(f32), log_g)
        mu = jnp.mean(r, axis=-1, keepdims=True)
        var = jnp.mean(jnp.square(r - mu), axis=-1, keepdims=True)
        r = ((r - mu) * lax.rsqrt(var + EPS)).reshape(B, T, nv).astype(h.dtype)
        out = (jax.nn.silu(g) * r) @ w_out_odd[o]
        return out, s_new.astype(h.dtype)

    def cross_attn(l, h, mk, mv):
        B, T, _ = h.shape
        q = (h @ w_xq[l]).reshape(B, T, X_HEADS, X_HD).astype(f32)
        s = jnp.einsum('bthd,bmhd->bhtm', q, mk.astype(f32)) * (X_HD ** -0.5)
        p = jax.nn.softmax(s, axis=-1)
        o = jnp.einsum('bhtm,bmhd->bthd', p, mv.astype(f32)).reshape(B, T, D_MODEL).astype(h.dtype)
        return o @ w_xo[l]

    def ffn(l, h, prev):
        up = h @ w_ffn_up[l]
        up, new = causal_dwconv(up, prev, conv_ffn[l])
        gt, val = jnp.split(up, [D_FF], axis=-1)
        return (jax.nn.gelu(gt, approximate=True) * val) @ w_ffn_down[l], new

    def trunk(x, pos0, mem_k, mem_v, conv_prev, ret_prev, ffn_prev):
        T = x.shape[1]
        pos = pos0 + jnp.arange(T, dtype=jnp.int32)
        new_conv, new_sgu, new_ret, new_ffn = [], [], [], []
        for l in range(DEPTH):
            h = rms_norm(x, norm_mix_pre[l])
            if l % 2 == 0:
                m, c, vrows = even_mixer(l // 2, h, conv_prev[l // 2])
                new_conv.append(c)
                new_sgu.append(vrows)
            else:
                m, s = odd_mixer(l // 2, h, ret_prev[l // 2], pos)
                new_ret.append(s)
            x = x + rms_norm(m, norm_mix_post[l])
            h = rms_norm(x, norm_x_pre[l])
            x = x + rms_norm(cross_attn(l, h, mem_k[l], mem_v[l]), norm_x_post[l])
            h = rms_norm(x, norm_ffn_pre[l])
            f, c = ffn(l, h, ffn_prev[l])
            new_ffn.append(c)
            x = x + rms_norm(f, norm_ffn_post[l])
        return x, jnp.stack(new_conv), jnp.stack(new_sgu), jnp.stack(new_ret), jnp.stack(new_ffn)

    Bp = x_prompt.shape[0]
    mks, mvs = [], []
    for l in range(DEPTH):
        mn = rms_norm(mem_prompt, norm_mem[l])
        mks.append((mn @ w_xk[l]).reshape(Bp, N_MEM, X_HEADS, X_HD))
        mvs.append((mn @ w_xv[l]).reshape(Bp, N_MEM, X_HEADS, X_HD))
    mem_k_prompt = jnp.stack(mks)
    mem_v_prompt = jnp.stack(mvs)
    dt = x_prompt.dtype
    zero_conv = jnp.zeros((N_EVEN, Bp, SC_CONV - 1, SC_WIDTH), dt)
    zero_ret = jnp.zeros((N_ODD, Bp, RET_HEADS, RET_DK, RET_DV), f32)
    zero_ffn = jnp.zeros((DEPTH, Bp, FFN_CONV - 1, 2 * D_FF), dt)
    y_prompt, shortconv_prompt, _, retention_prompt, ffn_conv_prompt = trunk(
        x_prompt, 0, mem_k_prompt, mem_v_prompt, zero_conv, zero_ret, zero_ffn)

    y_sample, shortconv_sample, sgu_v_sample, retention_sample, ffn_conv_sample = trunk(
        x_sample, PAST_LEN, cache_mem_k, cache_mem_v, state_shortconv, state_retention, state_ffn_conv)

    return (y_prompt, y_sample, mem_k_prompt, mem_v_prompt, shortconv_prompt, shortconv_sample, sgu_v_sample,
            retention_prompt, retention_sample, ffn_conv_prompt, ffn_conv_sample)
```

```python
import functools

import numpy as np
import jax
import jax.numpy as jnp
from jax import lax
from jax.experimental import pallas as pl
from jax.experimental.pallas import tpu as pltpu

F32 = jnp.float32
BF16 = jnp.bfloat16

D = 1024
SEQ = 2048
NB = 8
NS = 128
TS = 4
RS = NS * TS
PAST = 16384
CHUNK = 128
EPS = 1e-6
SGU_W = 512
SC_W = 512
RET_H = 4
RET_DK = 256
RET_DV = 512
NQ = RET_H * RET_DK
NV = RET_H * RET_DV
N_MEM = 256
XH = 4
XHD = 256
DFF = 2816
FC = 256
NFC = DFF // FC
ROPE_BASE = 10000.0

_LOG_G = [float(v) for v in np.log1p(-np.exp2(np.float32(-5.0) - np.arange(RET_H, dtype=np.float32))).astype(np.float32)]

_MIB = 1024 * 1024


def _rms(x, g):
    ms = jnp.mean(x * x, axis=-1, keepdims=True)
    return (x * lax.rsqrt(ms + EPS)) * g


def _gelu(x):
    return 0.5 * x * (1.0 + jnp.tanh(0.7978845608028654 * (x + 0.044715 * (x * x * x))))


def _dot(a, b):
    return jnp.dot(a, b, preferred_element_type=F32)


def _dot_nt(a, b):
    return lax.dot_general(a, b, (((1,), (1,)), ((), ())), preferred_element_type=F32)


def _const(shape):
    n = len(shape)
    return pl.BlockSpec(shape, lambda *_: (0,) * n, pipeline_mode=pl.Buffered(1))


def _full(shape):
    n = len(shape)
    return pl.BlockSpec(shape, lambda *_: (0,) * n)


def _params(sem, vmem_mib):
    return pltpu.CompilerParams(dimension_semantics=sem, vmem_limit_bytes=vmem_mib * _MIB)


def _conv3(p, p1, p2, w_ref, cols):
    return w_ref[0:1, cols] * p2 + w_ref[1:2, cols] * p1 + w_ref[2:3, cols] * p


def _shift_tile(p, prev_ref, cols):
    row = lax.broadcasted_iota(jnp.int32, p.shape, 0)
    m1 = prev_ref[7:8, cols]
    m2 = prev_ref[6:7, cols]
    p1 = jnp.where(row == 0, m1, pltpu.roll(p, 1, 0))
    p2 = jnp.where(row == 0, m2, jnp.where(row == 1, m1, pltpu.roll(p, 2, 0)))
    return p1, p2


def _shift_sample(p, e1, e2):
    t = lax.broadcasted_iota(jnp.int32, p.shape, 0) & (TS - 1)
    p1 = jnp.where(t >= 1, pltpu.roll(p, 1, 0), e1)
    p2 = jnp.where(t >= 2, pltpu.roll(p, 2, 0), e2)
    return p1, p2


def _memkv_body(mem_ref, g_ref, wk_ref, wv_ref, k_ref, v_ref, kb_ref, vb_ref):
    mn = _rms(mem_ref[...], g_ref[...]).astype(BF16)
    k = _dot(mn, wk_ref[...])
    v = _dot(mn, wv_ref[...])
    k_ref[...] = k
    v_ref[...] = v
    kb_ref[...] = k.astype(BF16)
    vb_ref[...] = v.astype(BF16)


def _memkv(mem, g, wk, wv):
    rows = mem.shape[0]
    tm = 512
    o_spec = pl.BlockSpec((None, tm, D), lambda l, i: (l, i, 0))
    w_spec = pl.BlockSpec((None, D, D), lambda l, i: (l, 0, 0))
    return pl.pallas_call(
        _memkv_body,
        grid=(2, rows // tm),
        in_specs=[pl.BlockSpec((tm, D), lambda l, i: (i, 0)),
                  pl.BlockSpec((None, 1, D), lambda l, i: (l, 0, 0)),
                  w_spec, w_spec],
        out_specs=[o_spec, o_spec, o_spec, o_spec],
        out_shape=[jax.ShapeDtypeStruct((2, rows, D), F32)] * 2 + [jax.ShapeDtypeStruct((2, rows, D), BF16)] * 2,
        compiler_params=_params(("arbitrary", "arbitrary"), 40),
        name="memkv",
    )(mem, g, wk, wv)


def _even_front(x, gpre_ref, win_ref, vn_ref):
    h = _rms(x, gpre_ref[...]).astype(BF16)
    u = _gelu(_dot(h, win_ref[:, 0:512]))
    v = _rms(_gelu(_dot(h, win_ref[:, 512:1024])), vn_ref[...])
    bg = _dot(h, win_ref[:, 1024:1536])
    p = _dot(h, win_ref[:, 1536:2048]) * _dot(h, win_ref[:, 2048:2560])
    return u, v, bg, p


def _even_p_body(x_ref, gpre_ref, win_ref, vn_ref, sw_ref, sb_ref, cw_ref, wout_ref, gpost_ref,
                 xo_ref, st_ref, carry_ref, cat_ref, *, tm):
    @pl.when(pl.program_id(1) == 0)
    def _():
        carry_ref[...] = jnp.zeros_like(carry_ref)

    x = x_ref[...]
    u, v, bg, p = _even_front(x, gpre_ref, win_ref, vn_ref)
    vb = v.astype(BF16)
    ri = lax.broadcasted_iota(jnp.int32, (CHUNK, CHUNK), 0)
    ci = lax.broadcasted_iota(jnp.int32, (CHUNK, CHUNK), 1)
    for g in range(4):
        cols = slice(g * 128, (g + 1) * 128)
        w = jnp.where(ri >= ci, sw_ref[g], 0.0).astype(BF16)
        for c in range(tm // CHUNK):
            rows = slice(c * CHUNK, (c + 1) * CHUNK)
            mixed = _dot(w, vb[rows, cols]) + sb_ref[:, cols]
            cat_ref[rows, cols] = (u[rows, cols] * mixed).astype(BF16)
    allc = slice(0, SC_W)
    p1, p2 = _shift_tile(p, carry_ref, allc)
    cz = _conv3(p, p1, p2, cw_ref, allc)
    cat_ref[:, 512:1024] = (bg * cz).astype(BF16)
    carry_ref[...] = p[tm - 8:tm, :]
    st_ref[...] = p[tm - 8:tm, :]
    out = _dot(cat_ref[...], wout_ref[...])
    xo_ref[...] = x + _rms(out, gpost_ref[...])


def _even_p(x, gpre, win, vn, sw, sb, cw, wout, gpost):
    tm = 256
    nt = SEQ // tm
    row_spec = pl.BlockSpec((tm, D), lambda b, t: (b * nt + t, 0))
    return pl.pallas_call(
        functools.partial(_even_p_body, tm=tm),
        grid=(NB, nt),
        in_specs=[row_spec, _const((1, D)), _const((D, 2560)), _const((1, SGU_W)), _const((4, CHUNK, CHUNK)),
                  _const((CHUNK, SGU_W)), _const((3, SC_W)), _const((D, D)), _const((1, D))],
        out_specs=[row_spec, pl.BlockSpec((None, 8, SC_W), lambda b, t: (b, 0, 0))],
        out_shape=[jax.ShapeDtypeStruct((NB * SEQ, D), F32), jax.ShapeDtypeStruct((NB, 8, SC_W), F32)],
        scratch_shapes=[pltpu.VMEM((8, SC_W), F32), pltpu.VMEM((tm, D), BF16)],
        compiler_params=_params(("arbitrary", "arbitrary"), 40),
        name="even_p",
    )(x, gpre, win, vn, sw, sb, cw, wout, gpost)


def _even_s_body(x_ref, gpre_ref, win_ref, vn_ref, gt_ref, sb_ref, cw_ref, e1_ref, e2_ref, wout_ref, gpost_ref,
                 xo_ref, p_ref, v_ref, cat_ref):
    x = x_ref[...]
    u, v, bg, p = _even_front(x, gpre_ref, win_ref, vn_ref)
    v_ref[...] = v
    p_ref[...] = p
    mixed = sb_ref[...] + gt_ref[0] * v
    for k in range(1, TS):
        mixed = mixed + gt_ref[k] * pltpu.roll(v, k, 0)
    cat_ref[:, 0:512] = (u * mixed).astype(BF16)
    p1, p2 = _shift_sample(p, e1_ref[...], e2_ref[...])
    cz = _conv3(p, p1, p2, cw_ref, slice(0, SC_W))
    cat_ref[:, 512:1024] = (bg * cz).astype(BF16)
    out = _dot(cat_ref[...], wout_ref[...])
    xo_ref[...] = x + _rms(out, gpost_ref[...])


def _even_s(x, gpre, win, vn, gtab, sbt, cw, e1, e2, wout, gpost):
    return pl.pallas_call(
        _even_s_body,
        grid=(1,),
        in_specs=[_const((RS, D)), _const((1, D)), _const((D, 2560)), _const((1, SGU_W)), _const((TS, RS, SGU_W)),
                  _const((RS, SGU_W)), _const((3, SC_W)), _const((RS, SC_W)), _const((RS, SC_W)), _const((D, D)),
                  _const((1, D))],
        out_specs=[_full((RS, D)), _full((RS, SC_W)), _full((RS, SGU_W))],
        out_shape=[jax.ShapeDtypeStruct((RS, D), F32), jax.ShapeDtypeStruct((RS, SC_W), F32),
                   jax.ShapeDtypeStruct((RS, SGU_W), F32)],
        scratch_shapes=[pltpu.VMEM((RS, D), BF16)],
        compiler_params=_params(("arbitrary",), 48),
        name="even_s",
    )(x, gpre, win, vn, gtab, sbt, cw, e1, e2, wout, gpost)


def _softmax_rows(s):
    m = jnp.max(s, axis=-1, keepdims=True)
    e = jnp.exp(s - m)
    return e * (1.0 / jnp.sum(e, axis=-1, keepdims=True))


def _xattn_p_body(x_ref, gpre_ref, wq_ref, k_ref, v_ref, wo_ref, gpost_ref, xo_ref, o_ref):
    x = x_ref[...]
    h = _rms(x, gpre_ref[...]).astype(BF16)
    for hd in range(XH):
        cols = slice(hd * XHD, (hd + 1) * XHD)
        q = _dot(h, wq_ref[:, cols]).astype(BF16)
        s = _dot_nt(q, k_ref[:, cols]) * (XHD ** -0.5)
        p = _softmax_rows(s).astype(BF16)
        o_ref[:, cols] = _dot(p, v_ref[:, cols]).astype(BF16)
    out = _dot(o_ref[...], wo_ref[...])
    xo_ref[...] = x + _rms(out, gpost_ref[...])


def _xattn_p(x, gpre, wq, kb, vb, wo, gpost):
    tm = 512
    nt = SEQ // tm
    row_spec = pl.BlockSpec((tm, D), lambda b, t: (b * nt + t, 0))
    kv_spec = pl.BlockSpec((N_MEM, D), lambda b, t: (b, 0))
    return pl.pallas_call(
        _xattn_p_body,
        grid=(NB, nt),
        in_specs=[row_spec, _const((1, D)), _const((D, D)), kv_spec, kv_spec, _const((D, D)), _const((1, D))],
        out_specs=row_spec,
        out_shape=jax.ShapeDtypeStruct((NB * SEQ, D), F32),
        scratch_shapes=[pltpu.VMEM((tm, D), BF16)],
        compiler_params=_params(("arbitrary", "arbitrary"), 40),
        name="xattn_p",
    )(x, gpre, wq, kb, vb, wo, gpost)


def _xattn_s_body(x_ref, gpre_ref, wq_ref, k_ref, v_ref, wo_ref, gpost_ref, xo_ref, q_ref, o_ref, *, bb):
    step = pl.program_id(0)

    @pl.when(step == 0)
    def _():
        h = _rms(x_ref[...], gpre_ref[...]).astype(BF16)
        q_ref[...] = _dot(h, wq_ref[...])

    row = lax.broadcasted_iota(jnp.int32, (8, D), 0)
    col = lax.broadcasted_iota(jnp.int32, (8, D), 1)
    for pi in range(bb // 2):
        r0 = pl.multiple_of((step * (bb // 2) + pi) * 8, 8)
        q8 = q_ref[pl.ds(r0, 8), :]
        o8 = jnp.zeros((8, D), F32)
        for bi in range(2):
            rmask = (row >> 2) == bi
            masks = [rmask & ((col >> 8) == hd) for hd in range(XH)]
            lhs = jnp.concatenate([jnp.where(m, q8, 0.0) for m in masks], axis=0).astype(BF16)
            kb = k_ref[pi * 2 + bi].astype(BF16)
            vb = v_ref[pi * 2 + bi].astype(BF16)
            s = _dot_nt(lhs, kb) * (XHD ** -0.5)
            p = _softmax_rows(s).astype(BF16)
            pf = _dot(p, vb)
            for hd in range(XH):
                o8 = o8 + jnp.where(masks[hd], pf[hd * 8:(hd + 1) * 8, :], 0.0)
        o_ref[pl.ds(r0, 8), :] = o8

    @pl.when(step == pl.num_programs(0) - 1)
    def _():
        out = _dot(o_ref[...].astype(BF16), wo_ref[...])
        xo_ref[...] = x_ref[...] + _rms(out, gpost_ref[...])


def _xattn_s(x, gpre, wq, k, v, wo, gpost):
    bb = 4
    kv_spec = pl.BlockSpec((bb, N_MEM, D), lambda i: (i, 0, 0))
    return pl.pallas_call(
        functools.partial(_xattn_s_body, bb=bb),
        grid=(NS // bb,),
        in_specs=[_const((RS, D)), _const((1, D)), _const((D, D)), kv_spec, kv_spec, _const((D, D)), _const((1, D))],
        out_specs=_full((RS, D)),
        out_shape=jax.ShapeDtypeStruct((RS, D), F32),
        scratch_shapes=[pltpu.VMEM((RS, D), F32), pltpu.VMEM((RS, D), F32)],
        compiler_params=_params(("arbitrary",), 48),
        name="xattn_s",
    )(x, gpre, wq, k, v, wo, gpost)


def _ffn_p_body(x_ref, gpre_ref, wup_ref, cw_ref, wdn_ref, gpost_ref, xo_ref, st_ref, carry_ref, *, tm):
    @pl.when(pl.program_id(1) == 0)
    def _():
        carry_ref[...] = jnp.zeros_like(carry_ref)

    x = x_ref[...]
    h = _rms(x, gpre_ref[...]).astype(BF16)
    acc = jnp.zeros((tm, D), F32)
    for c in range(NFC):
        halves = []
        for base in (0, DFF):
            cols = slice(base + c * FC, base + (c + 1) * FC)
            up = _dot(h, wup_ref[:, cols])
            p1, p2 = _shift_tile(up, carry_ref, cols)
            halves.append(_conv3(up, p1, p2, cw_ref, cols))
            carry_ref[:, cols] = up[tm - 8:tm, :]
            st_ref[:, cols] = up[tm - 8:tm, :]
        act = (_gelu(halves[0]) * halves[1]).astype(BF16)
        acc = acc + _dot(act, wdn_ref[c * FC:(c + 1) * FC, :])
    xo_ref[...] = x + _rms(acc, gpost_ref[...])


def _ffn_p(x, gpre, wup, cw, wdn, gpost):
    tm = 512
    nt = SEQ // tm
    row_spec = pl.BlockSpec((tm, D), lambda b, t: (b * nt + t, 0))
    return pl.pallas_call(
        functools.partial(_ffn_p_body, tm=tm),
        grid=(NB, nt),
        in_specs=[row_spec, _const((1, D)), _const((D, 2 * DFF)), _const((3, 2 * DFF)), _const((DFF, D)),
                  _const((1, D))],
        out_specs=[row_spec, pl.BlockSpec((None, 8, 2 * DFF), lambda b, t: (b, 0, 0))],
        out_shape=[jax.ShapeDtypeStruct((NB * SEQ, D), F32), jax.ShapeDtypeStruct((NB, 8, 2 * DFF), F32)],
        scratch_shapes=[pltpu.VMEM((8, 2 * DFF), F32)],
        compiler_params=_params(("arbitrary", "arbitrary"), 52),
        name="ffn_p",
    )(x, gpre, wup, cw, wdn, gpost)


def _ffn_s_body(x_ref, gpre_ref, wg_ref, wv_ref, cg_ref, cv_ref, e1g_ref, e2g_ref, e1v_ref, e2v_ref, wdn_ref, gpost_ref,
                xo_ref, ug_ref, uv_ref, h_ref, acc_ref):
    c = pl.program_id(0)

    @pl.when(c == 0)
    def _():
        h_ref[...] = _rms(x_ref[...], gpre_ref[...]).astype(BF16)
        acc_ref[...] = jnp.zeros_like(acc_ref)

    h = h_ref[...]
    allc = slice(0, FC)
    ug = _dot(h, wg_ref[...])
    uv = _dot(h, wv_ref[...])
    ug_ref[...] = ug
    uv_ref[...] = uv
    g1, g2 = _shift_sample(ug, e1g_ref[...], e2g_ref[...])
    v1, v2 = _shift_sample(uv, e1v_ref[...], e2v_ref[...])
    act = (_gelu(_conv3(ug, g1, g2, cg_ref, allc)) * _conv3(uv, v1, v2, cv_ref, allc)).astype(BF16)
    acc_ref[...] += _dot(act, wdn_ref[...])

    @pl.when(c == pl.num_programs(0) - 1)
    def _():
        xo_ref[...] = x_ref[...] + _rms(acc_ref[...], gpost_ref[...])


def _ffn_s(x, gpre, wup, cw, e1, e2, wdn, gpost):
    gcol = lambda r: pl.BlockSpec((r, FC), lambda c: (0, c))
    vcol = lambda r: pl.BlockSpec((r, FC), lambda c: (0, NFC + c))
    return pl.pallas_call(
        _ffn_s_body,
        grid=(NFC,),
        in_specs=[_const((RS, D)), _const((1, D)), gcol(D), vcol(D), gcol(3), vcol(3), gcol(RS), gcol(RS), vcol(RS),
                  vcol(RS), pl.BlockSpec((FC, D), lambda c: (c, 0)), _const((1, D))],
        out_specs=[_full((RS, D)), gcol(RS), gcol(RS)],
        out_shape=[jax.ShapeDtypeStruct((RS, D), F32), jax.ShapeDtypeStruct((RS, DFF), F32),
                   jax.ShapeDtypeStruct((RS, DFF), F32)],
        scratch_shapes=[pltpu.VMEM((RS, D), BF16), pltpu.VMEM((RS, D), F32)],
        compiler_params=_params(("arbitrary",), 40),
        name="ffn_s",
    )(x, gpre, wup, wup, cw, cw, e1, e2, e1, e2, wdn, gpost)


def _rope_head(z, cos, sin, base):
    x1 = z[:, base:base + 128]
    x2 = z[:, base + 128:base + 256]
    return x1 * cos - x2 * sin, x1 * sin + x2 * cos


def _group_norm_gate(o, g):
    mu = jnp.mean(o, axis=-1, keepdims=True)
    d = o - mu
    var = jnp.mean(d * d, axis=-1, keepdims=True)
    return (g * jax.nn.sigmoid(g)) * (d * lax.rsqrt(var + EPS))


def _ret_p_body(x_ref, cos_ref, sin_ref, gpre_ref, win_ref, wout_ref, gpost_ref, xo_ref, s_ref,
                q_ref, k_ref, kd_ref, y_ref, dm_ref, qd_ref, kdec_ref, *, tm):
    first = (pl.program_id(0) == 0) & (pl.program_id(1) == 0)

    @pl.when(first)
    def _():
        i = lax.broadcasted_iota(jnp.int32, (CHUNK, CHUNK), 0).astype(F32)
        j = lax.broadcasted_iota(jnp.int32, (CHUNK, CHUNK), 1).astype(F32)
        for hd in range(RET_H):
            lg = _LOG_G[hd]
            dm_ref[hd] = jnp.where(i >= j, jnp.exp(lg * jnp.maximum(i - j, 0.0)), 0.0)
            qd_ref[hd] = jnp.exp(lg * (i + 1.0))
            kdec_ref[hd] = jnp.exp(lg * (CHUNK - 1.0 - i))

    @pl.when(pl.program_id(1) == 0)
    def _():
        s_ref[...] = jnp.zeros_like(s_ref)

    x = x_ref[...]
    h = _rms(x, gpre_ref[...]).astype(BF16)
    cos = cos_ref[...]
    sin = sin_ref[...]
    nchunk = tm // CHUNK
    for hd in range(RET_H):
        base = hd * RET_DK
        zq = _dot(h, win_ref[:, base:base + RET_DK])
        zk = _dot(h, win_ref[:, NQ + base:NQ + base + RET_DK])
        q1, q2 = _rope_head(zq, cos, sin, 0)
        k1, k2 = _rope_head(zk, cos, sin, 0)
        q_ref[:, 0:128] = (q1 * (RET_DK ** -0.5)).astype(BF16)
        q_ref[:, 128:256] = (q2 * (RET_DK ** -0.5)).astype(BF16)
        k_ref[:, 0:128] = k1.astype(BF16)
        k_ref[:, 128:256] = k2.astype(BF16)
        for c in range(nchunk):
            rows = slice(c * CHUNK, (c + 1) * CHUNK)
            kdec = kdec_ref[hd]
            kd_ref[rows, 0:128] = k1[rows] * kdec
            kd_ref[rows, 128:256] = k2[rows] * kdec
        vcols = slice(2 * NQ + hd * RET_DV, 2 * NQ + (hd + 1) * RET_DV)
        gcols = slice(2 * NQ + NV + hd * RET_DV, 2 * NQ + NV + (hd + 1) * RET_DV)
        v = _dot(h, win_ref[:, vcols]).astype(BF16)
        g = _dot(h, win_ref[:, gcols])
        decay_l = float(np.exp(np.float32(_LOG_G[hd]) * np.float32(CHUNK)))
        for c in range(nchunk):
            rows = slice(c * CHUNK, (c + 1) * CHUNK)
            qc = q_ref[rows, :]
            kc = k_ref[rows, :]
            vc = v[rows]
            s_prev = s_ref[hd]
            sc = (_dot_nt(qc, kc) * dm_ref[hd]).astype(BF16)
            inter = _dot(qc, s_prev.astype(BF16))
            qd = qd_ref[hd]
            o = _dot(sc, vc) + inter * jnp.concatenate([qd, qd, qd, qd], axis=1)
            kdt = kd_ref[rows, :].T.astype(BF16)
            s_ref[hd] = decay_l * s_prev + _dot(kdt, vc)
            y_ref[rows, hd * RET_DV:(hd + 1) * RET_DV] = _group_norm_gate(o, g[rows]).astype(BF16)
    out = _dot(y_ref[...], wout_ref[...])
    xo_ref[...] = x + _rms(out, gpost_ref[...])


def _ret_p(x, cos, sin, gpre, win, wout, gpost):
    tm = 256
    nt = SEQ // tm
    row_spec = pl.BlockSpec((tm, D), lambda b, t: (b * nt + t, 0))
    cs_spec = pl.BlockSpec((tm, 128), lambda b, t: (t, 0))
    return pl.pallas_call(
        functools.partial(_ret_p_body, tm=tm),
        grid=(NB, nt),
        in_specs=[row_spec, cs_spec, cs_spec, _const((1, D)), _const((D, 2 * NQ + 2 * NV)), _const((NV, D)),
                  _const((1, D))],
        out_specs=[row_spec, pl.BlockSpec((None, RET_H, RET_DK, RET_DV), lambda b, t: (b, 0, 0, 0))],
        out_shape=[jax.ShapeDtypeStruct((NB * SEQ, D), F32), jax.ShapeDtypeStruct((NB, RET_H, RET_DK, RET_DV), F32)],
        scratch_shapes=[pltpu.VMEM((tm, RET_DK), BF16), pltpu.VMEM((tm, RET_DK), BF16), pltpu.VMEM((tm, RET_DK), F32),
                        pltpu.VMEM((tm, NV), BF16), pltpu.VMEM((RET_H, CHUNK, CHUNK), F32),
                        pltpu.VMEM((RET_H, CHUNK, CHUNK), F32), pltpu.VMEM((RET_H, CHUNK, CHUNK), F32)],
        compiler_params=_params(("arbitrary", "arbitrary"), 52),
        name="ret_p",
    )(x, cos, sin, gpre, win, wout, gpost)


def _ret_s_proj_body(x_ref, cos_ref, sin_ref, gpre_ref, win_ref, q_ref, kdt_ref, v_ref, g_ref, oi_ref):
    h = _rms(x_ref[...], gpre_ref[...]).astype(BF16)
    cos = cos_ref[...]
    sin = sin_ref[...]
    ti = (lax.broadcasted_iota(jnp.int32, (RS, 128), 0) & (TS - 1)).astype(F32)
    i = lax.broadcasted_iota(jnp.int32, (CHUNK, CHUNK), 0)
    j = lax.broadcasted_iota(jnp.int32, (CHUNK, CHUNK), 1)
    same = ((i >> 2) == (j >> 2)) & (i >= j)
    dij = jnp.maximum(i - j, 0).astype(F32)
    for hd in range(RET_H):
        lg = _LOG_G[hd]
        base = hd * RET_DK
        zq = _dot(h, win_ref[:, base:base + RET_DK])
        zk = _dot(h, win_ref[:, NQ + base:NQ + base + RET_DK])
        q1, q2 = _rope_head(zq, cos, sin, 0)
        k1, k2 = _rope_head(zk, cos, sin, 0)
        q = jnp.concatenate([q1, q2], axis=1) * (RET_DK ** -0.5)
        k = jnp.concatenate([k1, k2], axis=1)
        q_ref[:, base:base + RET_DK] = q
        kdec = jnp.exp(lg * (TS - 1.0 - ti))
        kd = k * jnp.concatenate([kdec, kdec], axis=1)
        vcols = slice(2 * NQ + hd * RET_DV, 2 * NQ + (hd + 1) * RET_DV)
        gcols = slice(2 * NQ + NV + hd * RET_DV, 2 * NQ + NV + (hd + 1) * RET_DV)
        vf = _dot(h, win_ref[:, vcols])
        v_ref[:, hd * RET_DV:(hd + 1) * RET_DV] = vf
        v = vf.astype(BF16)
        g_ref[:, hd * RET_DV:(hd + 1) * RET_DV] = _dot(h, win_ref[:, gcols])
        dm = jnp.where(same, jnp.exp(lg * dij), 0.0)
        qb = q.astype(BF16)
        kb = k.astype(BF16)
        for tl in range(RS // CHUNK):
            rows = slice(tl * CHUNK, (tl + 1) * CHUNK)
            sc = (_dot_nt(qb[rows], kb[rows]) * dm).astype(BF16)
            oi_ref[rows, hd * RET_DV:(hd + 1) * RET_DV] = _dot(sc, v[rows])
            kdt_ref[tl, hd] = kd[rows].T.astype(BF16)


def _ret_s_proj(x, cos, sin, gpre, win):
    return pl.pallas_call(
        _ret_s_proj_body,
        grid=(1,),
        in_specs=[_const((RS, D)), _const((RS, 128)), _const((RS, 128)), _const((1, D)),
                  _const((D, 2 * NQ + 2 * NV))],
        out_specs=[_full((RS, NQ)), _full((RS // CHUNK, RET_H, RET_DK, CHUNK)), _full((RS, NV)), _full((RS, NV)),
                   _full((RS, NV))],
        out_shape=[jax.ShapeDtypeStruct((RS, NQ), F32),
                   jax.ShapeDtypeStruct((RS // CHUNK, RET_H, RET_DK, CHUNK), BF16),
                   jax.ShapeDtypeStruct((RS, NV), F32), jax.ShapeDtypeStruct((RS, NV), F32),
                   jax.ShapeDtypeStruct((RS, NV), F32)],
        compiler_params=_params(("arbitrary",), 52),
        name="ret_s_proj",
    )(x, cos, sin, gpre, win)


def _ret_s_core_body(q_ref, kdt_ref, v_ref, oi_ref, s_ref, o_ref, so_ref):
    step = pl.program_id(0)
    pair_in_tile = step % (CHUNK // 8)
    row8 = lax.broadcasted_iota(jnp.int32, (8, RET_DK), 0)
    row128 = lax.broadcasted_iota(jnp.int32, (CHUNK, RET_DV), 0)
    t8 = (lax.broadcasted_iota(jnp.int32, (8, RET_DV), 0) & (TS - 1)).astype(F32)
    for hd in range(RET_H):
        lg = _LOG_G[hd]
        decay_l = float(np.exp(np.float32(lg) * np.float32(TS)))
        q8 = q_ref[:, hd * RET_DK:(hd + 1) * RET_DK]
        v128 = v_ref[:, hd * RET_DV:(hd + 1) * RET_DV]
        kdt = kdt_ref[hd]
        inter = jnp.zeros((8, RET_DV), F32)
        for bi in range(2):
            s_prev = s_ref[bi, hd]
            qm = jnp.where((row8 >> 2) == bi, q8, 0.0).astype(BF16)
            inter = inter + _dot(qm, s_prev.astype(BF16))
            vm = jnp.where((row128 >> 2) == pair_in_tile * 2 + bi, v128, 0.0).astype(BF16)
            so_ref[bi, hd] = decay_l * s_prev + _dot(kdt, vm)
        cols = slice(hd * RET_DV, (hd + 1) * RET_DV)
        o_ref[:, cols] = oi_ref[:, cols] + inter * jnp.exp(lg * (t8 + 1.0))


def _ret_s_core(q, kdt, v, oi, s):
    ppt = CHUNK // 8
    s_spec = pl.BlockSpec((2, RET_H, RET_DK, RET_DV), lambda i: (i, 0, 0, 0))
    return pl.pallas_call(
        _ret_s_core_body,
        grid=(NS // 2,),
        in_specs=[pl.BlockSpec((8, NQ), lambda i: (i, 0)),
                  pl.BlockSpec((None, RET_H, RET_DK, CHUNK), lambda i: (i // ppt, 0, 0, 0)),
                  pl.BlockSpec((CHUNK, NV), lambda i: (i // ppt, 0)),
                  pl.BlockSpec((8, NV), lambda i: (i, 0)),
                  s_spec],
        out_specs=[pl.BlockSpec((8, NV), lambda i: (i, 0)), s_spec],
        out_shape=[jax.ShapeDtypeStruct((RS, NV), F32), jax.ShapeDtypeStruct((NS, RET_H, RET_DK, RET_DV), F32)],
        compiler_params=_params(("arbitrary",), 40),
        name="ret_s_core",
    )(q, kdt, v, oi, s)


def _ret_s_out_body(x_ref, o_ref, g_ref, wout_ref, gpost_ref, xo_ref, y_ref):
    for hd in range(RET_H):
        cols = slice(hd * RET_DV, (hd + 1) * RET_DV)
        y_ref[:, cols] = _group_norm_gate(o_ref[:, cols], g_ref[:, cols]).astype(BF16)
    out = _dot(y_ref[...], wout_ref[...])
    xo_ref[...] = x_ref[...] + _rms(out, gpost_ref[...])


def _ret_s_out(x, o, g, wout, gpost):
    return pl.pallas_call(
        _ret_s_out_body,
        grid=(1,),
        in_specs=[_const((RS, D)), _const((RS, NV)), _const((RS, NV)), _const((NV, D)), _const((1, D))],
        out_specs=_full((RS, D)),
        out_shape=jax.ShapeDtypeStruct((RS, D), F32),
        scratch_shapes=[pltpu.VMEM((RS, NV), BF16)],
        compiler_params=_params(("arbitrary",), 40),
        name="ret_s_out",
    )(x, o, g, wout, gpost)


def _rope_tables(pos):
    half = RET_DK // 2
    inv = ROPE_BASE ** (-jnp.arange(half, dtype=F32) / half)
    ang = pos.astype(F32)[:, None] * inv[None, :]
    return jnp.cos(ang), jnp.sin(ang)


def _pad_state(st, keep_from):
    s = st[:, keep_from:, :]
    return jnp.pad(s, ((0, 0), (0, TS - s.shape[1]), (0, 0))).reshape(RS, st.shape[-1])


def kernel(x_prompt, x_sample, mem_prompt, cache_mem_k, cache_mem_v, state_shortconv, state_retention, state_ffn_conv, norm_mix_pre, norm_mix_post, w_in_even, sgu_vnorm, sgu_w, sgu_b, conv_short, w_out_even, w_in_odd, w_out_odd, norm_x_pre, norm_x_post, norm_mem, w_xq, w_xk, w_xv, w_xo, norm_ffn_pre, norm_ffn_post, w_ffn_up, conv_ffn, w_ffn_down):
    bf = lambda w: w.astype(BF16)
    row = lambda g: g.reshape(1, -1)
    win_e, wout_e = bf(w_in_even[0]), bf(w_out_even[0])
    win_o, wout_o = bf(w_in_odd[0]), bf(w_out_odd[0])
    wq, wk, wv, wo = bf(w_xq), bf(w_xk), bf(w_xv), bf(w_xo)
    wup, wdn = bf(w_ffn_up), bf(w_ffn_down)

    mem_k, mem_v, mem_kb, mem_vb = _memkv(mem_prompt.reshape(NB * N_MEM, D), norm_mem.reshape(2, 1, D), wk, wv)
    xp = x_prompt.reshape(NB * SEQ, D)
    sb_full = jnp.repeat(sgu_b[0].T, CHUNK, axis=1)
    xp, sc_p = _even_p(xp, row(norm_mix_pre[0]), win_e, row(sgu_vnorm[0]), sgu_w[0], sb_full, conv_short[0], wout_e,
                       row(norm_mix_post[0]))
    cos_p, sin_p = _rope_tables(jnp.arange(SEQ, dtype=jnp.int32))
    ffn_p_states = []
    ret_p_state = None
    for l in range(2):
        if l == 1:
            xp, ret_p_state = _ret_p(xp, cos_p, sin_p, row(norm_mix_pre[1]), win_o, wout_o, row(norm_mix_post[1]))
        xp = _xattn_p(xp, row(norm_x_pre[l]), wq[l], mem_kb[l], mem_vb[l], wo[l], row(norm_x_post[l]))
        xp, st = _ffn_p(xp, row(norm_ffn_pre[l]), wup[l], conv_ffn[l], wdn[l], row(norm_ffn_post[l]))
        ffn_p_states.append(st[:, 6:8, :])

    xs = x_sample.reshape(RS, D)
    w4 = sgu_w[0][:, :TS, :TS]
    tt = jnp.arange(RS) % TS
    gtab = []
    for k in range(TS):
        src = jnp.clip(tt - k, 0, TS - 1)
        wk_rows = jnp.where((tt >= k)[None, :], w4[:, tt, src], 0.0)
        gtab.append(jnp.repeat(wk_rows.T, CHUNK, axis=1))
    gtab = jnp.stack(gtab)
    sbt = jnp.repeat(sgu_b[0][:, tt].T, CHUNK, axis=1)
    xs, p_s, v_s = _even_s(xs, row(norm_mix_pre[0]), win_e, row(sgu_vnorm[0]), gtab, sbt, conv_short[0],
                           _pad_state(state_shortconv[0], 1), _pad_state(state_shortconv[0], 0), wout_e,
                           row(norm_mix_post[0]))
    cos_s, sin_s = _rope_tables(PAST + (jnp.arange(RS, dtype=jnp.int32) % TS))
    ffn_s_states = []
    ret_s_state = None
    for l in range(2):
        if l == 1:
            q, kdt, v, g, oi = _ret_s_proj(xs, cos_s, sin_s, row(norm_mix_pre[1]), win_o)
            o, ret_s_state = _ret_s_core(q, kdt, v, oi, state_retention[0])
            xs = _ret_s_out(xs, o, g, wout_o, row(norm_mix_post[1]))
        xs = _xattn_s(xs, row(norm_x_pre[l]), wq[l], cache_mem_k[l].reshape(NS, N_MEM, D),
                      cache_mem_v[l].reshape(NS, N_MEM, D), wo[l], row(norm_x_post[l]))
        xs, ug, uv = _ffn_s(xs, row(norm_ffn_pre[l]), wup[l], conv_ffn[l], _pad_state(state_ffn_conv[l], 1),
                            _pad_state(state_ffn_conv[l], 0), wdn[l], row(norm_ffn_post[l]))
        up = jnp.concatenate([ug, uv], axis=1).reshape(NS, TS, 2 * DFF)
        ffn_s_states.append(up[:, 2:4, :])

    return (xp.reshape(NB, SEQ, D), xs.reshape(NS, TS, D),
            mem_k.reshape(2, NB, N_MEM, XH, XHD), mem_v.reshape(2, NB, N_MEM, XH, XHD),
            sc_p[None, :, 6:8, :], p_s.reshape(NS, TS, SC_W)[None, :, 2:4, :],
            v_s.reshape(1, NS, TS, SGU_W),
            ret_p_state[None], ret_s_state[None],
            jnp.stack(ffn_p_states), jnp.stack(ffn_s_states))
```

```python
import functools

import numpy as np
import jax
import jax.numpy as jnp
from jax import lax
from jax.experimental import pallas as pl
from jax.experimental.pallas import tpu as pltpu

F32 = jnp.float32
BF16 = jnp.bfloat16

D = 1024
SEQ = 2048
NB = 8
NS = 128
TS = 4
RS = NS * TS
PAST = 16384
CHUNK = 128
EPS = 1e-6
SGU_W = 512
SC_W = 512
RET_H = 4
RET_DK = 256
RET_DV = 512
NQ = RET_H * RET_DK
NV = RET_H * RET_DV
N_MEM = 256
XH = 4
XHD = 256
DFF = 2816
FC = 256
NFC = DFF // FC
ROPE_BASE = 10000.0

_LOG_G = [float(v) for v in np.log1p(-np.exp2(np.float32(-5.0) - np.arange(RET_H, dtype=np.float32))).astype(np.float32)]

_MIB = 1024 * 1024


def _rms(x, g):
    ms = jnp.mean(x * x, axis=-1, keepdims=True)
    return (x * lax.rsqrt(ms + EPS)) * g


_GELU_C = 0.7978845608028654
_LOG2E = 1.4426950408889634


def _gelu(x):
    k0 = -2.0 * _LOG2E * _GELU_C
    k1 = k0 * 0.044715
    return x * (1.0 / (1.0 + jnp.exp2(x * (k0 + k1 * (x * x)))))


def _dot(a, b):
    return jnp.dot(a, b, preferred_element_type=F32)


def _dot_nt(a, b):
    return lax.dot_general(a, b, (((1,), (1,)), ((), ())), preferred_element_type=F32)


def _const(shape):
    n = len(shape)
    return pl.BlockSpec(shape, lambda *_: (0,) * n, pipeline_mode=pl.Buffered(1))


def _full(shape):
    n = len(shape)
    return pl.BlockSpec(shape, lambda *_: (0,) * n)


def _params(sem, vmem_mib):
    return pltpu.CompilerParams(dimension_semantics=sem, vmem_limit_bytes=vmem_mib * _MIB)


def _conv3(p, p1, p2, w_ref, cols):
    return w_ref[0:1, cols] * p2 + w_ref[1:2, cols] * p1 + w_ref[2:3, cols] * p


def _shift_down(a, first_row):
    r = pltpu.roll(a, 1, 0)
    row8 = lax.broadcasted_iota(jnp.int32, (8, a.shape[1]), 0)
    top = jnp.where(row8 == 0, first_row, r[0:8])
    return jnp.concatenate([top, r[8:]], axis=0)


def _conv_tile(u, prev_ref, w_ref, cols):
    w0, w1, w2 = w_ref[0:1, cols], w_ref[1:2, cols], w_ref[2:3, cols]
    m1 = prev_ref[7:8, cols]
    m2 = prev_ref[6:7, cols]
    inner = _shift_down(w0 * u, w0 * m1) + w1 * u
    return _shift_down(inner, w0 * m2 + w1 * m1) + w2 * u


def _conv_rows(u, prev_ref, w_ref, cols, sh_ref):
    r, c = u.shape
    ys = []
    for j in range(c // 128):
        cj = slice(cols.start + j * 128, cols.start + (j + 1) * 128)
        uj = u[:, j * 128:(j + 1) * 128]
        sh_ref[j, 0:8, :] = prev_ref[:, cj]
        sh_ref[j, 8:r + 8, :] = uj
        u1 = sh_ref[j, 7:r + 7, :]
        u2 = sh_ref[j, 6:r + 6, :]
        ys.append((w_ref[0:1, cj] * u2 + w_ref[1:2, cj] * u1) + w_ref[2:3, cj] * uj)
    return jnp.concatenate(ys, axis=1)


def _shift_sample(p, e1, e2):
    t = lax.broadcasted_iota(jnp.int32, p.shape, 0) & (TS - 1)
    p1 = jnp.where(t >= 1, pltpu.roll(p, 1, 0), e1)
    p2 = jnp.where(t >= 2, pltpu.roll(p, 2, 0), e2)
    return p1, p2


def _memkv_body(mem_ref, g_ref, wk_ref, wv_ref, k_ref, v_ref, kb_ref, vb_ref, *, nb):
    mn = _rms(mem_ref[...], g_ref[...]).astype(BF16)
    for w_ref, o_ref, ob_ref in ((wk_ref, k_ref, kb_ref), (wv_ref, v_ref, vb_ref)):
        y = _dot(mn, w_ref[...])
        ob_ref[...] = y.astype(BF16)
        for b in range(nb):
            for hd in range(XH):
                for j in range(2):
                    c0 = hd * XHD + j * 128
                    o_ref[b, pl.ds(j * XH + hd, N_MEM, stride=8), :] = y[b * N_MEM:(b + 1) * N_MEM, c0:c0 + 128]


def _memkv(mem, g, wk, wv):
    rows = mem.shape[0]
    nb = 2
    tm = nb * N_MEM
    o_spec = pl.BlockSpec((None, nb, N_MEM * 8, 128), lambda l, i: (l, i, 0, 0))
    ob_spec = pl.BlockSpec((None, tm, D), lambda l, i: (l, i, 0))
    w_spec = pl.BlockSpec((None, D, D), lambda l, i: (l, 0, 0))
    return pl.pallas_call(
        functools.partial(_memkv_body, nb=nb),
        grid=(2, rows // tm),
        in_specs=[pl.BlockSpec((tm, D), lambda l, i: (i, 0)),
                  pl.BlockSpec((None, 1, D), lambda l, i: (l, 0, 0)),
                  w_spec, w_spec],
        out_specs=[o_spec, o_spec, ob_spec, ob_spec],
        out_shape=[jax.ShapeDtypeStruct((2, NB, N_MEM * 8, 128), F32)] * 2
        + [jax.ShapeDtypeStruct((2, rows, D), BF16)] * 2,
        compiler_params=_params(("arbitrary", "arbitrary"), 40),
        name="memkv",
    )(mem, g, wk, wv)


def _from_head_view(c):
    s = c.shape
    return c.reshape(s[0], s[1], N_MEM, 2, XH, 128).transpose(0, 1, 2, 4, 3, 5).reshape(s[0], s[1], N_MEM, XH, XHD)


def _even_front(x, gpre_ref, win_ref, vn_ref):
    h = _rms(x, gpre_ref[...]).astype(BF16)
    u = _gelu(_dot(h, win_ref[:, 0:512]))
    v = _rms(_gelu(_dot(h, win_ref[:, 512:1024])), vn_ref[...])
    bg = _dot(h, win_ref[:, 1024:1536])
    p = _dot(h, win_ref[:, 1536:2048]) * _dot(h, win_ref[:, 2048:2560])
    return u, v, bg, p


def _even_p_body(x_ref, gpre_ref, win_ref, vn_ref, sw_ref, sb_ref, cw_ref, wout_ref, gpost_ref,
                 xo_ref, st_ref, carry_ref, cat_ref, sh_ref, *, tm):
    @pl.when(pl.program_id(1) == 0)
    def _():
        carry_ref[...] = jnp.zeros_like(carry_ref)

    x = x_ref[...]
    u, v, bg, p = _even_front(x, gpre_ref, win_ref, vn_ref)
    vb = v.astype(BF16)
    ri = lax.broadcasted_iota(jnp.int32, (CHUNK, CHUNK), 0)
    ci = lax.broadcasted_iota(jnp.int32, (CHUNK, CHUNK), 1)
    for g in range(4):
        cols = slice(g * 128, (g + 1) * 128)
        w = jnp.where(ri >= ci, sw_ref[g], 0.0).astype(BF16)
        for c in range(tm // CHUNK):
            rows = slice(c * CHUNK, (c + 1) * CHUNK)
            mixed = _dot(w, vb[rows, cols]) + sb_ref[:, cols]
            cat_ref[rows, cols] = (u[rows, cols] * mixed).astype(BF16)
    cz = _conv_rows(p, carry_ref, cw_ref, slice(0, SC_W), sh_ref)
    cat_ref[:, 512:1024] = (bg * cz).astype(BF16)
    carry_ref[...] = p[tm - 8:tm, :]
    st_ref[...] = p[tm - 8:tm, :]
    out = _dot(cat_ref[...], wout_ref[...])
    xo_ref[...] = x + _rms(out, gpost_ref[...])


def _even_p(x, gpre, win, vn, sw, sb, cw, wout, gpost):
    tm = 512
    nt = SEQ // tm
    row_spec = pl.BlockSpec((tm, D), lambda b, t: (b * nt + t, 0))
    return pl.pallas_call(
        functools.partial(_even_p_body, tm=tm),
        grid=(NB, nt),
        in_specs=[row_spec, _const((1, D)), _const((D, 2560)), _const((1, SGU_W)), _const((4, CHUNK, CHUNK)),
                  _const((CHUNK, SGU_W)), _const((3, SC_W)), _const((D, D)), _const((1, D))],
        out_specs=[row_spec, pl.BlockSpec((None, 8, SC_W), lambda b, t: (b, 0, 0))],
        out_shape=[jax.ShapeDtypeStruct((NB * SEQ, D), F32), jax.ShapeDtypeStruct((NB, 8, SC_W), F32)],
        scratch_shapes=[pltpu.VMEM((8, SC_W), F32), pltpu.VMEM((tm, D), BF16),
                        pltpu.VMEM((SC_W // 128, tm + 8, 128), F32)],
        compiler_params=_params(("arbitrary", "arbitrary"), 40),
        name="even_p",
    )(x, gpre, win, vn, sw, sb, cw, wout, gpost)


def _even_s_body(x_ref, gpre_ref, win_ref, vn_ref, gt_ref, sb_ref, cw_ref, e1_ref, e2_ref, wout_ref, gpost_ref,
                 xo_ref, p_ref, v_ref, cat_ref):
    x = x_ref[...]
    u, v, bg, p = _even_front(x, gpre_ref, win_ref, vn_ref)
    v_ref[...] = v
    p_ref[...] = p
    mixed = sb_ref[...] + gt_ref[0] * v
    for k in range(1, TS):
        mixed = mixed + gt_ref[k] * pltpu.roll(v, k, 0)
    cat_ref[:, 0:512] = (u * mixed).astype(BF16)
    p1, p2 = _shift_sample(p, e1_ref[...], e2_ref[...])
    cz = _conv3(p, p1, p2, cw_ref, slice(0, SC_W))
    cat_ref[:, 512:1024] = (bg * cz).astype(BF16)
    out = _dot(cat_ref[...], wout_ref[...])
    xo_ref[...] = x + _rms(out, gpost_ref[...])


def _even_s(x, gpre, win, vn, gtab, sbt, cw, e1, e2, wout, gpost):
    return pl.pallas_call(
        _even_s_body,
        grid=(1,),
        in_specs=[_const((RS, D)), _const((1, D)), _const((D, 2560)), _const((1, SGU_W)), _const((TS, RS, SGU_W)),
                  _const((RS, SGU_W)), _const((3, SC_W)), _const((RS, SC_W)), _const((RS, SC_W)), _const((D, D)),
                  _const((1, D))],
        out_specs=[_full((RS, D)), _full((RS, SC_W)), _full((RS, SGU_W))],
        out_shape=[jax.ShapeDtypeStruct((RS, D), F32), jax.ShapeDtypeStruct((RS, SC_W), F32),
                   jax.ShapeDtypeStruct((RS, SGU_W), F32)],
        scratch_shapes=[pltpu.VMEM((RS, D), BF16)],
        compiler_params=_params(("arbitrary",), 48),
        name="even_s",
    )(x, gpre, win, vn, gtab, sbt, cw, e1, e2, wout, gpost)


def _softmax_rows(s):
    m = jnp.max(s, axis=-1, keepdims=True)
    e = jnp.exp(s - m)
    return e * (1.0 / jnp.sum(e, axis=-1, keepdims=True))


def _xattn_p_body(x_ref, gpre_ref, wq_ref, k_ref, v_ref, wo_ref, gpost_ref, xo_ref, q_ref, s_ref, p_ref, o_ref):
    h = _rms(x_ref[...], gpre_ref[...]).astype(BF16)
    q_ref[...] = _dot(h, wq_ref[...]).astype(BF16)
    for hd in range(XH):
        cols = slice(hd * XHD, (hd + 1) * XHD)
        s_ref[hd] = _dot_nt(q_ref[:, cols], k_ref[:, cols]) * (XHD ** -0.5)
    p_ref[...] = _softmax_rows(s_ref[...]).astype(BF16)
    for hd in range(XH):
        cols = slice(hd * XHD, (hd + 1) * XHD)
        o_ref[:, cols] = _dot(p_ref[hd], v_ref[:, cols]).astype(BF16)
    out = _dot(o_ref[...], wo_ref[...])
    xo_ref[...] = x_ref[...] + _rms(out, gpost_ref[...])


def _xattn_p(x, gpre, wq, kb, vb, wo, gpost):
    tm = 512
    nt = SEQ // tm
    row_spec = pl.BlockSpec((tm, D), lambda b, t: (b * nt + t, 0))
    kv_spec = pl.BlockSpec((N_MEM, D), lambda b, t: (b, 0))
    return pl.pallas_call(
        _xattn_p_body,
        grid=(NB, nt),
        in_specs=[row_spec, _const((1, D)), _const((D, D)), kv_spec, kv_spec, _const((D, D)), _const((1, D))],
        out_specs=row_spec,
        out_shape=jax.ShapeDtypeStruct((NB * SEQ, D), F32),
        scratch_shapes=[pltpu.VMEM((tm, D), BF16), pltpu.VMEM((XH, tm, N_MEM), F32),
                        pltpu.VMEM((XH, tm, N_MEM), BF16), pltpu.VMEM((tm, D), BF16)],
        compiler_params=_params(("arbitrary", "arbitrary"), 40),
        name="xattn_p",
    )(x, gpre, wq, kb, vb, wo, gpost)


def _head_view(c):
    s = c.shape
    return c.reshape(s[0], s[1], N_MEM, XH, 2, 128).transpose(0, 1, 2, 4, 3, 5).reshape(s[0], s[1], N_MEM * 8, 128)


def _head_rows(ref, b, hd):
    halves = [ref[b, pl.ds(j * XH + hd, N_MEM, stride=8), :] for j in range(2)]
    return jnp.concatenate(halves, axis=1).astype(BF16)


def _xattn_s_body(x_ref, gpre_ref, wq_ref, k_ref, v_ref, wo_ref, gpost_ref, xo_ref, q_ref, o_ref, s_ref, *, bb):
    step = pl.program_id(0)

    @pl.when(step == 0)
    def _():
        h = _rms(x_ref[...], gpre_ref[...]).astype(BF16)
        q_ref[...] = _dot(h, wq_ref[...])

    first = (lax.broadcasted_iota(jnp.int32, (8, XHD), 0) >> 2) == 0
    groups = [(pi, hd, bi) for pi in range(bb // 2) for hd in range(XH) for bi in range(2)]
    row0 = [pl.multiple_of((step * (bb // 2) + pi) * 8, 8) for pi in range(bb // 2)]
    for gi, (pi, hd, bi) in enumerate(groups):
        q8 = q_ref[pl.ds(row0[pi], 8), hd * XHD:(hd + 1) * XHD].astype(BF16)
        kh = _head_rows(k_ref, pi * 2 + bi, hd)
        s_ref[gi * 8:(gi + 1) * 8, :] = _dot_nt(q8, kh) * (XHD ** -0.5)
    s_ref[...] = _softmax_rows(s_ref[...])
    for gi, (pi, hd, bi) in enumerate(groups):
        if bi == 1:
            continue
        pv = [_dot(s_ref[(gi + b) * 8:(gi + b + 1) * 8, :].astype(BF16), _head_rows(v_ref, pi * 2 + b, hd))
              for b in range(2)]
        o_ref[pl.ds(row0[pi], 8), hd * XHD:(hd + 1) * XHD] = jnp.where(first, pv[0], pv[1])

    @pl.when(step == pl.num_programs(0) - 1)
    def _():
        out = _dot(o_ref[...].astype(BF16), wo_ref[...])
        xo_ref[...] = x_ref[...] + _rms(out, gpost_ref[...])


def _xattn_s(l, x, gpre, wq, k, v, wo, gpost):
    bb = 4
    kv_spec = pl.BlockSpec((None, bb, N_MEM * 8, 128), lambda i: (l, i, 0, 0))
    return pl.pallas_call(
        functools.partial(_xattn_s_body, bb=bb),
        grid=(NS // bb,),
        in_specs=[_const((RS, D)), _const((1, D)), _const((D, D)), kv_spec, kv_spec, _const((D, D)), _const((1, D))],
        out_specs=_full((RS, D)),
        out_shape=jax.ShapeDtypeStruct((RS, D), F32),
        scratch_shapes=[pltpu.VMEM((RS, D), F32), pltpu.VMEM((RS, D), F32), pltpu.VMEM((bb * XH * 8, N_MEM), F32)],
        compiler_params=_params(("arbitrary",), 48),
        name="xattn_s",
    )(x, gpre, wq, k, v, wo, gpost)


_DOWN_SPLITS = (0, NFC)


def _ffn_p_body(x_ref, gpre_ref, wup_ref, cw_ref, wdn_ref, gpost_ref, xo_ref, st_ref, carry_ref, h_ref, act_ref,
                *sh_refs, tm):
    @pl.when(pl.program_id(1) == 0)
    def _():
        carry_ref[...] = jnp.zeros_like(carry_ref)

    h_ref[...] = _rms(x_ref[...], gpre_ref[...]).astype(BF16)
    for c in range(NFC):
        halves = []
        for i, base in enumerate((0, DFF)):
            cols = slice(base + c * FC, base + (c + 1) * FC)
            up = _dot(h_ref[...], wup_ref[:, cols])
            halves.append(_conv_rows(up, carry_ref, cw_ref, cols, sh_refs[(2 * c + i) % len(sh_refs)]))
            carry_ref[:, cols] = up[tm - 8:tm, :]
            st_ref[:, cols] = up[tm - 8:tm, :]
        act_ref[:, c * FC:(c + 1) * FC] = (_gelu(halves[0]) * halves[1]).astype(BF16)
    out = _dot(act_ref[...], wdn_ref[...])
    xo_ref[...] = x_ref[...] + _rms(out, gpost_ref[...])


def _ffn_p(x, gpre, wup, cw, wdn, gpost):
    tm = 512
    nt = SEQ // tm
    row_spec = pl.BlockSpec((tm, D), lambda b, t: (b * nt + t, 0))
    return pl.pallas_call(
        functools.partial(_ffn_p_body, tm=tm),
        grid=(NB, nt),
        in_specs=[row_spec, _const((1, D)), _const((D, 2 * DFF)), _const((3, 2 * DFF)), _const((DFF, D)),
                  _const((1, D))],
        out_specs=[row_spec, pl.BlockSpec((None, 8, 2 * DFF), lambda b, t: (b, 0, 0))],
        out_shape=[jax.ShapeDtypeStruct((NB * SEQ, D), F32), jax.ShapeDtypeStruct((NB, 8, 2 * DFF), F32)],
        scratch_shapes=[pltpu.VMEM((8, 2 * DFF), F32), pltpu.VMEM((tm, D), BF16), pltpu.VMEM((tm, DFF), BF16)]
        + [pltpu.VMEM((FC // 128, tm + 8, 128), F32)] * 4,
        compiler_params=_params(("arbitrary", "arbitrary"), 52),
        name="ffn_p",
    )(x, gpre, wup, cw, wdn, gpost)


def _ffn_s_body(x_ref, gpre_ref, wg_ref, wv_ref, cg_ref, cv_ref, e1g_ref, e2g_ref, e1v_ref, e2v_ref, wdn_ref, gpost_ref,
                xo_ref, ug_ref, uv_ref, h_ref, acc_ref):
    c = pl.program_id(0)

    @pl.when(c == 0)
    def _():
        h_ref[...] = _rms(x_ref[...], gpre_ref[...]).astype(BF16)
        acc_ref[...] = jnp.zeros_like(acc_ref)

    h = h_ref[...]
    allc = slice(0, FC)
    ug = _dot(h, wg_ref[...])
    uv = _dot(h, wv_ref[...])
    ug_ref[...] = ug
    uv_ref[...] = uv
    g1, g2 = _shift_sample(ug, e1g_ref[...], e2g_ref[...])
    v1, v2 = _shift_sample(uv, e1v_ref[...], e2v_ref[...])
    act = (_gelu(_conv3(ug, g1, g2, cg_ref, allc)) * _conv3(uv, v1, v2, cv_ref, allc)).astype(BF16)
    acc_ref[...] += _dot(act, wdn_ref[...])

    @pl.when(c == pl.num_programs(0) - 1)
    def _():
        xo_ref[...] = x_ref[...] + _rms(acc_ref[...], gpost_ref[...])


def _ffn_s(x, gpre, wup, cw, e1, e2, wdn, gpost):
    gcol = lambda r: pl.BlockSpec((r, FC), lambda c: (0, c))
    vcol = lambda r: pl.BlockSpec((r, FC), lambda c: (0, NFC + c))
    return pl.pallas_call(
        _ffn_s_body,
        grid=(NFC,),
        in_specs=[_const((RS, D)), _const((1, D)), gcol(D), vcol(D), gcol(3), vcol(3), gcol(RS), gcol(RS), vcol(RS),
                  vcol(RS), pl.BlockSpec((FC, D), lambda c: (c, 0)), _const((1, D))],
        out_specs=[_full((RS, D)), gcol(RS), gcol(RS)],
        out_shape=[jax.ShapeDtypeStruct((RS, D), F32), jax.ShapeDtypeStruct((RS, DFF), F32),
                   jax.ShapeDtypeStruct((RS, DFF), F32)],
        scratch_shapes=[pltpu.VMEM((RS, D), BF16), pltpu.VMEM((RS, D), F32)],
        compiler_params=_params(("arbitrary",), 40),
        name="ffn_s",
    )(x, gpre, wup, wup, cw, cw, e1, e2, e1, e2, wdn, gpost)


def _rope_head(z, cos, sin, base):
    x1 = z[:, base:base + 128]
    x2 = z[:, base + 128:base + 256]
    return x1 * cos - x2 * sin, x1 * sin + x2 * cos


def _group_norm_gate(o, g):
    mu = jnp.mean(o, axis=-1, keepdims=True)
    d = o - mu
    var = jnp.mean(d * d, axis=-1, keepdims=True)
    return (g * jax.nn.sigmoid(g)) * (d * lax.rsqrt(var + EPS))


RCH = 256


def _ret_p_body(x_ref, cos_ref, sin_ref, gpre_ref, win_ref, wout_ref, gpost_ref, xo_ref, s_ref,
                h_ref, q_ref, k_ref, kd_ref, v_ref, g_ref, y_ref, dm_ref, qd_ref, kdec_ref, *, tm):
    first = (pl.program_id(0) == 0) & (pl.program_id(1) == 0)

    @pl.when(first)
    def _():
        i = lax.broadcasted_iota(jnp.int32, (RCH, RCH), 0).astype(F32)
        j = lax.broadcasted_iota(jnp.int32, (RCH, RCH), 1).astype(F32)
        i1 = lax.broadcasted_iota(jnp.int32, (RCH, 128), 0).astype(F32)
        for hd in range(RET_H):
            lg = _LOG_G[hd]
            dm_ref[hd] = jnp.where(i >= j, jnp.exp(lg * jnp.maximum(i - j, 0.0)), 0.0)
            qd_ref[hd] = jnp.exp(lg * (i1 + 1.0))
            kdec_ref[hd] = jnp.exp(lg * (RCH - 1.0 - i1))

    @pl.when(pl.program_id(1) == 0)
    def _():
        s_ref[...] = jnp.zeros_like(s_ref)

    h_ref[...] = _rms(x_ref[...], gpre_ref[...]).astype(BF16)
    cos = cos_ref[...]
    sin = sin_ref[...]
    nchunk = tm // RCH
    for hd in range(RET_H):
        base = hd * RET_DK
        zq = _dot(h_ref[...], win_ref[:, base:base + RET_DK])
        zk = _dot(h_ref[...], win_ref[:, NQ + base:NQ + base + RET_DK])
        q1, q2 = _rope_head(zq, cos, sin, 0)
        k1, k2 = _rope_head(zk, cos, sin, 0)
        q_ref[:, base:base + 128] = (q1 * (RET_DK ** -0.5)).astype(BF16)
        q_ref[:, base + 128:base + 256] = (q2 * (RET_DK ** -0.5)).astype(BF16)
        k_ref[:, base:base + 128] = k1.astype(BF16)
        k_ref[:, base + 128:base + 256] = k2.astype(BF16)
        kdec = jnp.concatenate([kdec_ref[hd]] * nchunk, axis=0)
        kd_ref[:, base:base + 128] = k1 * kdec
        kd_ref[:, base + 128:base + 256] = k2 * kdec
        vcols = slice(2 * NQ + hd * RET_DV, 2 * NQ + (hd + 1) * RET_DV)
        gcols = slice(2 * NQ + NV + hd * RET_DV, 2 * NQ + NV + (hd + 1) * RET_DV)
        v_ref[:, hd * RET_DV:(hd + 1) * RET_DV] = _dot(h_ref[...], win_ref[:, vcols]).astype(BF16)
        g_ref[:, hd * RET_DV:(hd + 1) * RET_DV] = _dot(h_ref[...], win_ref[:, gcols])
    for hd in range(RET_H):
        kcols = slice(hd * RET_DK, (hd + 1) * RET_DK)
        vcols = slice(hd * RET_DV, (hd + 1) * RET_DV)
        decay_l = float(np.exp(np.float32(_LOG_G[hd]) * np.float32(RCH)))
        qd = qd_ref[hd]
        qd4 = jnp.concatenate([qd, qd, qd, qd], axis=1)
        for c in range(nchunk):
            rows = slice(c * RCH, (c + 1) * RCH)
            qc = q_ref[rows, kcols]
            vc = v_ref[rows, vcols]
            s_prev = s_ref[hd]
            sc = (_dot_nt(qc, k_ref[rows, kcols]) * dm_ref[hd]).astype(BF16)
            o = _dot(sc, vc) + _dot(qc, s_prev.astype(BF16)) * qd4
            kdt = kd_ref[rows, kcols].T.astype(BF16)
            s_ref[hd] = decay_l * s_prev + _dot(kdt, vc)
            y_ref[rows, vcols] = _group_norm_gate(o, g_ref[rows, vcols]).astype(BF16)
    out = _dot(y_ref[...], wout_ref[...])
    xo_ref[...] = x_ref[...] + _rms(out, gpost_ref[...])


def _ret_p(x, cos, sin, gpre, win, wout, gpost):
    tm = 512
    nt = SEQ // tm
    row_spec = pl.BlockSpec((tm, D), lambda b, t: (b * nt + t, 0))
    cs_spec = pl.BlockSpec((tm, 128), lambda b, t: (t, 0))
    return pl.pallas_call(
        functools.partial(_ret_p_body, tm=tm),
        grid=(NB, nt),
        in_specs=[row_spec, cs_spec, cs_spec, _const((1, D)), _const((D, 2 * NQ + 2 * NV)), _const((NV, D)),
                  _const((1, D))],
        out_specs=[row_spec, pl.BlockSpec((None, RET_H, RET_DK, RET_DV), lambda b, t: (b, 0, 0, 0))],
        out_shape=[jax.ShapeDtypeStruct((NB * SEQ, D), F32), jax.ShapeDtypeStruct((NB, RET_H, RET_DK, RET_DV), F32)],
        scratch_shapes=[pltpu.VMEM((tm, D), BF16), pltpu.VMEM((tm, NQ), BF16), pltpu.VMEM((tm, NQ), BF16),
                        pltpu.VMEM((tm, NQ), F32), pltpu.VMEM((tm, NV), BF16), pltpu.VMEM((tm, NV), F32),
                        pltpu.VMEM((tm, NV), BF16), pltpu.VMEM((RET_H, RCH, RCH), F32),
                        pltpu.VMEM((RET_H, RCH, 128), F32), pltpu.VMEM((RET_H, RCH, 128), F32)],
        compiler_params=_params(("arbitrary", "arbitrary"), 56),
        name="ret_p",
    )(x, cos, sin, gpre, win, wout, gpost)


def _ret_s_proj_body(x_ref, cos_ref, sin_ref, gpre_ref, win_ref, q_ref, kdt_ref, v_ref, g_ref, oi_ref):
    h = _rms(x_ref[...], gpre_ref[...]).astype(BF16)
    cos = cos_ref[...]
    sin = sin_ref[...]
    ti = (lax.broadcasted_iota(jnp.int32, (RS, 128), 0) & (TS - 1)).astype(F32)
    i = lax.broadcasted_iota(jnp.int32, (CHUNK, CHUNK), 0)
    j = lax.broadcasted_iota(jnp.int32, (CHUNK, CHUNK), 1)
    same = ((i >> 2) == (j >> 2)) & (i >= j)
    dij = jnp.maximum(i - j, 0).astype(F32)
    for hd in range(RET_H):
        lg = _LOG_G[hd]
        base = hd * RET_DK
        zq = _dot(h, win_ref[:, base:base + RET_DK])
        zk = _dot(h, win_ref[:, NQ + base:NQ + base + RET_DK])
        q1, q2 = _rope_head(zq, cos, sin, 0)
        k1, k2 = _rope_head(zk, cos, sin, 0)
        q = jnp.concatenate([q1, q2], axis=1) * (RET_DK ** -0.5)
        k = jnp.concatenate([k1, k2], axis=1)
        q_ref[:, base:base + RET_DK] = q
        kdec = jnp.exp(lg * (TS - 1.0 - ti))
        kd = k * jnp.concatenate([kdec, kdec], axis=1)
        vcols = slice(2 * NQ + hd * RET_DV, 2 * NQ + (hd + 1) * RET_DV)
        gcols = slice(2 * NQ + NV + hd * RET_DV, 2 * NQ + NV + (hd + 1) * RET_DV)
        vf = _dot(h, win_ref[:, vcols])
        v_ref[:, hd * RET_DV:(hd + 1) * RET_DV] = vf
        v = vf.astype(BF16)
        g_ref[:, hd * RET_DV:(hd + 1) * RET_DV] = _dot(h, win_ref[:, gcols])
        dm = jnp.where(same, jnp.exp(lg * dij), 0.0)
        qb = q.astype(BF16)
        kb = k.astype(BF16)
        for tl in range(RS // CHUNK):
            rows = slice(tl * CHUNK, (tl + 1) * CHUNK)
            sc = (_dot_nt(qb[rows], kb[rows]) * dm).astype(BF16)
            oi_ref[rows, hd * RET_DV:(hd + 1) * RET_DV] = _dot(sc, v[rows])
            kdt_ref[tl, hd] = kd[rows].T.astype(BF16)


def _ret_s_proj(x, cos, sin, gpre, win):
    return pl.pallas_call(
        _ret_s_proj_body,
        grid=(1,),
        in_specs=[_const((RS, D)), _const((RS, 128)), _const((RS, 128)), _const((1, D)),
                  _const((D, 2 * NQ + 2 * NV))],
        out_specs=[_full((RS, NQ)), _full((RS // CHUNK, RET_H, RET_DK, CHUNK)), _full((RS, NV)), _full((RS, NV)),
                   _full((RS, NV))],
        out_shape=[jax.ShapeDtypeStruct((RS, NQ), F32),
                   jax.ShapeDtypeStruct((RS // CHUNK, RET_H, RET_DK, CHUNK), BF16),
                   jax.ShapeDtypeStruct((RS, NV), F32), jax.ShapeDtypeStruct((RS, NV), F32),
                   jax.ShapeDtypeStruct((RS, NV), F32)],
        compiler_params=_params(("arbitrary",), 52),
        name="ret_s_proj",
    )(x, cos, sin, gpre, win)


def _ret_s_core_body(q_ref, kdt_ref, v_ref, oi_ref, s_ref, o_ref, so_ref):
    step = pl.program_id(0)
    pair_in_tile = step % (CHUNK // 8)
    row8 = lax.broadcasted_iota(jnp.int32, (8, RET_DK), 0)
    row128 = lax.broadcasted_iota(jnp.int32, (CHUNK, RET_DV), 0)
    t8 = (lax.broadcasted_iota(jnp.int32, (8, RET_DV), 0) & (TS - 1)).astype(F32)
    for hd in range(RET_H):
        lg = _LOG_G[hd]
        decay_l = float(np.exp(np.float32(lg) * np.float32(TS)))
        q8 = q_ref[:, hd * RET_DK:(hd + 1) * RET_DK]
        v128 = v_ref[:, hd * RET_DV:(hd + 1) * RET_DV]
        kdt = kdt_ref[hd]
        inter = jnp.zeros((8, RET_DV), F32)
        for bi in range(2):
            s_prev = s_ref[bi, hd]
            qm = jnp.where((row8 >> 2) == bi, q8, 0.0).astype(BF16)
            inter = inter + _dot(qm, s_prev.astype(BF16))
            vm = jnp.where((row128 >> 2) == pair_in_tile * 2 + bi, v128, 0.0).astype(BF16)
            so_ref[bi, hd] = decay_l * s_prev + _dot(kdt, vm)
        cols = slice(hd * RET_DV, (hd + 1) * RET_DV)
        o_ref[:, cols] = oi_ref[:, cols] + inter * jnp.exp(lg * (t8 + 1.0))


def _ret_s_core(q, kdt, v, oi, s):
    ppt = CHUNK // 8
    s_spec = pl.BlockSpec((2, RET_H, RET_DK, RET_DV), lambda i: (i, 0, 0, 0))
    return pl.pallas_call(
        _ret_s_core_body,
        grid=(NS // 2,),
        in_specs=[pl.BlockSpec((8, NQ), lambda i: (i, 0)),
                  pl.BlockSpec((None, RET_H, RET_DK, CHUNK), lambda i: (i // ppt, 0, 0, 0)),
                  pl.BlockSpec((CHUNK, NV), lambda i: (i // ppt, 0)),
                  pl.BlockSpec((8, NV), lambda i: (i, 0)),
                  s_spec],
        out_specs=[pl.BlockSpec((8, NV), lambda i: (i, 0)), s_spec],
        out_shape=[jax.ShapeDtypeStruct((RS, NV), F32), jax.ShapeDtypeStruct((NS, RET_H, RET_DK, RET_DV), F32)],
        compiler_params=_params(("arbitrary",), 40),
        name="ret_s_core",
    )(q, kdt, v, oi, s)


def _ret_s_out_body(x_ref, o_ref, g_ref, wout_ref, gpost_ref, xo_ref, y_ref):
    for hd in range(RET_H):
        cols = slice(hd * RET_DV, (hd + 1) * RET_DV)
        y_ref[:, cols] = _group_norm_gate(o_ref[:, cols], g_ref[:, cols]).astype(BF16)
    out = _dot(y_ref[...], wout_ref[...])
    xo_ref[...] = x_ref[...] + _rms(out, gpost_ref[...])


def _ret_s_out(x, o, g, wout, gpost):
    return pl.pallas_call(
        _ret_s_out_body,
        grid=(1,),
        in_specs=[_const((RS, D)), _const((RS, NV)), _const((RS, NV)), _const((NV, D)), _const((1, D))],
        out_specs=_full((RS, D)),
        out_shape=jax.ShapeDtypeStruct((RS, D), F32),
        scratch_shapes=[pltpu.VMEM((RS, NV), BF16)],
        compiler_params=_params(("arbitrary",), 40),
        name="ret_s_out",
    )(x, o, g, wout, gpost)


def _rope_tables(pos):
    half = RET_DK // 2
    inv = ROPE_BASE ** (-jnp.arange(half, dtype=F32) / half)
    ang = pos.astype(F32)[:, None] * inv[None, :]
    return jnp.cos(ang), jnp.sin(ang)


def _pad_state(st, keep_from):
    s = st[:, keep_from:, :]
    return jnp.pad(s, ((0, 0), (0, TS - s.shape[1]), (0, 0))).reshape(RS, st.shape[-1])


def kernel(x_prompt, x_sample, mem_prompt, cache_mem_k, cache_mem_v, state_shortconv, state_retention, state_ffn_conv, norm_mix_pre, norm_mix_post, w_in_even, sgu_vnorm, sgu_w, sgu_b, conv_short, w_out_even, w_in_odd, w_out_odd, norm_x_pre, norm_x_post, norm_mem, w_xq, w_xk, w_xv, w_xo, norm_ffn_pre, norm_ffn_post, w_ffn_up, conv_ffn, w_ffn_down):
    bf = lambda w: w.astype(BF16)
    row = lambda g: g.reshape(1, -1)
    win_e, wout_e = bf(w_in_even[0]), bf(w_out_even[0])
    win_o, wout_o = bf(w_in_odd[0]), bf(w_out_odd[0])
    wq, wk, wv, wo = bf(w_xq), bf(w_xk), bf(w_xv), bf(w_xo)
    wup, wdn = bf(w_ffn_up), bf(w_ffn_down)

    mem_k, mem_v, mem_kb, mem_vb = _memkv(mem_prompt.reshape(NB * N_MEM, D), norm_mem.reshape(2, 1, D), wk, wv)
    xp = x_prompt.reshape(NB * SEQ, D)
    sb_full = jnp.repeat(sgu_b[0].T, CHUNK, axis=1)
    xp, sc_p = _even_p(xp, row(norm_mix_pre[0]), win_e, row(sgu_vnorm[0]), sgu_w[0], sb_full, conv_short[0], wout_e,
                       row(norm_mix_post[0]))
    cos_p, sin_p = _rope_tables(jnp.arange(SEQ, dtype=jnp.int32))
    ffn_p_states = []
    ret_p_state = None
    for l in range(2):
        if l == 1:
            xp, ret_p_state = _ret_p(xp, cos_p, sin_p, row(norm_mix_pre[1]), win_o, wout_o, row(norm_mix_post[1]))
        xp = _xattn_p(xp, row(norm_x_pre[l]), wq[l], mem_kb[l], mem_vb[l], wo[l], row(norm_x_post[l]))
        xp, st = _ffn_p(xp, row(norm_ffn_pre[l]), wup[l], conv_ffn[l], wdn[l], row(norm_ffn_post[l]))
        ffn_p_states.append(st[:, 6:8, :])

    xs = x_sample.reshape(RS, D)
    w4 = sgu_w[0][:, :TS, :TS]
    tt = jnp.arange(RS) % TS
    gtab = []
    for k in range(TS):
        src = jnp.clip(tt - k, 0, TS - 1)
        wk_rows = jnp.where((tt >= k)[None, :], w4[:, tt, src], 0.0)
        gtab.append(jnp.repeat(wk_rows.T, CHUNK, axis=1))
    gtab = jnp.stack(gtab)
    sbt = jnp.repeat(sgu_b[0][:, tt].T, CHUNK, axis=1)
    xs, p_s, v_s = _even_s(xs, row(norm_mix_pre[0]), win_e, row(sgu_vnorm[0]), gtab, sbt, conv_short[0],
                           _pad_state(state_shortconv[0], 1), _pad_state(state_shortconv[0], 0), wout_e,
                           row(norm_mix_post[0]))
    cos_s, sin_s = _rope_tables(PAST + (jnp.arange(RS, dtype=jnp.int32) % TS))
    cache_k, cache_v = _head_view(cache_mem_k), _head_view(cache_mem_v)
    ffn_s_states = []
    ret_s_state = None
    for l in range(2):
        if l == 1:
            q, kdt, v, g, oi = _ret_s_proj(xs, cos_s, sin_s, row(norm_mix_pre[1]), win_o)
            o, ret_s_state = _ret_s_core(q, kdt, v, oi, state_retention[0])
            xs = _ret_s_out(xs, o, g, wout_o, row(norm_mix_post[1]))
        xs = _xattn_s(l, xs, row(norm_x_pre[l]), wq[l], cache_k, cache_v, wo[l], row(norm_x_post[l]))
        xs, ug, uv = _ffn_s(xs, row(norm_ffn_pre[l]), wup[l], conv_ffn[l], _pad_state(state_ffn_conv[l], 1),
                            _pad_state(state_ffn_conv[l], 0), wdn[l], row(norm_ffn_post[l]))
        up = jnp.concatenate([ug, uv], axis=1).reshape(NS, TS, 2 * DFF)
        ffn_s_states.append(up[:, 2:4, :])

    return (xp.reshape(NB, SEQ, D), xs.reshape(NS, TS, D),
            _from_head_view(mem_k), _from_head_view(mem_v),
            sc_p[None, :, 6:8, :], p_s.reshape(NS, TS, SC_W)[None, :, 2:4, :],
            v_s.reshape(1, NS, TS, SGU_W),
            ret_p_state[None], ret_s_state[None],
            jnp.stack(ffn_p_states), jnp.stack(ffn_s_states))
```

```python
import functools

import numpy as np
import jax
import jax.numpy as jnp
from jax import lax
from jax.experimental import pallas as pl
from jax.experimental.pallas import tpu as pltpu

F32 = jnp.float32
BF16 = jnp.bfloat16

D = 1024
SEQ = 2048
NB = 8
NS = 128
TS = 4
RS = NS * TS
PAST = 16384
CHUNK = 128
EPS = 1e-6
SGU_W = 512
SC_W = 512
RET_H = 4
RET_DK = 256
RET_DV = 512
NQ = RET_H * RET_DK
NV = RET_H * RET_DV
N_MEM = 256
XH = 4
XHD = 256
DFF = 2816
FC = 256
NFC = DFF // FC
ROPE_BASE = 10000.0

_LOG_G = [float(v) for v in np.log1p(-np.exp2(np.float32(-5.0) - np.arange(RET_H, dtype=np.float32))).astype(np.float32)]

_MIB = 1024 * 1024


def _rms(x, g):
    ms = jnp.mean(x * x, axis=-1, keepdims=True)
    return (x * lax.rsqrt(ms + EPS)) * g


_GELU_C = 0.7978845608028654
_LOG2E = 1.4426950408889634


def _gelu(x):
    k0 = -2.0 * _LOG2E * _GELU_C
    k1 = k0 * 0.044715
    return x * (1.0 / (1.0 + jnp.exp2(x * (k0 + k1 * (x * x)))))


def _dot(a, b):
    return jnp.dot(a, b, preferred_element_type=F32)


def _dot_nt(a, b):
    return lax.dot_general(a, b, (((1,), (1,)), ((), ())), preferred_element_type=F32)


def _const(shape):
    n = len(shape)
    return pl.BlockSpec(shape, lambda *_: (0,) * n, pipeline_mode=pl.Buffered(1))


def _layer_const(l, shape):
    n = len(shape)
    return pl.BlockSpec((None,) + tuple(shape), lambda *_: (l,) + (0,) * n, pipeline_mode=pl.Buffered(1))


def _full(shape):
    n = len(shape)
    return pl.BlockSpec(shape, lambda *_: (0,) * n)


def _params(sem, vmem_mib):
    return pltpu.CompilerParams(dimension_semantics=sem, vmem_limit_bytes=vmem_mib * _MIB)


def _conv3(p, p1, p2, w_ref, cols):
    return w_ref[0:1, cols] * p2 + w_ref[1:2, cols] * p1 + w_ref[2:3, cols] * p


def _shift_down(a, first_row):
    r = pltpu.roll(a, 1, 0)
    row8 = lax.broadcasted_iota(jnp.int32, (8, a.shape[1]), 0)
    top = jnp.where(row8 == 0, first_row, r[0:8])
    return jnp.concatenate([top, r[8:]], axis=0)


def _conv_tile(u, prev_ref, w_ref, cols):
    w0, w1, w2 = w_ref[0:1, cols], w_ref[1:2, cols], w_ref[2:3, cols]
    m1 = prev_ref[7:8, cols]
    m2 = prev_ref[6:7, cols]
    inner = _shift_down(w0 * u, w0 * m1) + w1 * u
    return _shift_down(inner, w0 * m2 + w1 * m1) + w2 * u


def _conv_rows(u, prev_ref, w_ref, cols, sh_ref):
    r, c = u.shape
    ys = []
    for j in range(c // 128):
        cj = slice(cols.start + j * 128, cols.start + (j + 1) * 128)
        uj = u[:, j * 128:(j + 1) * 128]
        sh_ref[j, 0:8, :] = prev_ref[:, cj]
        sh_ref[j, 8:r + 8, :] = uj
        u1 = sh_ref[j, 7:r + 7, :]
        u2 = sh_ref[j, 6:r + 6, :]
        ys.append((w_ref[0:1, cj] * u2 + w_ref[1:2, cj] * u1) + w_ref[2:3, cj] * uj)
    return jnp.concatenate(ys, axis=1)


def _shift_sample(p, e1, e2):
    t = lax.broadcasted_iota(jnp.int32, p.shape, 0) & (TS - 1)
    p1 = jnp.where(t >= 1, pltpu.roll(p, 1, 0), e1)
    p2 = jnp.where(t >= 2, pltpu.roll(p, 2, 0), e2)
    return p1, p2


def _memkv_body(mem_ref, g_ref, wk_ref, wv_ref, k_ref, v_ref, kb_ref, vb_ref, *, nb):
    mn = _rms(mem_ref[...], g_ref[...]).astype(BF16)
    for w_ref, o_ref, ob_ref in ((wk_ref, k_ref, kb_ref), (wv_ref, v_ref, vb_ref)):
        y = _dot(mn, w_ref[...].astype(BF16))
        ob_ref[...] = y.astype(BF16)
        for b in range(nb):
            for hd in range(XH):
                for j in range(2):
                    c0 = hd * XHD + j * 128
                    o_ref[b, pl.ds(j * XH + hd, N_MEM, stride=8), :] = y[b * N_MEM:(b + 1) * N_MEM, c0:c0 + 128]


def _memkv(mem, g, wk, wv):
    rows = mem.shape[0]
    nb = 2
    tm = nb * N_MEM
    o_spec = pl.BlockSpec((None, nb, N_MEM * 8, 128), lambda l, i: (l, i, 0, 0))
    ob_spec = pl.BlockSpec((None, tm, D), lambda l, i: (l, i, 0))
    w_spec = pl.BlockSpec((None, D, D), lambda l, i: (l, 0, 0))
    return pl.pallas_call(
        functools.partial(_memkv_body, nb=nb),
        grid=(2, rows // tm),
        in_specs=[pl.BlockSpec((tm, D), lambda l, i: (i, 0)),
                  pl.BlockSpec((None, 1, D), lambda l, i: (l, 0, 0)),
                  w_spec, w_spec],
        out_specs=[o_spec, o_spec, ob_spec, ob_spec],
        out_shape=[jax.ShapeDtypeStruct((2, NB, N_MEM * 8, 128), F32)] * 2
        + [jax.ShapeDtypeStruct((2, rows, D), BF16)] * 2,
        compiler_params=_params(("arbitrary", "arbitrary"), 40),
        name="memkv",
    )(mem, g, wk, wv)


def _from_head_view(c):
    s = c.shape
    return c.reshape(s[0], s[1], N_MEM, 2, XH, 128).transpose(0, 1, 2, 4, 3, 5).reshape(s[0], s[1], N_MEM, XH, XHD)


def _even_front(x, gpre_ref, win, vn_ref):
    h = _rms(x, gpre_ref[...]).astype(BF16)
    u = _gelu(_dot(h, win(0, 512)))
    v = _rms(_gelu(_dot(h, win(512, 1024))), vn_ref[...])
    bg = _dot(h, win(1024, 1536))
    p = _dot(h, win(1536, 2048)) * _dot(h, win(2048, 2560))
    return u, v, bg, p


def _even_p_body(x_ref, gpre_ref, winf_ref, vn_ref, sw_ref, sb_ref, cw_ref, woutf_ref, gpost_ref,
                 xo_ref, st_ref, win_ref, wout_ref, carry_ref, cat_ref, sh_ref, *, tm):
    @pl.when((pl.program_id(0) == 0) & (pl.program_id(1) == 0))
    def _():
        win_ref[...] = winf_ref[...].astype(BF16)
        wout_ref[...] = woutf_ref[...].astype(BF16)

    @pl.when(pl.program_id(1) == 0)
    def _():
        carry_ref[...] = jnp.zeros_like(carry_ref)

    x = x_ref[...]
    u, v, bg, p = _even_front(x, gpre_ref, lambda c0, c1: win_ref[:, c0:c1], vn_ref)
    vb = v.astype(BF16)
    ri = lax.broadcasted_iota(jnp.int32, (CHUNK, CHUNK), 0)
    ci = lax.broadcasted_iota(jnp.int32, (CHUNK, CHUNK), 1)
    for g in range(4):
        cols = slice(g * 128, (g + 1) * 128)
        w = jnp.where(ri >= ci, sw_ref[g], 0.0).astype(BF16)
        for c in range(tm // CHUNK):
            rows = slice(c * CHUNK, (c + 1) * CHUNK)
            mixed = _dot(w, vb[rows, cols]) + sb_ref[:, cols]
            cat_ref[rows, cols] = (u[rows, cols] * mixed).astype(BF16)
    cz = _conv_rows(p, carry_ref, cw_ref, slice(0, SC_W), sh_ref)
    cat_ref[:, 512:1024] = (bg * cz).astype(BF16)
    carry_ref[...] = p[tm - 8:tm, :]
    st_ref[...] = p[tm - 8:tm, :]
    out = _dot(cat_ref[...], wout_ref[...])
    xo_ref[...] = x + _rms(out, gpost_ref[...])


def _even_p(x, gpre, win, vn, sw, sb, cw, wout, gpost):
    tm = 512
    nt = SEQ // tm
    row_spec = pl.BlockSpec((tm, D), lambda b, t: (b * nt + t, 0))
    return pl.pallas_call(
        functools.partial(_even_p_body, tm=tm),
        grid=(NB, nt),
        in_specs=[row_spec, _const((1, D)), _layer_const(0, (D, 2560)), _const((1, SGU_W)),
                  _layer_const(0, (4, CHUNK, CHUNK)), _const((CHUNK, SGU_W)), _layer_const(0, (3, SC_W)),
                  _layer_const(0, (D, D)), _const((1, D))],
        out_specs=[row_spec, pl.BlockSpec((None, 8, SC_W), lambda b, t: (b, 0, 0))],
        out_shape=[jax.ShapeDtypeStruct((NB * SEQ, D), F32), jax.ShapeDtypeStruct((NB, 8, SC_W), F32)],
        scratch_shapes=[pltpu.VMEM((D, 2560), BF16), pltpu.VMEM((D, D), BF16), pltpu.VMEM((8, SC_W), F32),
                        pltpu.VMEM((tm, D), BF16), pltpu.VMEM((SC_W // 128, tm + 8, 128), F32)],
        compiler_params=_params(("arbitrary", "arbitrary"), 48),
        name="even_p",
    )(x, gpre, win, vn, sw, sb, cw, wout, gpost)


def _even_s_body(x_ref, gpre_ref, win_ref, vn_ref, gw_ref, gb_ref, cw_ref, st_ref, wout_ref, gpost_ref,
                 xo_ref, ns_ref, v_ref, pan_ref):
    xt = _load_time_major(x_ref, pan_ref)
    u, v, bg, p = _even_front(xt, gpre_ref, lambda c0, c1: win_ref[:, c0:c1].astype(BF16), vn_ref)
    _store_batch_major(v_ref, v, pan_ref)
    vt = [v[t * NS:(t + 1) * NS, :] for t in range(TS)]
    mixed = []
    for t in range(TS):
        m = gb_ref[t:t + 1, :] + gw_ref[t, 0:1, :] * vt[0]
        for s in range(1, t + 1):
            m = m + gw_ref[t, s:s + 1, :] * vt[s]
        mixed.append(m)
    a = u * jnp.concatenate(mixed, axis=0)
    prev = [jnp.concatenate([st_ref[j * 2 + r] for j in range(SC_W // 128)], axis=1) for r in range(2)]
    cz = _conv_time_major(p, prev, cw_ref)
    for j in range(SC_W // 128):
        for r in range(2):
            ns_ref[j * 2 + r] = p[(TS - 2 + r) * NS:(TS - 1 + r) * NS, j * 128:(j + 1) * 128]
    cat = jnp.concatenate([a, bg * cz], axis=1).astype(BF16)
    out = _dot(cat, wout_ref[...].astype(BF16))
    _store_batch_major(xo_ref, xt + _rms(out, gpost_ref[...]), pan_ref)


def _even_s(x, gpre, win, vn, gw, gb, cw, st, wout, gpost):
    nst = SC_W // 128 * 2
    return pl.pallas_call(
        _even_s_body,
        grid=(1,),
        in_specs=[_const((RS, D)), _const((1, D)), _layer_const(0, (D, 2560)), _const((1, SGU_W)),
                  _const((TS, TS, SGU_W)), _const((TS, SGU_W)), _layer_const(0, (3, SC_W)), _const((nst, NS, 128)),
                  _layer_const(0, (D, D)), _const((1, D))],
        out_specs=[_full((RS, D)), _full((nst, NS, 128)), _full((RS, SGU_W))],
        out_shape=[jax.ShapeDtypeStruct((RS, D), F32), jax.ShapeDtypeStruct((nst, NS, 128), F32),
                   jax.ShapeDtypeStruct((RS, SGU_W), F32)],
        scratch_shapes=[pltpu.VMEM((D // 128, RS, 128), F32)],
        compiler_params=_params(("arbitrary",), 48),
        name="even_s",
    )(x, gpre, win, vn, gw, gb, cw, st, wout, gpost)


def _softmax_rows(s):
    m = jnp.max(s, axis=-1, keepdims=True)
    e = jnp.exp(s - m)
    return e * (1.0 / jnp.sum(e, axis=-1, keepdims=True))


def _xattn_p_body(x_ref, gpre_ref, wqf_ref, k_ref, v_ref, wof_ref, gpost_ref, xo_ref, wq_ref, wo_ref, q_ref, s_ref,
                  p_ref, o_ref):
    @pl.when((pl.program_id(0) == 0) & (pl.program_id(1) == 0))
    def _():
        wq_ref[...] = wqf_ref[...].astype(BF16)
        wo_ref[...] = wof_ref[...].astype(BF16)

    h = _rms(x_ref[...], gpre_ref[...]).astype(BF16)
    q_ref[...] = _dot(h, wq_ref[...]).astype(BF16)
    for hd in range(XH):
        cols = slice(hd * XHD, (hd + 1) * XHD)
        s_ref[hd] = _dot_nt(q_ref[:, cols], k_ref[:, cols]) * (XHD ** -0.5)
    p_ref[...] = _softmax_rows(s_ref[...]).astype(BF16)
    for hd in range(XH):
        cols = slice(hd * XHD, (hd + 1) * XHD)
        o_ref[:, cols] = _dot(p_ref[hd], v_ref[:, cols]).astype(BF16)
    out = _dot(o_ref[...], wo_ref[...])
    xo_ref[...] = x_ref[...] + _rms(out, gpost_ref[...])


def _xattn_p(l, x, gpre, wq, kb, vb, wo, gpost):
    tm = 512
    nt = SEQ // tm
    row_spec = pl.BlockSpec((tm, D), lambda b, t: (b * nt + t, 0))
    kv_spec = pl.BlockSpec((None, N_MEM, D), lambda b, t: (l, b, 0))
    return pl.pallas_call(
        _xattn_p_body,
        grid=(NB, nt),
        in_specs=[row_spec, _const((1, D)), _layer_const(l, (D, D)), kv_spec, kv_spec, _layer_const(l, (D, D)),
                  _const((1, D))],
        out_specs=row_spec,
        out_shape=jax.ShapeDtypeStruct((NB * SEQ, D), F32),
        scratch_shapes=[pltpu.VMEM((D, D), BF16), pltpu.VMEM((D, D), BF16), pltpu.VMEM((tm, D), BF16),
                        pltpu.VMEM((XH, tm, N_MEM), F32), pltpu.VMEM((XH, tm, N_MEM), BF16),
                        pltpu.VMEM((tm, D), BF16)],
        compiler_params=_params(("arbitrary", "arbitrary"), 48),
        name="xattn_p",
    )(x, gpre, wq, kb, vb, wo, gpost)


def _head_view(c):
    s = c.shape
    return c.reshape(s[0], s[1], N_MEM, XH, 2, 128).transpose(0, 1, 2, 4, 3, 5).reshape(s[0], s[1], N_MEM * 8, 128)


def _head_rows(ref, b, hd):
    halves = [ref[b, pl.ds(j * XH + hd, N_MEM, stride=8), :] for j in range(2)]
    return jnp.concatenate(halves, axis=1).astype(BF16)


def _xattn_s_body(x_ref, gpre_ref, wq_ref, k_ref, v_ref, wo_ref, gpost_ref, xo_ref, q_ref, o_ref, s_ref, *, bb):
    step = pl.program_id(0)

    @pl.when(step == 0)
    def _():
        h = _rms(x_ref[...], gpre_ref[...]).astype(BF16)
        q_ref[...] = _dot(h, wq_ref[...].astype(BF16))

    first = (lax.broadcasted_iota(jnp.int32, (8, XHD), 0) >> 2) == 0
    groups = [(pi, hd, bi) for pi in range(bb // 2) for hd in range(XH) for bi in range(2)]
    row0 = [pl.multiple_of((step * (bb // 2) + pi) * 8, 8) for pi in range(bb // 2)]
    for gi, (pi, hd, bi) in enumerate(groups):
        q8 = q_ref[pl.ds(row0[pi], 8), hd * XHD:(hd + 1) * XHD].astype(BF16)
        kh = _head_rows(k_ref, pi * 2 + bi, hd)
        s_ref[gi * 8:(gi + 1) * 8, :] = _dot_nt(q8, kh) * (XHD ** -0.5)
    s_ref[...] = _softmax_rows(s_ref[...])
    for gi, (pi, hd, bi) in enumerate(groups):
        if bi == 1:
            continue
        pv = [_dot(s_ref[(gi + b) * 8:(gi + b + 1) * 8, :].astype(BF16), _head_rows(v_ref, pi * 2 + b, hd))
              for b in range(2)]
        o_ref[pl.ds(row0[pi], 8), hd * XHD:(hd + 1) * XHD] = jnp.where(first, pv[0], pv[1])

    @pl.when(step == pl.num_programs(0) - 1)
    def _():
        out = _dot(o_ref[...].astype(BF16), wo_ref[...].astype(BF16))
        xo_ref[...] = x_ref[...] + _rms(out, gpost_ref[...])


def _xattn_s(l, x, gpre, wq, k, v, wo, gpost):
    bb = 4
    kv_spec = pl.BlockSpec((None, bb, N_MEM * 8, 128), lambda i: (l, i, 0, 0))
    return pl.pallas_call(
        functools.partial(_xattn_s_body, bb=bb),
        grid=(NS // bb,),
        in_specs=[_const((RS, D)), _const((1, D)), _layer_const(l, (D, D)), kv_spec, kv_spec, _layer_const(l, (D, D)),
                  _const((1, D))],
        out_specs=_full((RS, D)),
        out_shape=jax.ShapeDtypeStruct((RS, D), F32),
        scratch_shapes=[pltpu.VMEM((RS, D), F32), pltpu.VMEM((RS, D), F32), pltpu.VMEM((bb * XH * 8, N_MEM), F32)],
        compiler_params=_params(("arbitrary",), 48),
        name="xattn_s",
    )(x, gpre, wq, k, v, wo, gpost)


_DOWN_SPLITS = (0, NFC)


def _ffn_p_body(x_ref, gpre_ref, wg_ref, wv_ref, cw_ref, wdn_ref, gpost_ref, xo_ref, st_ref, carry_ref, h_ref, act_ref,
                *sh_refs, tm):
    @pl.when(pl.program_id(1) == 0)
    def _():
        carry_ref[...] = jnp.zeros_like(carry_ref)

    h_ref[...] = _rms(x_ref[...], gpre_ref[...]).astype(BF16)
    for c in range(NFC):
        halves = []
        for i, (base, w_ref) in enumerate(((0, wg_ref), (DFF, wv_ref))):
            cols = slice(base + c * FC, base + (c + 1) * FC)
            up = _dot(h_ref[...], w_ref[:, c * FC:(c + 1) * FC])
            halves.append(_conv_rows(up, carry_ref, cw_ref, cols, sh_refs[(2 * c + i) % len(sh_refs)]))
            carry_ref[:, cols] = up[tm - 8:tm, :]
            st_ref[:, cols] = up[tm - 8:tm, :]
        act_ref[:, c * FC:(c + 1) * FC] = (_gelu(halves[0]) * halves[1]).astype(BF16)
    out = _dot(act_ref[...], wdn_ref[...])
    xo_ref[...] = x_ref[...] + _rms(out, gpost_ref[...])


def _ffn_p(l, x, gpre, wg, wv, cw, wdn, gpost):
    tm = 512
    nt = SEQ // tm
    row_spec = pl.BlockSpec((tm, D), lambda b, t: (b * nt + t, 0))
    return pl.pallas_call(
        functools.partial(_ffn_p_body, tm=tm),
        grid=(NB, nt),
        in_specs=[row_spec, _const((1, D)), _const((D, DFF)), _const((D, DFF)), _layer_const(l, (3, 2 * DFF)),
                  _const((DFF, D)), _const((1, D))],
        out_specs=[row_spec, pl.BlockSpec((None, 8, 2 * DFF), lambda b, t: (b, 0, 0))],
        out_shape=[jax.ShapeDtypeStruct((NB * SEQ, D), F32), jax.ShapeDtypeStruct((NB, 8, 2 * DFF), F32)],
        scratch_shapes=[pltpu.VMEM((8, 2 * DFF), F32), pltpu.VMEM((tm, D), BF16), pltpu.VMEM((tm, DFF), BF16)]
        + [pltpu.VMEM((FC // 128, tm + 8, 128), F32)] * 4,
        compiler_params=_params(("arbitrary", "arbitrary"), 52),
        name="ffn_p",
    )(x, gpre, wg, wv, cw, wdn, gpost)


def _load_time_major(x_ref, pan_ref):
    n = x_ref.shape[1] // 128
    for p in range(n):
        pan_ref[p] = x_ref[:, p * 128:(p + 1) * 128]
    return jnp.concatenate(
        [jnp.concatenate([pan_ref[p, pl.ds(t, NS, stride=TS), :] for p in range(n)], axis=1) for t in range(TS)],
        axis=0)


def _store_batch_major(o_ref, y, pan_ref):
    n = o_ref.shape[1] // 128
    for p in range(n):
        for t in range(TS):
            pan_ref[p, pl.ds(t, NS, stride=TS), :] = y[t * NS:(t + 1) * NS, p * 128:(p + 1) * 128]
    for p in range(n):
        o_ref[:, p * 128:(p + 1) * 128] = pan_ref[p]


def _conv_time_major(u, prev, w_ref):
    blocks = list(prev) + [u[t * NS:(t + 1) * NS, :] for t in range(TS)]
    w0, w1, w2 = w_ref[0:1, :], w_ref[1:2, :], w_ref[2:3, :]
    return jnp.concatenate([(w0 * blocks[t] + w1 * blocks[t + 1]) + w2 * blocks[t + 2] for t in range(TS)], axis=0)


def _ffn_s_body(x_ref, gpre_ref, wg_ref, wv_ref, cg_ref, cv_ref, sg_ref, sv_ref, wdn_ref, gpost_ref,
                xo_ref, ng_ref, nv_ref, wgb_ref, wvb_ref, wdb_ref, xt_ref, h_ref, acc_ref, pan_ref):
    c = pl.program_id(0)

    @pl.when(c == 0)
    def _():
        xt = _load_time_major(x_ref, pan_ref)
        xt_ref[...] = xt
        h_ref[...] = _rms(xt, gpre_ref[...]).astype(BF16)
        acc_ref[...] = jnp.zeros_like(acc_ref)

    wgb_ref[...] = wg_ref[...].astype(BF16)
    wvb_ref[...] = wv_ref[...].astype(BF16)
    wdb_ref[...] = wdn_ref[...].astype(BF16)
    ys = []
    for wb_ref, cw_ref, s_ref, n_ref in ((wgb_ref, cg_ref, sg_ref, ng_ref), (wvb_ref, cv_ref, sv_ref, nv_ref)):
        u = _dot(h_ref[...], wb_ref[...])
        prev = [jnp.concatenate([s_ref[j * 2 + r] for j in range(FC // 128)], axis=1) for r in range(2)]
        ys.append(_conv_time_major(u, prev, cw_ref))
        for j in range(FC // 128):
            for r in range(2):
                n_ref[j * 2 + r] = u[(TS - 2 + r) * NS:(TS - 1 + r) * NS, j * 128:(j + 1) * 128]
    act = (_gelu(ys[0]) * ys[1]).astype(BF16)
    acc_ref[...] += _dot(act, wdb_ref[...])

    @pl.when(c == pl.num_programs(0) - 1)
    def _():
        _store_batch_major(xo_ref, xt_ref[...] + _rms(acc_ref[...], gpost_ref[...]), pan_ref)


def _ffn_s(l, x, gpre, wup, cw, st, wdn, gpost):
    rpc = FC // 128 * 2
    gcol = lambda r: pl.BlockSpec((None, r, FC), lambda c: (l, 0, c))
    vcol = lambda r: pl.BlockSpec((None, r, FC), lambda c: (l, 0, NFC + c))
    sg_in = pl.BlockSpec((None, rpc, NS, 128), lambda c: (l, c, 0, 0))
    sv_in = pl.BlockSpec((None, rpc, NS, 128), lambda c: (l, NFC + c, 0, 0))
    s_out = pl.BlockSpec((rpc, NS, 128), lambda c: (c, 0, 0))
    wb_out = pl.BlockSpec((D, FC), lambda c: (0, c))
    return pl.pallas_call(
        _ffn_s_body,
        grid=(NFC,),
        in_specs=[_const((RS, D)), _const((1, D)), gcol(D), vcol(D), gcol(3), vcol(3), sg_in, sv_in,
                  pl.BlockSpec((None, FC, D), lambda c: (l, c, 0)), _const((1, D))],
        out_specs=[_full((RS, D)), s_out, s_out, wb_out, wb_out, pl.BlockSpec((FC, D), lambda c: (c, 0))],
        out_shape=[jax.ShapeDtypeStruct((RS, D), F32),
                   jax.ShapeDtypeStruct((NFC * rpc, NS, 128), F32), jax.ShapeDtypeStruct((NFC * rpc, NS, 128), F32),
                   jax.ShapeDtypeStruct((D, DFF), BF16), jax.ShapeDtypeStruct((D, DFF), BF16),
                   jax.ShapeDtypeStruct((DFF, D), BF16)],
        scratch_shapes=[pltpu.VMEM((RS, D), F32), pltpu.VMEM((RS, D), BF16), pltpu.VMEM((RS, D), F32),
                        pltpu.VMEM((D // 128, RS, 128), F32)],
        compiler_params=_params(("arbitrary",), 40),
        name="ffn_s",
    )(x, gpre, wup, wup, cw, cw, st, st, wdn, gpost)


def _rope_head(z, cos, sin, base):
    x1 = z[:, base:base + 128]
    x2 = z[:, base + 128:base + 256]
    return x1 * cos - x2 * sin, x1 * sin + x2 * cos


def _group_norm_gate(o, g):
    mu = jnp.mean(o, axis=-1, keepdims=True)
    d = o - mu
    var = jnp.mean(d * d, axis=-1, keepdims=True)
    return (g * jax.nn.sigmoid(g)) * (d * lax.rsqrt(var + EPS))


RCH = 256


def _ret_p_body(x_ref, cos_ref, sin_ref, gpre_ref, wq_ref, wk_ref, wv_ref, wg_ref, wout_ref, gpost_ref, xo_ref, s_ref,
                h_ref, q_ref, k_ref, kd_ref, v_ref, g_ref, y_ref, dm_ref, qd_ref, kdec_ref, *, tm):
    first = (pl.program_id(0) == 0) & (pl.program_id(1) == 0)

    @pl.when(first)
    def _():
        i = lax.broadcasted_iota(jnp.int32, (RCH, RCH), 0).astype(F32)
        j = lax.broadcasted_iota(jnp.int32, (RCH, RCH), 1).astype(F32)
        i1 = lax.broadcasted_iota(jnp.int32, (RCH, 128), 0).astype(F32)
        for hd in range(RET_H):
            lg = _LOG_G[hd]
            dm_ref[hd] = jnp.where(i >= j, jnp.exp(lg * jnp.maximum(i - j, 0.0)), 0.0)
            qd_ref[hd] = jnp.exp(lg * (i1 + 1.0))
            kdec_ref[hd] = jnp.exp(lg * (RCH - 1.0 - i1))

    @pl.when(pl.program_id(1) == 0)
    def _():
        s_ref[...] = jnp.zeros_like(s_ref)

    h_ref[...] = _rms(x_ref[...], gpre_ref[...]).astype(BF16)
    cos = cos_ref[...]
    sin = sin_ref[...]
    nchunk = tm // RCH
    for hd in range(RET_H):
        base = hd * RET_DK
        zq = _dot(h_ref[...], wq_ref[:, base:base + RET_DK])
        zk = _dot(h_ref[...], wk_ref[:, base:base + RET_DK])
        q1, q2 = _rope_head(zq, cos, sin, 0)
        k1, k2 = _rope_head(zk, cos, sin, 0)
        q_ref[:, base:base + 128] = (q1 * (RET_DK ** -0.5)).astype(BF16)
        q_ref[:, base + 128:base + 256] = (q2 * (RET_DK ** -0.5)).astype(BF16)
        k_ref[:, base:base + 128] = k1.astype(BF16)
        k_ref[:, base + 128:base + 256] = k2.astype(BF16)
        kdec = jnp.concatenate([kdec_ref[hd]] * nchunk, axis=0)
        kd_ref[:, base:base + 128] = k1 * kdec
        kd_ref[:, base + 128:base + 256] = k2 * kdec
        vcols = slice(hd * RET_DV, (hd + 1) * RET_DV)
        v_ref[:, vcols] = _dot(h_ref[...], wv_ref[:, vcols]).astype(BF16)
        g_ref[:, vcols] = _dot(h_ref[...], wg_ref[:, vcols])
    for hd in range(RET_H):
        kcols = slice(hd * RET_DK, (hd + 1) * RET_DK)
        vcols = slice(hd * RET_DV, (hd + 1) * RET_DV)
        decay_l = float(np.exp(np.float32(_LOG_G[hd]) * np.float32(RCH)))
        qd = qd_ref[hd]
        qd4 = jnp.concatenate([qd, qd, qd, qd], axis=1)
        for c in range(nchunk):
            rows = slice(c * RCH, (c + 1) * RCH)
            qc = q_ref[rows, kcols]
            vc = v_ref[rows, vcols]
            s_prev = s_ref[hd]
            sc = (_dot_nt(qc, k_ref[rows, kcols]) * dm_ref[hd]).astype(BF16)
            o = _dot(sc, vc) + _dot(qc, s_prev.astype(BF16)) * qd4
            kdt = kd_ref[rows, kcols].T.astype(BF16)
            s_ref[hd] = decay_l * s_prev + _dot(kdt, vc)
            y_ref[rows, vcols] = _group_norm_gate(o, g_ref[rows, vcols]).astype(BF16)
    out = _dot(y_ref[...], wout_ref[...])
    xo_ref[...] = x_ref[...] + _rms(out, gpost_ref[...])


def _ret_p(x, cos, sin, gpre, wq, wk, wv, wg, wout, gpost):
    tm = 512
    nt = SEQ // tm
    row_spec = pl.BlockSpec((tm, D), lambda b, t: (b * nt + t, 0))
    cs_spec = pl.BlockSpec((tm, 128), lambda b, t: (t, 0))
    return pl.pallas_call(
        functools.partial(_ret_p_body, tm=tm),
        grid=(NB, nt),
        in_specs=[row_spec, cs_spec, cs_spec, _const((1, D)), _const((D, NQ)), _const((D, NQ)), _const((D, NV)),
                  _const((D, NV)), _const((NV, D)), _const((1, D))],
        out_specs=[row_spec, pl.BlockSpec((None, RET_H, RET_DK, RET_DV), lambda b, t: (b, 0, 0, 0))],
        out_shape=[jax.ShapeDtypeStruct((NB * SEQ, D), F32), jax.ShapeDtypeStruct((NB, RET_H, RET_DK, RET_DV), F32)],
        scratch_shapes=[pltpu.VMEM((tm, D), BF16), pltpu.VMEM((tm, NQ), BF16), pltpu.VMEM((tm, NQ), BF16),
                        pltpu.VMEM((tm, NQ), F32), pltpu.VMEM((tm, NV), BF16), pltpu.VMEM((tm, NV), F32),
                        pltpu.VMEM((tm, NV), BF16), pltpu.VMEM((RET_H, RCH, RCH), F32),
                        pltpu.VMEM((RET_H, RCH, 128), F32), pltpu.VMEM((RET_H, RCH, 128), F32)],
        compiler_params=_params(("arbitrary", "arbitrary"), 56),
        name="ret_p",
    )(x, cos, sin, gpre, wq, wk, wv, wg, wout, gpost)


def _ret_s_proj_body(x_ref, cos_ref, sin_ref, gpre_ref, wqf_ref, wkf_ref, wvf_ref, wgf_ref,
                     q_ref, kdt_ref, v_ref, g_ref, oi_ref, wq_ref, wk_ref, wv_ref, wg_ref, h_ref):
    hd = pl.program_id(0)

    @pl.when(hd == 0)
    def _():
        h_ref[...] = _rms(x_ref[...], gpre_ref[...]).astype(BF16)

    for wf_ref, wb_ref in ((wqf_ref, wq_ref), (wkf_ref, wk_ref), (wvf_ref, wv_ref), (wgf_ref, wg_ref)):
        wb_ref[...] = wf_ref[...].astype(BF16)
    lg = jnp.where(hd == 0, _LOG_G[0], jnp.where(hd == 1, _LOG_G[1], jnp.where(hd == 2, _LOG_G[2], _LOG_G[3])))
    h = h_ref[...]
    cos = cos_ref[...]
    sin = sin_ref[...]
    ti = (lax.broadcasted_iota(jnp.int32, (RS, 128), 0) & (TS - 1)).astype(F32)
    i = lax.broadcasted_iota(jnp.int32, (CHUNK, CHUNK), 0)
    j = lax.broadcasted_iota(jnp.int32, (CHUNK, CHUNK), 1)
    same = ((i >> 2) == (j >> 2)) & (i >= j)
    dm = jnp.where(same, jnp.exp(lg * jnp.maximum(i - j, 0).astype(F32)), 0.0)
    q1, q2 = _rope_head(_dot(h, wq_ref[...]), cos, sin, 0)
    k1, k2 = _rope_head(_dot(h, wk_ref[...]), cos, sin, 0)
    q = jnp.concatenate([q1, q2], axis=1) * (RET_DK ** -0.5)
    k = jnp.concatenate([k1, k2], axis=1)
    q_ref[...] = q
    kdec = jnp.exp(lg * (TS - 1.0 - ti))
    kd = k * jnp.concatenate([kdec, kdec], axis=1)
    vf = _dot(h, wv_ref[...])
    v_ref[...] = vf
    v = vf.astype(BF16)
    g_ref[...] = _dot(h, wg_ref[...])
    qb = q.astype(BF16)
    kb = k.astype(BF16)
    for tl in range(RS // CHUNK):
        rows = slice(tl * CHUNK, (tl + 1) * CHUNK)
        sc = (_dot_nt(qb[rows], kb[rows]) * dm).astype(BF16)
        oi_ref[rows, :] = _dot(sc, v[rows])
        kdt_ref[tl] = kd[rows].T.astype(BF16)


def _ret_s_proj(x, cos, sin, gpre, win):
    def wcol(width, first_block):
        return pl.BlockSpec((None, D, width), lambda h: (0, 0, first_block + h))
    def hcol(rows, width):
        return pl.BlockSpec((rows, width), lambda h: (0, h))
    return pl.pallas_call(
        _ret_s_proj_body,
        grid=(RET_H,),
        in_specs=[_const((RS, D)), _const((RS, 128)), _const((RS, 128)), _const((1, D)),
                  wcol(RET_DK, 0), wcol(RET_DK, NQ // RET_DK), wcol(RET_DV, 2 * NQ // RET_DV),
                  wcol(RET_DV, (2 * NQ + NV) // RET_DV)],
        out_specs=[hcol(RS, RET_DK),
                   pl.BlockSpec((RS // CHUNK, None, RET_DK, CHUNK), lambda h: (0, h, 0, 0)),
                   hcol(RS, RET_DV), hcol(RS, RET_DV), hcol(RS, RET_DV),
                   hcol(D, RET_DK), hcol(D, RET_DK), hcol(D, RET_DV), hcol(D, RET_DV)],
        out_shape=[jax.ShapeDtypeStruct((RS, NQ), F32),
                   jax.ShapeDtypeStruct((RS // CHUNK, RET_H, RET_DK, CHUNK), BF16),
                   jax.ShapeDtypeStruct((RS, NV), F32), jax.ShapeDtypeStruct((RS, NV), F32),
                   jax.ShapeDtypeStruct((RS, NV), F32),
                   jax.ShapeDtypeStruct((D, NQ), BF16), jax.ShapeDtypeStruct((D, NQ), BF16),
                   jax.ShapeDtypeStruct((D, NV), BF16), jax.ShapeDtypeStruct((D, NV), BF16)],
        scratch_shapes=[pltpu.VMEM((RS, D), BF16)],
        compiler_params=_params(("arbitrary",), 40),
        name="ret_s_proj",
    )(x, cos, sin, gpre, win, win, win, win)


def _ret_s_core_body(q_ref, kdt_ref, v_ref, oi_ref, s_ref, o_ref, so_ref):
    step = pl.program_id(0)
    pair_in_tile = step % (CHUNK // 8)
    row8 = lax.broadcasted_iota(jnp.int32, (8, RET_DK), 0)
    row128 = lax.broadcasted_iota(jnp.int32, (CHUNK, RET_DV), 0)
    t8 = (lax.broadcasted_iota(jnp.int32, (8, RET_DV), 0) & (TS - 1)).astype(F32)
    for hd in range(RET_H):
        lg = _LOG_G[hd]
        decay_l = float(np.exp(np.float32(lg) * np.float32(TS)))
        q8 = q_ref[:, hd * RET_DK:(hd + 1) * RET_DK]
        v128 = v_ref[:, hd * RET_DV:(hd + 1) * RET_DV]
        kdt = kdt_ref[hd]
        inter = jnp.zeros((8, RET_DV), F32)
        for bi in range(2):
            s_prev = s_ref[bi, hd]
            qm = jnp.where((row8 >> 2) == bi, q8, 0.0).astype(BF16)
            inter = inter + _dot(qm, s_prev.astype(BF16))
            vm = jnp.where((row128 >> 2) == pair_in_tile * 2 + bi, v128, 0.0).astype(BF16)
            so_ref[bi, hd] = decay_l * s_prev + _dot(kdt, vm)
        cols = slice(hd * RET_DV, (hd + 1) * RET_DV)
        o_ref[:, cols] = oi_ref[:, cols] + inter * jnp.exp(lg * (t8 + 1.0))


def _ret_s_core(q, kdt, v, oi, s):
    ppt = CHUNK // 8
    s_spec = pl.BlockSpec((2, RET_H, RET_DK, RET_DV), lambda i: (i, 0, 0, 0))
    return pl.pallas_call(
        _ret_s_core_body,
        grid=(NS // 2,),
        in_specs=[pl.BlockSpec((8, NQ), lambda i: (i, 0)),
                  pl.BlockSpec((None, RET_H, RET_DK, CHUNK), lambda i: (i // ppt, 0, 0, 0)),
                  pl.BlockSpec((CHUNK, NV), lambda i: (i // ppt, 0)),
                  pl.BlockSpec((8, NV), lambda i: (i, 0)),
                  s_spec],
        out_specs=[pl.BlockSpec((8, NV), lambda i: (i, 0)), s_spec],
        out_shape=[jax.ShapeDtypeStruct((RS, NV), F32), jax.ShapeDtypeStruct((NS, RET_H, RET_DK, RET_DV), F32)],
        compiler_params=_params(("arbitrary",), 40),
        name="ret_s_core",
    )(q, kdt, v, oi, s)


def _ret_s_out_body(x_ref, o_ref, g_ref, woutf_ref, gpost_ref, xo_ref, wout_ref, y_ref):
    wout_ref[...] = woutf_ref[...].astype(BF16)
    for hd in range(RET_H):
        cols = slice(hd * RET_DV, (hd + 1) * RET_DV)
        y_ref[:, cols] = _group_norm_gate(o_ref[:, cols], g_ref[:, cols]).astype(BF16)
    out = _dot(y_ref[...], wout_ref[...])
    xo_ref[...] = x_ref[...] + _rms(out, gpost_ref[...])


def _ret_s_out(x, o, g, wout, gpost):
    return pl.pallas_call(
        _ret_s_out_body,
        grid=(1,),
        in_specs=[_const((RS, D)), _const((RS, NV)), _const((RS, NV)), _layer_const(0, (NV, D)), _const((1, D))],
        out_specs=[_full((RS, D)), _full((NV, D))],
        out_shape=[jax.ShapeDtypeStruct((RS, D), F32), jax.ShapeDtypeStruct((NV, D), BF16)],
        scratch_shapes=[pltpu.VMEM((RS, NV), BF16)],
        compiler_params=_params(("arbitrary",), 48),
        name="ret_s_out",
    )(x, o, g, wout, gpost)


def _rope_tables(pos):
    half = RET_DK // 2
    inv = ROPE_BASE ** (-jnp.arange(half, dtype=F32) / half)
    ang = pos.astype(F32)[:, None] * inv[None, :]
    return jnp.cos(ang), jnp.sin(ang)


def _state_rows(st):
    lead, c = st.shape[:-3], st.shape[-1]
    n = len(lead)
    s = st.reshape(lead + (NS, 2, c // 128, 128))
    s = s.transpose(tuple(range(n)) + (n + 2, n + 1, n, n + 3))
    return s.reshape(lead + (c // 128 * 2, NS, 128))


def _from_state_rows(s):
    tiles = s.shape[0] // 2
    return s.reshape(tiles, 2, NS, 128).transpose(2, 1, 0, 3).reshape(NS, 2, tiles * 128)


def kernel(x_prompt, x_sample, mem_prompt, cache_mem_k, cache_mem_v, state_shortconv, state_retention, state_ffn_conv, norm_mix_pre, norm_mix_post, w_in_even, sgu_vnorm, sgu_w, sgu_b, conv_short, w_out_even, w_in_odd, w_out_odd, norm_x_pre, norm_x_post, norm_mem, w_xq, w_xk, w_xv, w_xo, norm_ffn_pre, norm_ffn_post, w_ffn_up, conv_ffn, w_ffn_down):
    row = lambda g: g.reshape(1, -1)

    xs = x_sample.reshape(RS, D)
    w4 = jnp.tril(sgu_w[0][:, :TS, :TS])
    gw = jnp.repeat(w4.transpose(1, 2, 0), CHUNK, axis=2)
    gb = jnp.repeat(sgu_b[0][:, :TS].T, CHUNK, axis=1)
    xs, sc_s, v_s = _even_s(xs, row(norm_mix_pre[0]), w_in_even, row(sgu_vnorm[0]), gw, gb, conv_short,
                            _state_rows(state_shortconv[0]), w_out_even, row(norm_mix_post[0]))
    cos_s, sin_s = _rope_tables(PAST + (jnp.arange(RS, dtype=jnp.int32) % TS))
    cache_k, cache_v = _head_view(cache_mem_k), _head_view(cache_mem_v)
    ffn_rows = _state_rows(state_ffn_conv)
    ffn_s_states, ffn_w = [], []
    for l in range(2):
        if l == 1:
            q, kdt, v, g, oi, wq_o, wk_o, wv_o, wg_o = _ret_s_proj(xs, cos_s, sin_s, row(norm_mix_pre[1]), w_in_odd)
            o, ret_s_state = _ret_s_core(q, kdt, v, oi, state_retention[0])
            xs, wout_o = _ret_s_out(xs, o, g, w_out_odd, row(norm_mix_post[1]))
        xs = _xattn_s(l, xs, row(norm_x_pre[l]), w_xq, cache_k, cache_v, w_xo, row(norm_x_post[l]))
        xs, ng, nv, wg_b, wv_b, wd_b = _ffn_s(l, xs, row(norm_ffn_pre[l]), w_ffn_up, conv_ffn, ffn_rows, w_ffn_down,
                                              row(norm_ffn_post[l]))
        ffn_s_states.append(_from_state_rows(jnp.concatenate([ng, nv], axis=0)))
        ffn_w.append((wg_b, wv_b, wd_b))

    mem_k, mem_v, mem_kb, mem_vb = _memkv(mem_prompt.reshape(NB * N_MEM, D), norm_mem.reshape(2, 1, D), w_xk, w_xv)
    xp = x_prompt.reshape(NB * SEQ, D)
    sb_full = jnp.repeat(sgu_b[0].T, CHUNK, axis=1)
    xp, sc_p = _even_p(xp, row(norm_mix_pre[0]), w_in_even, row(sgu_vnorm[0]), sgu_w, sb_full, conv_short, w_out_even,
                       row(norm_mix_post[0]))
    cos_p, sin_p = _rope_tables(jnp.arange(SEQ, dtype=jnp.int32))
    ffn_p_states = []
    for l in range(2):
        if l == 1:
            xp, ret_p_state = _ret_p(xp, cos_p, sin_p, row(norm_mix_pre[1]), wq_o, wk_o, wv_o, wg_o, wout_o,
                                     row(norm_mix_post[1]))
        xp = _xattn_p(l, xp, row(norm_x_pre[l]), w_xq, mem_kb, mem_vb, w_xo, row(norm_x_post[l]))
        xp, st = _ffn_p(l, xp, row(norm_ffn_pre[l]), ffn_w[l][0], ffn_w[l][1], conv_ffn, ffn_w[l][2],
                        row(norm_ffn_post[l]))
        ffn_p_states.append(st[:, 6:8, :])

    return (xp.reshape(NB, SEQ, D), xs.reshape(NS, TS, D),
            _from_head_view(mem_k), _from_head_view(mem_v),
            sc_p[None, :, 6:8, :], _from_state_rows(sc_s)[None],
            v_s.reshape(1, NS, TS, SGU_W),
            ret_p_state[None], ret_s_state[None],
            jnp.stack(ffn_p_states), jnp.stack(ffn_s_states))
```

```python
import functools

import numpy as np
import jax
import jax.numpy as jnp
from jax import lax
from jax.experimental import pallas as pl
from jax.experimental.pallas import tpu as pltpu

F32 = jnp.float32
BF16 = jnp.bfloat16

D = 1024
SEQ = 2048
NB = 8
NS = 128
TS = 4
RS = NS * TS
PAST = 16384
CHUNK = 128
EPS = 1e-6
SGU_W = 512
SC_W = 512
RET_H = 4
RET_DK = 256
RET_DV = 512
NQ = RET_H * RET_DK
NV = RET_H * RET_DV
N_MEM = 256
XH = 4
XHD = 256
DFF = 2816
FC = 256
NFC = DFF // FC
ROPE_BASE = 10000.0

_LOG_G = [float(v) for v in np.log1p(-np.exp2(np.float32(-5.0) - np.arange(RET_H, dtype=np.float32))).astype(np.float32)]

_MIB = 1024 * 1024


def _rms(x, g):
    ms = jnp.mean(x * x, axis=-1, keepdims=True)
    return (x * lax.rsqrt(ms + EPS)) * g


_GELU_C = 0.7978845608028654
_LOG2E = 1.4426950408889634


def _gelu(x):
    k0 = -2.0 * _LOG2E * _GELU_C
    k1 = k0 * 0.044715
    return x * (1.0 / (1.0 + jnp.exp2(x * (k0 + k1 * (x * x)))))


def _dot(a, b):
    return jnp.dot(a, b, preferred_element_type=F32)


def _dot_nt(a, b):
    return lax.dot_general(a, b, (((1,), (1,)), ((), ())), preferred_element_type=F32)


def _const(shape):
    n = len(shape)
    return pl.BlockSpec(shape, lambda *_: (0,) * n, pipeline_mode=pl.Buffered(1))


def _layer_const(l, shape):
    n = len(shape)
    return pl.BlockSpec((None,) + tuple(shape), lambda *_: (l,) + (0,) * n, pipeline_mode=pl.Buffered(1))


def _full(shape):
    n = len(shape)
    return pl.BlockSpec(shape, lambda *_: (0,) * n)


def _params(sem, vmem_mib):
    return pltpu.CompilerParams(dimension_semantics=sem, vmem_limit_bytes=vmem_mib * _MIB)


def _conv3(p, p1, p2, w_ref, cols):
    return w_ref[0:1, cols] * p2 + w_ref[1:2, cols] * p1 + w_ref[2:3, cols] * p


def _shift_down(a, first_row):
    r = pltpu.roll(a, 1, 0)
    row8 = lax.broadcasted_iota(jnp.int32, (8, a.shape[1]), 0)
    top = jnp.where(row8 == 0, first_row, r[0:8])
    return jnp.concatenate([top, r[8:]], axis=0)


def _conv_tile(u, prev_ref, w_ref, cols):
    w0, w1, w2 = w_ref[0:1, cols], w_ref[1:2, cols], w_ref[2:3, cols]
    m1 = prev_ref[7:8, cols]
    m2 = prev_ref[6:7, cols]
    inner = _shift_down(w0 * u, w0 * m1) + w1 * u
    return _shift_down(inner, w0 * m2 + w1 * m1) + w2 * u


def _conv_rows(u, prev_ref, w_ref, cols, sh_ref):
    r, c = u.shape
    ys = []
    for j in range(c // 128):
        cj = slice(cols.start + j * 128, cols.start + (j + 1) * 128)
        uj = u[:, j * 128:(j + 1) * 128]
        sh_ref[j, 0:8, :] = prev_ref[:, cj]
        sh_ref[j, 8:r + 8, :] = uj
        u1 = sh_ref[j, 7:r + 7, :]
        u2 = sh_ref[j, 6:r + 6, :]
        ys.append((w_ref[0:1, cj] * u2 + w_ref[1:2, cj] * u1) + w_ref[2:3, cj] * uj)
    return jnp.concatenate(ys, axis=1)


def _shift_sample(p, e1, e2):
    t = lax.broadcasted_iota(jnp.int32, p.shape, 0) & (TS - 1)
    p1 = jnp.where(t >= 1, pltpu.roll(p, 1, 0), e1)
    p2 = jnp.where(t >= 2, pltpu.roll(p, 2, 0), e2)
    return p1, p2


def _memkv_body(mem_ref, g_ref, wk_ref, wv_ref, k_ref, v_ref, kb_ref, vb_ref, *, nb):
    mn = _rms(mem_ref[...], g_ref[...]).astype(BF16)
    for w_ref, o_ref, ob_ref in ((wk_ref, k_ref, kb_ref), (wv_ref, v_ref, vb_ref)):
        y = _dot(mn, w_ref[...].astype(BF16))
        ob_ref[...] = y.astype(BF16)
        for b in range(nb):
            for hd in range(XH):
                for j in range(2):
                    c0 = hd * XHD + j * 128
                    o_ref[b, pl.ds(j * XH + hd, N_MEM, stride=8), :] = y[b * N_MEM:(b + 1) * N_MEM, c0:c0 + 128]


def _memkv(mem, g, wk, wv):
    rows = mem.shape[0]
    nb = 2
    tm = nb * N_MEM
    o_spec = pl.BlockSpec((None, nb, N_MEM * 8, 128), lambda l, i: (l, i, 0, 0))
    ob_spec = pl.BlockSpec((None, tm, D), lambda l, i: (l, i, 0))
    w_spec = pl.BlockSpec((None, D, D), lambda l, i: (l, 0, 0))
    return pl.pallas_call(
        functools.partial(_memkv_body, nb=nb),
        grid=(2, rows // tm),
        in_specs=[pl.BlockSpec((tm, D), lambda l, i: (i, 0)),
                  pl.BlockSpec((None, 1, D), lambda l, i: (l, 0, 0)),
                  w_spec, w_spec],
        out_specs=[o_spec, o_spec, ob_spec, ob_spec],
        out_shape=[jax.ShapeDtypeStruct((2, NB, N_MEM * 8, 128), F32)] * 2
        + [jax.ShapeDtypeStruct((2, rows, D), BF16)] * 2,
        compiler_params=_params(("arbitrary", "arbitrary"), 40),
        name="memkv",
    )(mem, g, wk, wv)


def _from_head_view(c):
    s = c.shape
    return c.reshape(s[0], s[1], N_MEM, 2, XH, 128).transpose(0, 1, 2, 4, 3, 5).reshape(s[0], s[1], N_MEM, XH, XHD)


def _even_front(x, gpre_ref, win, vn_ref):
    h = _rms(x, gpre_ref[...]).astype(BF16)
    u = _gelu(_dot(h, win(0, 512)))
    v = _rms(_gelu(_dot(h, win(512, 1024))), vn_ref[...])
    bg = _dot(h, win(1024, 1536))
    p = _dot(h, win(1536, 2048)) * _dot(h, win(2048, 2560))
    return u, v, bg, p


def _even_p_body(x_ref, gpre_ref, winf_ref, vn_ref, sw_ref, sb_ref, cw_ref, woutf_ref, gpost_ref,
                 xo_ref, st_ref, win_ref, wout_ref, carry_ref, cat_ref, sh_ref, *, tm):
    @pl.when((pl.program_id(0) == 0) & (pl.program_id(1) == 0))
    def _():
        win_ref[...] = winf_ref[...].astype(BF16)
        wout_ref[...] = woutf_ref[...].astype(BF16)

    @pl.when(pl.program_id(1) == 0)
    def _():
        carry_ref[...] = jnp.zeros_like(carry_ref)

    x = x_ref[...]
    u, v, bg, p = _even_front(x, gpre_ref, lambda c0, c1: win_ref[:, c0:c1], vn_ref)
    vb = v.astype(BF16)
    ri = lax.broadcasted_iota(jnp.int32, (CHUNK, CHUNK), 0)
    ci = lax.broadcasted_iota(jnp.int32, (CHUNK, CHUNK), 1)
    for g in range(4):
        cols = slice(g * 128, (g + 1) * 128)
        w = jnp.where(ri >= ci, sw_ref[g], 0.0).astype(BF16)
        for c in range(tm // CHUNK):
            rows = slice(c * CHUNK, (c + 1) * CHUNK)
            mixed = _dot(w, vb[rows, cols]) + sb_ref[:, cols]
            cat_ref[rows, cols] = (u[rows, cols] * mixed).astype(BF16)
    cz = _conv_rows(p, carry_ref, cw_ref, slice(0, SC_W), sh_ref)
    cat_ref[:, 512:1024] = (bg * cz).astype(BF16)
    carry_ref[...] = p[tm - 8:tm, :]
    st_ref[...] = p[tm - 8:tm, :]
    out = _dot(cat_ref[...], wout_ref[...])
    xo_ref[...] = x + _rms(out, gpost_ref[...])


def _even_p(x, gpre, win, vn, sw, sb, cw, wout, gpost):
    tm = 512
    nt = SEQ // tm
    row_spec = pl.BlockSpec((tm, D), lambda b, t: (b * nt + t, 0))
    return pl.pallas_call(
        functools.partial(_even_p_body, tm=tm),
        grid=(NB, nt),
        in_specs=[row_spec, _const((1, D)), _layer_const(0, (D, 2560)), _const((1, SGU_W)),
                  _layer_const(0, (4, CHUNK, CHUNK)), _const((CHUNK, SGU_W)), _layer_const(0, (3, SC_W)),
                  _layer_const(0, (D, D)), _const((1, D))],
        out_specs=[row_spec, pl.BlockSpec((None, 8, SC_W), lambda b, t: (b, 0, 0))],
        out_shape=[jax.ShapeDtypeStruct((NB * SEQ, D), F32), jax.ShapeDtypeStruct((NB, 8, SC_W), F32)],
        scratch_shapes=[pltpu.VMEM((D, 2560), BF16), pltpu.VMEM((D, D), BF16), pltpu.VMEM((8, SC_W), F32),
                        pltpu.VMEM((tm, D), BF16), pltpu.VMEM((SC_W // 128, tm + 8, 128), F32)],
        compiler_params=_params(("arbitrary", "arbitrary"), 48),
        name="even_p",
    )(x, gpre, win, vn, sw, sb, cw, wout, gpost)


def _even_s_body(x_ref, gpre_ref, win_ref, vn_ref, gw_ref, gb_ref, cw_ref, st_ref, wout_ref, gpost_ref,
                 xo_ref, ns_ref, v_ref, pan_ref):
    xt = _load_time_major(x_ref, pan_ref)
    u, v, bg, p = _even_front(xt, gpre_ref, lambda c0, c1: win_ref[:, c0:c1].astype(BF16), vn_ref)
    _store_batch_major(v_ref, v, pan_ref)
    vt = [v[t * NS:(t + 1) * NS, :] for t in range(TS)]
    mixed = []
    for t in range(TS):
        m = gb_ref[t:t + 1, :] + gw_ref[t, 0:1, :] * vt[0]
        for s in range(1, t + 1):
            m = m + gw_ref[t, s:s + 1, :] * vt[s]
        mixed.append(m)
    a = u * jnp.concatenate(mixed, axis=0)
    prev = [jnp.concatenate([st_ref[j * 2 + r] for j in range(SC_W // 128)], axis=1) for r in range(2)]
    cz = _conv_time_major(p, prev, cw_ref)
    for j in range(SC_W // 128):
        for r in range(2):
            ns_ref[j * 2 + r] = p[(TS - 2 + r) * NS:(TS - 1 + r) * NS, j * 128:(j + 1) * 128]
    cat = jnp.concatenate([a, bg * cz], axis=1).astype(BF16)
    out = _dot(cat, wout_ref[...].astype(BF16))
    _store_batch_major(xo_ref, xt + _rms(out, gpost_ref[...]), pan_ref)


def _even_s(x, gpre, win, vn, gw, gb, cw, st, wout, gpost):
    nst = SC_W // 128 * 2
    return pl.pallas_call(
        _even_s_body,
        grid=(1,),
        in_specs=[_const((RS, D)), _const((1, D)), _layer_const(0, (D, 2560)), _const((1, SGU_W)),
                  _const((TS, TS, SGU_W)), _const((TS, SGU_W)), _layer_const(0, (3, SC_W)), _const((nst, NS, 128)),
                  _layer_const(0, (D, D)), _const((1, D))],
        out_specs=[_full((RS, D)), _full((nst, NS, 128)), _full((RS, SGU_W))],
        out_shape=[jax.ShapeDtypeStruct((RS, D), F32), jax.ShapeDtypeStruct((nst, NS, 128), F32),
                   jax.ShapeDtypeStruct((RS, SGU_W), F32)],
        scratch_shapes=[pltpu.VMEM((D // 128, RS, 128), F32)],
        compiler_params=_params(("arbitrary",), 48),
        name="even_s",
    )(x, gpre, win, vn, gw, gb, cw, st, wout, gpost)


def _softmax_rows(s):
    m = jnp.max(s, axis=-1, keepdims=True)
    e = jnp.exp(s - m)
    return e * (1.0 / jnp.sum(e, axis=-1, keepdims=True))


def _xattn_p_body(x_ref, gpre_ref, wqf_ref, k_ref, v_ref, wof_ref, gpost_ref, xo_ref, wq_ref, wo_ref, q_ref, s_ref,
                  p_ref, o_ref):
    @pl.when((pl.program_id(0) == 0) & (pl.program_id(1) == 0))
    def _():
        wq_ref[...] = wqf_ref[...].astype(BF16)
        wo_ref[...] = wof_ref[...].astype(BF16)

    h = _rms(x_ref[...], gpre_ref[...]).astype(BF16)
    q_ref[...] = _dot(h, wq_ref[...]).astype(BF16)
    for hd in range(XH):
        cols = slice(hd * XHD, (hd + 1) * XHD)
        s_ref[hd] = _dot_nt(q_ref[:, cols], k_ref[:, cols]) * (XHD ** -0.5)
    p_ref[...] = _softmax_rows(s_ref[...]).astype(BF16)
    for hd in range(XH):
        cols = slice(hd * XHD, (hd + 1) * XHD)
        o_ref[:, cols] = _dot(p_ref[hd], v_ref[:, cols]).astype(BF16)
    out = _dot(o_ref[...], wo_ref[...])
    xo_ref[...] = x_ref[...] + _rms(out, gpost_ref[...])


def _xattn_p_parts(l, tm):
    nt = SEQ // tm
    row_spec = pl.BlockSpec((tm, D), lambda b, t: (b * nt + t, 0))
    kv_spec = pl.BlockSpec((None, N_MEM, D), lambda b, t: (l, b, 0))
    in_specs = [row_spec, _const((1, D)), _layer_const(l, (D, D)), kv_spec, kv_spec, _layer_const(l, (D, D)),
                _const((1, D))]
    scratch = [pltpu.VMEM((D, D), BF16), pltpu.VMEM((D, D), BF16), pltpu.VMEM((tm, D), BF16),
               pltpu.VMEM((XH, tm, N_MEM), F32), pltpu.VMEM((XH, tm, N_MEM), BF16), pltpu.VMEM((tm, D), BF16)]
    return nt, row_spec, in_specs, scratch


def _xattn_p(l, x, gpre, wq, kb, vb, wo, gpost):
    nt, row_spec, in_specs, scratch = _xattn_p_parts(l, 512)
    return pl.pallas_call(
        _xattn_p_body,
        grid=(NB, nt),
        in_specs=in_specs,
        out_specs=row_spec,
        out_shape=jax.ShapeDtypeStruct((NB * SEQ, D), F32),
        scratch_shapes=scratch,
        compiler_params=_params(("arbitrary", "arbitrary"), 48),
        name="xattn_p",
    )(x, gpre, wq, kb, vb, wo, gpost)


def _head_view(c):
    s = c.shape
    return c.reshape(s[0], s[1], N_MEM, XH, 2, 128).transpose(0, 1, 2, 4, 3, 5).reshape(s[0], s[1], N_MEM * 8, 128)


def _head_rows(ref, b, hd):
    halves = [ref[b, pl.ds(j * XH + hd, N_MEM, stride=8), :] for j in range(2)]
    return jnp.concatenate(halves, axis=1).astype(BF16)


def _xattn_s_body(x_ref, gpre_ref, wq_ref, k_ref, v_ref, wo_ref, gpost_ref, xo_ref, q_ref, o_ref, s_ref, *, bb):
    step = pl.program_id(0)

    @pl.when(step == 0)
    def _():
        h = _rms(x_ref[...], gpre_ref[...]).astype(BF16)
        q_ref[...] = _dot(h, wq_ref[...].astype(BF16))

    first = (lax.broadcasted_iota(jnp.int32, (8, XHD), 0) >> 2) == 0
    groups = [(pi, hd, bi) for pi in range(bb // 2) for hd in range(XH) for bi in range(2)]
    row0 = [pl.multiple_of((step * (bb // 2) + pi) * 8, 8) for pi in range(bb // 2)]
    for gi, (pi, hd, bi) in enumerate(groups):
        q8 = q_ref[pl.ds(row0[pi], 8), hd * XHD:(hd + 1) * XHD].astype(BF16)
        kh = _head_rows(k_ref, pi * 2 + bi, hd)
        s_ref[gi * 8:(gi + 1) * 8, :] = _dot_nt(q8, kh) * (XHD ** -0.5)
    s_ref[...] = _softmax_rows(s_ref[...])
    for gi, (pi, hd, bi) in enumerate(groups):
        if bi == 1:
            continue
        pv = [_dot(s_ref[(gi + b) * 8:(gi + b + 1) * 8, :].astype(BF16), _head_rows(v_ref, pi * 2 + b, hd))
              for b in range(2)]
        o_ref[pl.ds(row0[pi], 8), hd * XHD:(hd + 1) * XHD] = jnp.where(first, pv[0], pv[1])

    @pl.when(step == pl.num_programs(0) - 1)
    def _():
        out = _dot(o_ref[...].astype(BF16), wo_ref[...].astype(BF16))
        xo_ref[...] = x_ref[...] + _rms(out, gpost_ref[...])


def _xattn_s(l, x, gpre, wq, k, v, wo, gpost):
    bb = 4
    kv_spec = pl.BlockSpec((None, bb, N_MEM * 8, 128), lambda i: (l, i, 0, 0))
    return pl.pallas_call(
        functools.partial(_xattn_s_body, bb=bb),
        grid=(NS // bb,),
        in_specs=[_const((RS, D)), _const((1, D)), _layer_const(l, (D, D)), kv_spec, kv_spec, _layer_const(l, (D, D)),
                  _const((1, D))],
        out_specs=_full((RS, D)),
        out_shape=jax.ShapeDtypeStruct((RS, D), F32),
        scratch_shapes=[pltpu.VMEM((RS, D), F32), pltpu.VMEM((RS, D), F32), pltpu.VMEM((bb * XH * 8, N_MEM), F32)],
        compiler_params=_params(("arbitrary",), 48),
        name="xattn_s",
    )(x, gpre, wq, k, v, wo, gpost)


_DOWN_SPLITS = (0, NFC)


def _ffn_p_body(x_ref, gpre_ref, wg_ref, wv_ref, cw_ref, wdn_ref, gpost_ref, xo_ref, st_ref, carry_ref, h_ref, act_ref,
                *sh_refs, tm):
    @pl.when(pl.program_id(1) == 0)
    def _():
        carry_ref[...] = jnp.zeros_like(carry_ref)

    h_ref[...] = _rms(x_ref[...], gpre_ref[...]).astype(BF16)
    for c in range(NFC):
        halves = []
        for i, (base, w_ref) in enumerate(((0, wg_ref), (DFF, wv_ref))):
            cols = slice(base + c * FC, base + (c + 1) * FC)
            up = _dot(h_ref[...], w_ref[:, c * FC:(c + 1) * FC])
            halves.append(_conv_rows(up, carry_ref, cw_ref, cols, sh_refs[(2 * c + i) % len(sh_refs)]))
            carry_ref[:, cols] = up[tm - 8:tm, :]
            st_ref[:, cols] = up[tm - 8:tm, :]
        act_ref[:, c * FC:(c + 1) * FC] = (_gelu(halves[0]) * halves[1]).astype(BF16)
    out = _dot(act_ref[...], wdn_ref[...])
    xo_ref[...] = x_ref[...] + _rms(out, gpost_ref[...])


def _ffn_p(l, x, gpre, wg, wv, cw, wdn, gpost):
    tm = 512
    nt = SEQ // tm
    row_spec = pl.BlockSpec((tm, D), lambda b, t: (b * nt + t, 0))
    return pl.pallas_call(
        functools.partial(_ffn_p_body, tm=tm),
        grid=(NB, nt),
        in_specs=[row_spec, _const((1, D)), _const((D, DFF)), _const((D, DFF)), _layer_const(l, (3, 2 * DFF)),
                  _const((DFF, D)), _const((1, D))],
        out_specs=[row_spec, pl.BlockSpec((None, 8, 2 * DFF), lambda b, t: (b, 0, 0))],
        out_shape=[jax.ShapeDtypeStruct((NB * SEQ, D), F32), jax.ShapeDtypeStruct((NB, 8, 2 * DFF), F32)],
        scratch_shapes=[pltpu.VMEM((8, 2 * DFF), F32), pltpu.VMEM((tm, D), BF16), pltpu.VMEM((tm, DFF), BF16)]
        + [pltpu.VMEM((FC // 128, tm + 8, 128), F32)] * 4,
        compiler_params=_params(("arbitrary", "arbitrary"), 52),
        name="ffn_p",
    )(x, gpre, wg, wv, cw, wdn, gpost)


def _load_time_major(x_ref, pan_ref):
    n = x_ref.shape[1] // 128
    for p in range(n):
        pan_ref[p] = x_ref[:, p * 128:(p + 1) * 128]
    return jnp.concatenate(
        [jnp.concatenate([pan_ref[p, pl.ds(t, NS, stride=TS), :] for p in range(n)], axis=1) for t in range(TS)],
        axis=0)


def _store_batch_major(o_ref, y, pan_ref):
    n = o_ref.shape[1] // 128
    for p in range(n):
        for t in range(TS):
            pan_ref[p, pl.ds(t, NS, stride=TS), :] = y[t * NS:(t + 1) * NS, p * 128:(p + 1) * 128]
    for p in range(n):
        o_ref[:, p * 128:(p + 1) * 128] = pan_ref[p]


def _conv_time_major(u, prev, w_ref):
    blocks = list(prev) + [u[t * NS:(t + 1) * NS, :] for t in range(TS)]
    w0, w1, w2 = w_ref[0:1, :], w_ref[1:2, :], w_ref[2:3, :]
    return jnp.concatenate([(w0 * blocks[t] + w1 * blocks[t + 1]) + w2 * blocks[t + 2] for t in range(TS)], axis=0)


def _ffn_s_body(x_ref, gpre_ref, wg_ref, wv_ref, cg_ref, cv_ref, sg_ref, sv_ref, wdn_ref, gpost_ref,
                xo_ref, ng_ref, nv_ref, wgb_ref, wvb_ref, wdb_ref, xt_ref, h_ref, acc_ref, pan_ref):
    c = pl.program_id(0)

    @pl.when(c == 0)
    def _():
        xt = _load_time_major(x_ref, pan_ref)
        xt_ref[...] = xt
        h_ref[...] = _rms(xt, gpre_ref[...]).astype(BF16)
        acc_ref[...] = jnp.zeros_like(acc_ref)

    wgb_ref[...] = wg_ref[...].astype(BF16)
    wvb_ref[...] = wv_ref[...].astype(BF16)
    wdb_ref[...] = wdn_ref[...].astype(BF16)
    ys = []
    for wb_ref, cw_ref, s_ref, n_ref in ((wgb_ref, cg_ref, sg_ref, ng_ref), (wvb_ref, cv_ref, sv_ref, nv_ref)):
        u = _dot(h_ref[...], wb_ref[...])
        prev = [jnp.concatenate([s_ref[j * 2 + r] for j in range(FC // 128)], axis=1) for r in range(2)]
        ys.append(_conv_time_major(u, prev, cw_ref))
        for j in range(FC // 128):
            for r in range(2):
                n_ref[j * 2 + r] = u[(TS - 2 + r) * NS:(TS - 1 + r) * NS, j * 128:(j + 1) * 128]
    act = (_gelu(ys[0]) * ys[1]).astype(BF16)
    acc_ref[...] += _dot(act, wdb_ref[...])

    @pl.when(c == pl.num_programs(0) - 1)
    def _():
        _store_batch_major(xo_ref, xt_ref[...] + _rms(acc_ref[...], gpost_ref[...]), pan_ref)


def _ffn_s(l, x, gpre, wup, cw, st, wdn, gpost):
    rpc = FC // 128 * 2
    gcol = lambda r: pl.BlockSpec((None, r, FC), lambda c: (l, 0, c))
    vcol = lambda r: pl.BlockSpec((None, r, FC), lambda c: (l, 0, NFC + c))
    sg_in = pl.BlockSpec((None, rpc, NS, 128), lambda c: (l, c, 0, 0))
    sv_in = pl.BlockSpec((None, rpc, NS, 128), lambda c: (l, NFC + c, 0, 0))
    s_out = pl.BlockSpec((rpc, NS, 128), lambda c: (c, 0, 0))
    wb_out = pl.BlockSpec((D, FC), lambda c: (0, c))
    return pl.pallas_call(
        _ffn_s_body,
        grid=(NFC,),
        in_specs=[_const((RS, D)), _const((1, D)), gcol(D), vcol(D), gcol(3), vcol(3), sg_in, sv_in,
                  pl.BlockSpec((None, FC, D), lambda c: (l, c, 0)), _const((1, D))],
        out_specs=[_full((RS, D)), s_out, s_out, wb_out, wb_out, pl.BlockSpec((FC, D), lambda c: (c, 0))],
        out_shape=[jax.ShapeDtypeStruct((RS, D), F32),
                   jax.ShapeDtypeStruct((NFC * rpc, NS, 128), F32), jax.ShapeDtypeStruct((NFC * rpc, NS, 128), F32),
                   jax.ShapeDtypeStruct((D, DFF), BF16), jax.ShapeDtypeStruct((D, DFF), BF16),
                   jax.ShapeDtypeStruct((DFF, D), BF16)],
        scratch_shapes=[pltpu.VMEM((RS, D), F32), pltpu.VMEM((RS, D), BF16), pltpu.VMEM((RS, D), F32),
                        pltpu.VMEM((D // 128, RS, 128), F32)],
        compiler_params=_params(("arbitrary",), 40),
        name="ffn_s",
    )(x, gpre, wup, wup, cw, cw, st, st, wdn, gpost)


def _rope_head(z, cos, sin, base):
    x1 = z[:, base:base + 128]
    x2 = z[:, base + 128:base + 256]
    return x1 * cos - x2 * sin, x1 * sin + x2 * cos


def _group_norm_gate(o, g):
    mu = jnp.mean(o, axis=-1, keepdims=True)
    d = o - mu
    var = jnp.mean(d * d, axis=-1, keepdims=True)
    return (g * jax.nn.sigmoid(g)) * (d * lax.rsqrt(var + EPS))


RCH = 256


def _ret_p_body(x_ref, cos_ref, sin_ref, gpre_ref, wq_ref, wk_ref, wv_ref, wg_ref, wout_ref, gpost_ref, xo_ref, s_ref,
                h_ref, q_ref, k_ref, kd_ref, v_ref, g_ref, y_ref, dm_ref, qd_ref, kdec_ref, *, tm):
    first = (pl.program_id(0) == 0) & (pl.program_id(1) == 0)

    @pl.when(first)
    def _():
        i = lax.broadcasted_iota(jnp.int32, (RCH, RCH), 0).astype(F32)
        j = lax.broadcasted_iota(jnp.int32, (RCH, RCH), 1).astype(F32)
        i1 = lax.broadcasted_iota(jnp.int32, (RCH, 128), 0).astype(F32)
        for hd in range(RET_H):
            lg = _LOG_G[hd]
            dm_ref[hd] = jnp.where(i >= j, jnp.exp(lg * jnp.maximum(i - j, 0.0)), 0.0)
            qd_ref[hd] = jnp.exp(lg * (i1 + 1.0))
            kdec_ref[hd] = jnp.exp(lg * (RCH - 1.0 - i1))

    @pl.when(pl.program_id(1) == 0)
    def _():
        s_ref[...] = jnp.zeros_like(s_ref)

    h_ref[...] = _rms(x_ref[...], gpre_ref[...]).astype(BF16)
    cos = cos_ref[...]
    sin = sin_ref[...]
    nchunk = tm // RCH
    for hd in range(RET_H):
        base = hd * RET_DK
        zq = _dot(h_ref[...], wq_ref[:, base:base + RET_DK])
        zk = _dot(h_ref[...], wk_ref[:, base:base + RET_DK])
        q1, q2 = _rope_head(zq, cos, sin, 0)
        k1, k2 = _rope_head(zk, cos, sin, 0)
        q_ref[:, base:base + 128] = (q1 * (RET_DK ** -0.5)).astype(BF16)
        q_ref[:, base + 128:base + 256] = (q2 * (RET_DK ** -0.5)).astype(BF16)
        k_ref[:, base:base + 128] = k1.astype(BF16)
        k_ref[:, base + 128:base + 256] = k2.astype(BF16)
        kdec = jnp.concatenate([kdec_ref[hd]] * nchunk, axis=0)
        kd_ref[:, base:base + 128] = k1 * kdec
        kd_ref[:, base + 128:base + 256] = k2 * kdec
        vcols = slice(hd * RET_DV, (hd + 1) * RET_DV)
        v_ref[:, vcols] = _dot(h_ref[...], wv_ref[:, vcols]).astype(BF16)
        g_ref[:, vcols] = _dot(h_ref[...], wg_ref[:, vcols])
    for hd in range(RET_H):
        kcols = slice(hd * RET_DK, (hd + 1) * RET_DK)
        vcols = slice(hd * RET_DV, (hd + 1) * RET_DV)
        decay_l = float(np.exp(np.float32(_LOG_G[hd]) * np.float32(RCH)))
        qd = qd_ref[hd]
        qd4 = jnp.concatenate([qd, qd, qd, qd], axis=1)
        for c in range(nchunk):
            rows = slice(c * RCH, (c + 1) * RCH)
            qc = q_ref[rows, kcols]
            vc = v_ref[rows, vcols]
            s_prev = s_ref[hd]
            sc = (_dot_nt(qc, k_ref[rows, kcols]) * dm_ref[hd]).astype(BF16)
            o = _dot(sc, vc) + _dot(qc, s_prev.astype(BF16)) * qd4
            kdt = kd_ref[rows, kcols].T.astype(BF16)
            s_ref[hd] = decay_l * s_prev + _dot(kdt, vc)
            y_ref[rows, vcols] = _group_norm_gate(o, g_ref[rows, vcols]).astype(BF16)
    out = _dot(y_ref[...], wout_ref[...])
    xo_ref[...] = x_ref[...] + _rms(out, gpost_ref[...])


def _ret_p(x, cos, sin, gpre, wq, wk, wv, wg, wout, gpost):
    tm = 512
    nt = SEQ // tm
    row_spec = pl.BlockSpec((tm, D), lambda b, t: (b * nt + t, 0))
    cs_spec = pl.BlockSpec((tm, 128), lambda b, t: (t, 0))
    return pl.pallas_call(
        functools.partial(_ret_p_body, tm=tm),
        grid=(NB, nt),
        in_specs=[row_spec, cs_spec, cs_spec, _const((1, D)), _const((D, NQ)), _const((D, NQ)), _const((D, NV)),
                  _const((D, NV)), _const((NV, D)), _const((1, D))],
        out_specs=[row_spec, pl.BlockSpec((None, RET_H, RET_DK, RET_DV), lambda b, t: (b, 0, 0, 0))],
        out_shape=[jax.ShapeDtypeStruct((NB * SEQ, D), F32), jax.ShapeDtypeStruct((NB, RET_H, RET_DK, RET_DV), F32)],
        scratch_shapes=[pltpu.VMEM((tm, D), BF16), pltpu.VMEM((tm, NQ), BF16), pltpu.VMEM((tm, NQ), BF16),
                        pltpu.VMEM((tm, NQ), F32), pltpu.VMEM((tm, NV), BF16), pltpu.VMEM((tm, NV), F32),
                        pltpu.VMEM((tm, NV), BF16), pltpu.VMEM((RET_H, RCH, RCH), F32),
                        pltpu.VMEM((RET_H, RCH, 128), F32), pltpu.VMEM((RET_H, RCH, 128), F32)],
        compiler_params=_params(("arbitrary", "arbitrary"), 56),
        name="ret_p",
    )(x, cos, sin, gpre, wq, wk, wv, wg, wout, gpost)


def _ret_s_proj_body(x_ref, cos_ref, sin_ref, gpre_ref, wqf_ref, wkf_ref, wvf_ref, wgf_ref,
                     q_ref, kdt_ref, v_ref, g_ref, oi_ref, wq_ref, wk_ref, wv_ref, wg_ref, h_ref):
    hd = pl.program_id(0)

    @pl.when(hd == 0)
    def _():
        h_ref[...] = _rms(x_ref[...], gpre_ref[...]).astype(BF16)

    for wf_ref, wb_ref in ((wqf_ref, wq_ref), (wkf_ref, wk_ref), (wvf_ref, wv_ref), (wgf_ref, wg_ref)):
        wb_ref[...] = wf_ref[...].astype(BF16)
    lg = jnp.where(hd == 0, _LOG_G[0], jnp.where(hd == 1, _LOG_G[1], jnp.where(hd == 2, _LOG_G[2], _LOG_G[3])))
    h = h_ref[...]
    cos = cos_ref[...]
    sin = sin_ref[...]
    ti = (lax.broadcasted_iota(jnp.int32, (RS, 128), 0) & (TS - 1)).astype(F32)
    i = lax.broadcasted_iota(jnp.int32, (CHUNK, CHUNK), 0)
    j = lax.broadcasted_iota(jnp.int32, (CHUNK, CHUNK), 1)
    same = ((i >> 2) == (j >> 2)) & (i >= j)
    dm = jnp.where(same, jnp.exp(lg * jnp.maximum(i - j, 0).astype(F32)), 0.0)
    q1, q2 = _rope_head(_dot(h, wq_ref[...]), cos, sin, 0)
    k1, k2 = _rope_head(_dot(h, wk_ref[...]), cos, sin, 0)
    q = jnp.concatenate([q1, q2], axis=1) * (RET_DK ** -0.5)
    k = jnp.concatenate([k1, k2], axis=1)
    q_ref[...] = q
    kdec = jnp.exp(lg * (TS - 1.0 - ti))
    kd = k * jnp.concatenate([kdec, kdec], axis=1)
    vf = _dot(h, wv_ref[...])
    v_ref[...] = vf
    v = vf.astype(BF16)
    g_ref[...] = _dot(h, wg_ref[...])
    qb = q.astype(BF16)
    kb = k.astype(BF16)
    for tl in range(RS // CHUNK):
        rows = slice(tl * CHUNK, (tl + 1) * CHUNK)
        sc = (_dot_nt(qb[rows], kb[rows]) * dm).astype(BF16)
        oi_ref[rows, :] = _dot(sc, v[rows])
        kdt_ref[tl] = kd[rows].T.astype(BF16)


def _ret_s_proj(x, cos, sin, gpre, win):
    def wcol(width, first_block):
        return pl.BlockSpec((None, D, width), lambda h: (0, 0, first_block + h))
    def hcol(rows, width):
        return pl.BlockSpec((rows, width), lambda h: (0, h))
    return pl.pallas_call(
        _ret_s_proj_body,
        grid=(RET_H,),
        in_specs=[_const((RS, D)), _const((RS, 128)), _const((RS, 128)), _const((1, D)),
                  wcol(RET_DK, 0), wcol(RET_DK, NQ // RET_DK), wcol(RET_DV, 2 * NQ // RET_DV),
                  wcol(RET_DV, (2 * NQ + NV) // RET_DV)],
        out_specs=[hcol(RS, RET_DK),
                   pl.BlockSpec((RS // CHUNK, None, RET_DK, CHUNK), lambda h: (0, h, 0, 0)),
                   hcol(RS, RET_DV), hcol(RS, RET_DV), hcol(RS, RET_DV),
                   hcol(D, RET_DK), hcol(D, RET_DK), hcol(D, RET_DV), hcol(D, RET_DV)],
        out_shape=[jax.ShapeDtypeStruct((RS, NQ), F32),
                   jax.ShapeDtypeStruct((RS // CHUNK, RET_H, RET_DK, CHUNK), BF16),
                   jax.ShapeDtypeStruct((RS, NV), F32), jax.ShapeDtypeStruct((RS, NV), F32),
                   jax.ShapeDtypeStruct((RS, NV), F32),
                   jax.ShapeDtypeStruct((D, NQ), BF16), jax.ShapeDtypeStruct((D, NQ), BF16),
                   jax.ShapeDtypeStruct((D, NV), BF16), jax.ShapeDtypeStruct((D, NV), BF16)],
        scratch_shapes=[pltpu.VMEM((RS, D), BF16)],
        compiler_params=_params(("arbitrary",), 40),
        name="ret_s_proj",
    )(x, cos, sin, gpre, win, win, win, win)


def _ret_s_core_body(step, q_ref, kdt_ref, v_ref, oi_ref, s_ref, o_ref, so_ref):
    pair_in_tile = step % (CHUNK // 8)
    row8 = lax.broadcasted_iota(jnp.int32, (8, RET_DK), 0)
    row128 = lax.broadcasted_iota(jnp.int32, (CHUNK, RET_DV), 0)
    t8 = (lax.broadcasted_iota(jnp.int32, (8, RET_DV), 0) & (TS - 1)).astype(F32)
    for hd in range(RET_H):
        lg = _LOG_G[hd]
        decay_l = float(np.exp(np.float32(lg) * np.float32(TS)))
        q8 = q_ref[:, hd * RET_DK:(hd + 1) * RET_DK]
        v128 = v_ref[:, hd * RET_DV:(hd + 1) * RET_DV]
        kdt = kdt_ref[hd]
        inter = jnp.zeros((8, RET_DV), F32)
        for bi in range(2):
            s_prev = s_ref[bi, hd]
            qm = jnp.where((row8 >> 2) == bi, q8, 0.0).astype(BF16)
            inter = inter + _dot(qm, s_prev.astype(BF16))
            vm = jnp.where((row128 >> 2) == pair_in_tile * 2 + bi, v128, 0.0).astype(BF16)
            so_ref[bi, hd] = decay_l * s_prev + _dot(kdt, vm)
        cols = slice(hd * RET_DV, (hd + 1) * RET_DV)
        o_ref[:, cols] = oi_ref[:, cols] + inter * jnp.exp(lg * (t8 + 1.0))


def _ret_s_core_parts():
    ppt = CHUNK // 8
    s_spec = pl.BlockSpec((2, RET_H, RET_DK, RET_DV), lambda i: (i, 0, 0, 0))
    in_specs = [pl.BlockSpec((8, NQ), lambda i: (i, 0)),
                pl.BlockSpec((None, RET_H, RET_DK, CHUNK), lambda i: (i // ppt, 0, 0, 0)),
                pl.BlockSpec((CHUNK, NV), lambda i: (i // ppt, 0)),
                pl.BlockSpec((8, NV), lambda i: (i, 0)),
                s_spec]
    out_specs = [pl.BlockSpec((8, NV), lambda i: (i, 0)), s_spec]
    out_shape = [jax.ShapeDtypeStruct((RS, NV), F32), jax.ShapeDtypeStruct((NS, RET_H, RET_DK, RET_DV), F32)]
    return in_specs, out_specs, out_shape


def _xattn_p_ret_s_body(*refs):
    host_in, guest_in = refs[0:7], refs[7:12]
    xo_ref, guest_out, scratch = refs[12], refs[13:15], refs[15:]
    _xattn_p_body(*host_in, xo_ref, *scratch)
    step = pl.program_id(0) * pl.num_programs(1) + pl.program_id(1)
    _ret_s_core_body(step, *guest_in, *guest_out)


def _xattn_p_ret_s(l, x, gpre, wq, kb, vb, wo, gpost, q, kdt, v, oi, s):
    steps = NS // 2
    tm = NB * SEQ // steps
    nt, row_spec, h_in, scratch = _xattn_p_parts(l, tm)
    g_in, g_out, g_shape = _ret_s_core_parts()

    def on_grid(spec):
        return pl.BlockSpec(spec.block_shape, lambda b, t, m=spec.index_map: m(b * nt + t))

    return pl.pallas_call(
        _xattn_p_ret_s_body,
        grid=(NB, nt),
        in_specs=h_in + [on_grid(sp) for sp in g_in],
        out_specs=[row_spec] + [on_grid(sp) for sp in g_out],
        out_shape=[jax.ShapeDtypeStruct((NB * SEQ, D), F32)] + g_shape,
        scratch_shapes=scratch,
        compiler_params=_params(("arbitrary", "arbitrary"), 56),
        name="xattn_p_ret_s",
    )(x, gpre, wq, kb, vb, wo, gpost, q, kdt, v, oi, s)


def _ret_s_out_body(x_ref, o_ref, g_ref, woutf_ref, gpost_ref, xo_ref, wout_ref, y_ref):
    wout_ref[...] = woutf_ref[...].astype(BF16)
    for hd in range(RET_H):
        cols = slice(hd * RET_DV, (hd + 1) * RET_DV)
        y_ref[:, cols] = _group_norm_gate(o_ref[:, cols], g_ref[:, cols]).astype(BF16)
    out = _dot(y_ref[...], wout_ref[...])
    xo_ref[...] = x_ref[...] + _rms(out, gpost_ref[...])


def _ret_s_out(x, o, g, wout, gpost):
    return pl.pallas_call(
        _ret_s_out_body,
        grid=(1,),
        in_specs=[_const((RS, D)), _const((RS, NV)), _const((RS, NV)), _layer_const(0, (NV, D)), _const((1, D))],
        out_specs=[_full((RS, D)), _full((NV, D))],
        out_shape=[jax.ShapeDtypeStruct((RS, D), F32), jax.ShapeDtypeStruct((NV, D), BF16)],
        scratch_shapes=[pltpu.VMEM((RS, NV), BF16)],
        compiler_params=_params(("arbitrary",), 48),
        name="ret_s_out",
    )(x, o, g, wout, gpost)


def _rope_tables(pos):
    half = RET_DK // 2
    inv = ROPE_BASE ** (-jnp.arange(half, dtype=F32) / half)
    ang = pos.astype(F32)[:, None] * inv[None, :]
    return jnp.cos(ang), jnp.sin(ang)


def _state_rows(st):
    lead, c = st.shape[:-3], st.shape[-1]
    n = len(lead)
    s = st.reshape(lead + (NS, 2, c // 128, 128))
    s = s.transpose(tuple(range(n)) + (n + 2, n + 1, n, n + 3))
    return s.reshape(lead + (c // 128 * 2, NS, 128))


def _from_state_rows(s):
    tiles = s.shape[0] // 2
    return s.reshape(tiles, 2, NS, 128).transpose(2, 1, 0, 3).reshape(NS, 2, tiles * 128)


def kernel(x_prompt, x_sample, mem_prompt, cache_mem_k, cache_mem_v, state_shortconv, state_retention, state_ffn_conv, norm_mix_pre, norm_mix_post, w_in_even, sgu_vnorm, sgu_w, sgu_b, conv_short, w_out_even, w_in_odd, w_out_odd, norm_x_pre, norm_x_post, norm_mem, w_xq, w_xk, w_xv, w_xo, norm_ffn_pre, norm_ffn_post, w_ffn_up, conv_ffn, w_ffn_down):
    row = lambda g: g.reshape(1, -1)

    xs = x_sample.reshape(RS, D)
    w4 = jnp.tril(sgu_w[0][:, :TS, :TS])
    gw = jnp.repeat(w4.transpose(1, 2, 0), CHUNK, axis=2)
    gb = jnp.repeat(sgu_b[0][:, :TS].T, CHUNK, axis=1)
    xs, sc_s, v_s = _even_s(xs, row(norm_mix_pre[0]), w_in_even, row(sgu_vnorm[0]), gw, gb, conv_short,
                            _state_rows(state_shortconv[0]), w_out_even, row(norm_mix_post[0]))
    cos_s, sin_s = _rope_tables(PAST + (jnp.arange(RS, dtype=jnp.int32) % TS))
    cache_k, cache_v = _head_view(cache_mem_k), _head_view(cache_mem_v)
    ffn_rows = _state_rows(state_ffn_conv)
    ffn_s_states, ffn_p_states, ffn_w = [], [], []

    def sample_tail(l, xs):
        xs = _xattn_s(l, xs, row(norm_x_pre[l]), w_xq, cache_k, cache_v, w_xo, row(norm_x_post[l]))
        xs, ng, nv, wg_b, wv_b, wd_b = _ffn_s(l, xs, row(norm_ffn_pre[l]), w_ffn_up, conv_ffn, ffn_rows, w_ffn_down,
                                              row(norm_ffn_post[l]))
        ffn_s_states.append(_from_state_rows(jnp.concatenate([ng, nv], axis=0)))
        ffn_w.append((wg_b, wv_b, wd_b))
        return xs

    def prompt_ffn(l, xp):
        xp, st = _ffn_p(l, xp, row(norm_ffn_pre[l]), ffn_w[l][0], ffn_w[l][1], conv_ffn, ffn_w[l][2],
                        row(norm_ffn_post[l]))
        ffn_p_states.append(st[:, 6:8, :])
        return xp

    xs = sample_tail(0, xs)
    q, kdt, v, g, oi, wq_o, wk_o, wv_o, wg_o = _ret_s_proj(xs, cos_s, sin_s, row(norm_mix_pre[1]), w_in_odd)

    mem_k, mem_v, mem_kb, mem_vb = _memkv(mem_prompt.reshape(NB * N_MEM, D), norm_mem.reshape(2, 1, D), w_xk, w_xv)
    xp = x_prompt.reshape(NB * SEQ, D)
    sb_full = jnp.repeat(sgu_b[0].T, CHUNK, axis=1)
    xp, sc_p = _even_p(xp, row(norm_mix_pre[0]), w_in_even, row(sgu_vnorm[0]), sgu_w, sb_full, conv_short, w_out_even,
                       row(norm_mix_post[0]))
    xp, o, ret_s_state = _xattn_p_ret_s(0, xp, row(norm_x_pre[0]), w_xq, mem_kb, mem_vb, w_xo, row(norm_x_post[0]),
                                        q, kdt, v, oi, state_retention[0])
    xs, wout_o = _ret_s_out(xs, o, g, w_out_odd, row(norm_mix_post[1]))
    xs = sample_tail(1, xs)

    xp = prompt_ffn(0, xp)
    cos_p, sin_p = _rope_tables(jnp.arange(SEQ, dtype=jnp.int32))
    xp, ret_p_state = _ret_p(xp, cos_p, sin_p, row(norm_mix_pre[1]), wq_o, wk_o, wv_o, wg_o, wout_o,
                             row(norm_mix_post[1]))
    xp = _xattn_p(1, xp, row(norm_x_pre[1]), w_xq, mem_kb, mem_vb, w_xo, row(norm_x_post[1]))
    xp = prompt_ffn(1, xp)

    return (xp.reshape(NB, SEQ, D), xs.reshape(NS, TS, D),
            _from_head_view(mem_k), _from_head_view(mem_v),
            sc_p[None, :, 6:8, :], _from_state_rows(sc_s)[None],
            v_s.reshape(1, NS, TS, SGU_W),
            ret_p_state[None], ret_s_state[None],
            jnp.stack(ffn_p_states), jnp.stack(ffn_s_states))
```

```python
import functools

import numpy as np
import jax
import jax.numpy as jnp
from jax import lax
from jax.experimental import pallas as pl
from jax.experimental.pallas import tpu as pltpu

F32 = jnp.float32
BF16 = jnp.bfloat16

D = 1024
SEQ = 2048
NB = 8
NS = 128
TS = 4
RS = NS * TS
PAST = 16384
CHUNK = 128
EPS = 1e-6
SGU_W = 512
SC_W = 512
RET_H = 4
RET_DK = 256
RET_DV = 512
NQ = RET_H * RET_DK
NV = RET_H * RET_DV
N_MEM = 256
XH = 4
XHD = 256
DFF = 2816
FC = 256
NFC = DFF // FC
ROPE_BASE = 10000.0

_LOG_G = [float(v) for v in np.log1p(-np.exp2(np.float32(-5.0) - np.arange(RET_H, dtype=np.float32))).astype(np.float32)]

_MIB = 1024 * 1024


def _rms(x, g):
    ms = jnp.mean(x * x, axis=-1, keepdims=True)
    return (x * lax.rsqrt(ms + EPS)) * g


_GELU_C = 0.7978845608028654
_LOG2E = 1.4426950408889634


def _gelu(x):
    k0 = -2.0 * _LOG2E * _GELU_C
    k1 = k0 * 0.044715
    return x * (1.0 / (1.0 + jnp.exp2(x * (k0 + k1 * (x * x)))))


def _dot(a, b):
    return jnp.dot(a, b, preferred_element_type=F32)


def _dot_nt(a, b):
    return lax.dot_general(a, b, (((1,), (1,)), ((), ())), preferred_element_type=F32)


def _const(shape):
    n = len(shape)
    return pl.BlockSpec(shape, lambda *_: (0,) * n, pipeline_mode=pl.Buffered(1))


def _layer_const(l, shape):
    n = len(shape)
    return pl.BlockSpec((None,) + tuple(shape), lambda *_: (l,) + (0,) * n, pipeline_mode=pl.Buffered(1))


def _full(shape):
    n = len(shape)
    return pl.BlockSpec(shape, lambda *_: (0,) * n)


def _params(sem, vmem_mib):
    return pltpu.CompilerParams(dimension_semantics=sem, vmem_limit_bytes=vmem_mib * _MIB)


def _conv3(p, p1, p2, w_ref, cols):
    return w_ref[0:1, cols] * p2 + w_ref[1:2, cols] * p1 + w_ref[2:3, cols] * p


def _shift_down(a, first_row):
    r = pltpu.roll(a, 1, 0)
    row8 = lax.broadcasted_iota(jnp.int32, (8, a.shape[1]), 0)
    top = jnp.where(row8 == 0, first_row, r[0:8])
    return jnp.concatenate([top, r[8:]], axis=0)


def _conv_tile(u, prev_ref, w_ref, cols):
    w0, w1, w2 = w_ref[0:1, cols], w_ref[1:2, cols], w_ref[2:3, cols]
    m1 = prev_ref[7:8, cols]
    m2 = prev_ref[6:7, cols]
    inner = _shift_down(w0 * u, w0 * m1) + w1 * u
    return _shift_down(inner, w0 * m2 + w1 * m1) + w2 * u


def _conv_rows(u, prev_ref, w_ref, cols, sh_ref):
    r, c = u.shape
    ys = []
    for j in range(c // 128):
        cj = slice(cols.start + j * 128, cols.start + (j + 1) * 128)
        uj = u[:, j * 128:(j + 1) * 128]
        sh_ref[j, 0:8, :] = prev_ref[:, cj]
        sh_ref[j, 8:r + 8, :] = uj
        u1 = sh_ref[j, 7:r + 7, :]
        u2 = sh_ref[j, 6:r + 6, :]
        ys.append((w_ref[0:1, cj] * u2 + w_ref[1:2, cj] * u1) + w_ref[2:3, cj] * uj)
    return jnp.concatenate(ys, axis=1)


def _shift_sample(p, e1, e2):
    t = lax.broadcasted_iota(jnp.int32, p.shape, 0) & (TS - 1)
    p1 = jnp.where(t >= 1, pltpu.roll(p, 1, 0), e1)
    p2 = jnp.where(t >= 2, pltpu.roll(p, 2, 0), e2)
    return p1, p2


def _memkv_body(mem_ref, g_ref, wk_ref, wv_ref, k_ref, v_ref, kb_ref, vb_ref, *, nb):
    mn = _rms(mem_ref[...], g_ref[...]).astype(BF16)
    for w_ref, o_ref, ob_ref in ((wk_ref, k_ref, kb_ref), (wv_ref, v_ref, vb_ref)):
        y = _dot(mn, w_ref[...].astype(BF16))
        ob_ref[...] = y.astype(BF16)
        for b in range(nb):
            for hd in range(XH):
                for j in range(2):
                    c0 = hd * XHD + j * 128
                    o_ref[b, pl.ds(j * XH + hd, N_MEM, stride=8), :] = y[b * N_MEM:(b + 1) * N_MEM, c0:c0 + 128]


def _memkv(mem, g, wk, wv):
    rows = mem.shape[0]
    nb = 2
    tm = nb * N_MEM
    o_spec = pl.BlockSpec((None, nb, N_MEM * 8, 128), lambda l, i: (l, i, 0, 0))
    ob_spec = pl.BlockSpec((None, tm, D), lambda l, i: (l, i, 0))
    w_spec = pl.BlockSpec((None, D, D), lambda l, i: (l, 0, 0))
    return pl.pallas_call(
        functools.partial(_memkv_body, nb=nb),
        grid=(2, rows // tm),
        in_specs=[pl.BlockSpec((tm, D), lambda l, i: (i, 0)),
                  pl.BlockSpec((None, 1, D), lambda l, i: (l, 0, 0)),
                  w_spec, w_spec],
        out_specs=[o_spec, o_spec, ob_spec, ob_spec],
        out_shape=[jax.ShapeDtypeStruct((2, NB, N_MEM * 8, 128), F32)] * 2
        + [jax.ShapeDtypeStruct((2, rows, D), BF16)] * 2,
        compiler_params=_params(("arbitrary", "arbitrary"), 40),
        name="memkv",
    )(mem, g, wk, wv)


def _from_head_view(c):
    s = c.shape
    return c.reshape(s[0], s[1], N_MEM, 2, XH, 128).transpose(0, 1, 2, 4, 3, 5).reshape(s[0], s[1], N_MEM, XH, XHD)


def _even_front(x, gpre_ref, win, vn_ref):
    h = _rms(x, gpre_ref[...]).astype(BF16)
    u = _gelu(_dot(h, win(0, 512)))
    v = _rms(_gelu(_dot(h, win(512, 1024))), vn_ref[...])
    bg = _dot(h, win(1024, 1536))
    p = _dot(h, win(1536, 2048)) * _dot(h, win(2048, 2560))
    return u, v, bg, p


def _even_p_body(x_ref, gpre_ref, winf_ref, vn_ref, sw_ref, sb_ref, cw_ref, woutf_ref, gpost_ref,
                 xo_ref, st_ref, win_ref, wout_ref, carry_ref, cat_ref, sh_ref, *, tm):
    @pl.when((pl.program_id(0) == 0) & (pl.program_id(1) == 0))
    def _():
        win_ref[...] = winf_ref[...].astype(BF16)
        wout_ref[...] = woutf_ref[...].astype(BF16)

    @pl.when(pl.program_id(1) == 0)
    def _():
        carry_ref[...] = jnp.zeros_like(carry_ref)

    x = x_ref[...]
    u, v, bg, p = _even_front(x, gpre_ref, lambda c0, c1: win_ref[:, c0:c1], vn_ref)
    vb = v.astype(BF16)
    ri = lax.broadcasted_iota(jnp.int32, (CHUNK, CHUNK), 0)
    ci = lax.broadcasted_iota(jnp.int32, (CHUNK, CHUNK), 1)
    for g in range(4):
        cols = slice(g * 128, (g + 1) * 128)
        w = jnp.where(ri >= ci, sw_ref[g], 0.0).astype(BF16)
        for c in range(tm // CHUNK):
            rows = slice(c * CHUNK, (c + 1) * CHUNK)
            mixed = _dot(w, vb[rows, cols]) + sb_ref[:, cols]
            cat_ref[rows, cols] = (u[rows, cols] * mixed).astype(BF16)
    cz = _conv_rows(p, carry_ref, cw_ref, slice(0, SC_W), sh_ref)
    cat_ref[:, 512:1024] = (bg * cz).astype(BF16)
    carry_ref[...] = p[tm - 8:tm, :]
    st_ref[...] = p[tm - 8:tm, :]
    out = _dot(cat_ref[...], wout_ref[...])
    xo_ref[...] = x + _rms(out, gpost_ref[...])


def _even_p_parts(tm):
    nt = SEQ // tm
    row_spec = pl.BlockSpec((tm, D), lambda b, t: (b * nt + t, 0))
    in_specs = [row_spec, _const((1, D)), _layer_const(0, (D, 2560)), _const((1, SGU_W)),
                _layer_const(0, (4, CHUNK, CHUNK)), _const((CHUNK, SGU_W)), _layer_const(0, (3, SC_W)),
                _layer_const(0, (D, D)), _const((1, D))]
    out_specs = [row_spec, pl.BlockSpec((None, 8, SC_W), lambda b, t: (b, 0, 0))]
    out_shape = [jax.ShapeDtypeStruct((NB * SEQ, D), F32), jax.ShapeDtypeStruct((NB, 8, SC_W), F32)]
    scratch = [pltpu.VMEM((D, 2560), BF16), pltpu.VMEM((D, D), BF16), pltpu.VMEM((8, SC_W), F32),
               pltpu.VMEM((tm, D), BF16), pltpu.VMEM((SC_W // 128, tm + 8, 128), F32)]
    return nt, in_specs, out_specs, out_shape, scratch


def _even_s_body(x_ref, gpre_ref, win_ref, vn_ref, gw_ref, gb_ref, cw_ref, st_ref, wout_ref, gpost_ref,
                 xo_ref, ns_ref, v_ref, pan_ref):
    xt = _load_time_major(x_ref, pan_ref)
    u, v, bg, p = _even_front(xt, gpre_ref, lambda c0, c1: win_ref[:, c0:c1].astype(BF16), vn_ref)
    _store_batch_major(v_ref, v, pan_ref)
    vt = [v[t * NS:(t + 1) * NS, :] for t in range(TS)]
    mixed = []
    for t in range(TS):
        m = gb_ref[t:t + 1, :] + gw_ref[t, 0:1, :] * vt[0]
        for s in range(1, t + 1):
            m = m + gw_ref[t, s:s + 1, :] * vt[s]
        mixed.append(m)
    a = u * jnp.concatenate(mixed, axis=0)
    prev = [jnp.concatenate([st_ref[j * 2 + r] for j in range(SC_W // 128)], axis=1) for r in range(2)]
    cz = _conv_time_major(p, prev, cw_ref)
    for j in range(SC_W // 128):
        for r in range(2):
            ns_ref[j * 2 + r] = p[(TS - 2 + r) * NS:(TS - 1 + r) * NS, j * 128:(j + 1) * 128]
    cat = jnp.concatenate([a, bg * cz], axis=1).astype(BF16)
    out = _dot(cat, wout_ref[...].astype(BF16))
    _store_batch_major(xo_ref, xt + _rms(out, gpost_ref[...]), pan_ref)


def _even_s(x, gpre, win, vn, gw, gb, cw, st, wout, gpost):
    nst = SC_W // 128 * 2
    return pl.pallas_call(
        _even_s_body,
        grid=(1,),
        in_specs=[_const((RS, D)), _const((1, D)), _layer_const(0, (D, 2560)), _const((1, SGU_W)),
                  _const((TS, TS, SGU_W)), _const((TS, SGU_W)), _layer_const(0, (3, SC_W)), _const((nst, NS, 128)),
                  _layer_const(0, (D, D)), _const((1, D))],
        out_specs=[_full((RS, D)), _full((nst, NS, 128)), _full((RS, SGU_W))],
        out_shape=[jax.ShapeDtypeStruct((RS, D), F32), jax.ShapeDtypeStruct((nst, NS, 128), F32),
                   jax.ShapeDtypeStruct((RS, SGU_W), F32)],
        scratch_shapes=[pltpu.VMEM((D // 128, RS, 128), F32)],
        compiler_params=_params(("arbitrary",), 48),
        name="even_s",
    )(x, gpre, win, vn, gw, gb, cw, st, wout, gpost)


def _softmax_rows(s):
    m = jnp.max(s, axis=-1, keepdims=True)
    e = jnp.exp(s - m)
    return e * (1.0 / jnp.sum(e, axis=-1, keepdims=True))


def _xattn_p_body(x_ref, gpre_ref, wqf_ref, k_ref, v_ref, wof_ref, gpost_ref, xo_ref, wq_ref, wo_ref, q_ref, s_ref,
                  p_ref, o_ref):
    @pl.when((pl.program_id(0) == 0) & (pl.program_id(1) == 0))
    def _():
        wq_ref[...] = wqf_ref[...].astype(BF16)
        wo_ref[...] = wof_ref[...].astype(BF16)

    h = _rms(x_ref[...], gpre_ref[...]).astype(BF16)
    q_ref[...] = _dot(h, wq_ref[...]).astype(BF16)
    for hd in range(XH):
        cols = slice(hd * XHD, (hd + 1) * XHD)
        s_ref[hd] = _dot_nt(q_ref[:, cols], k_ref[:, cols]) * (XHD ** -0.5)
    p_ref[...] = _softmax_rows(s_ref[...]).astype(BF16)
    for hd in range(XH):
        cols = slice(hd * XHD, (hd + 1) * XHD)
        o_ref[:, cols] = _dot(p_ref[hd], v_ref[:, cols]).astype(BF16)
    out = _dot(o_ref[...], wo_ref[...])
    xo_ref[...] = x_ref[...] + _rms(out, gpost_ref[...])


def _xattn_p_parts(l, tm):
    nt = SEQ // tm
    row_spec = pl.BlockSpec((tm, D), lambda b, t: (b * nt + t, 0))
    kv_spec = pl.BlockSpec((None, N_MEM, D), lambda b, t: (l, b, 0))
    in_specs = [row_spec, _const((1, D)), _layer_const(l, (D, D)), kv_spec, kv_spec, _layer_const(l, (D, D)),
                _const((1, D))]
    scratch = [pltpu.VMEM((D, D), BF16), pltpu.VMEM((D, D), BF16), pltpu.VMEM((tm, D), BF16),
               pltpu.VMEM((XH, tm, N_MEM), F32), pltpu.VMEM((XH, tm, N_MEM), BF16), pltpu.VMEM((tm, D), BF16)]
    return nt, row_spec, in_specs, scratch


def _xattn_p(l, x, gpre, wq, kb, vb, wo, gpost):
    nt, row_spec, in_specs, scratch = _xattn_p_parts(l, 512)
    return pl.pallas_call(
        _xattn_p_body,
        grid=(NB, nt),
        in_specs=in_specs,
        out_specs=row_spec,
        out_shape=jax.ShapeDtypeStruct((NB * SEQ, D), F32),
        scratch_shapes=scratch,
        compiler_params=_params(("arbitrary", "arbitrary"), 48),
        name="xattn_p",
    )(x, gpre, wq, kb, vb, wo, gpost)


def _head_view(c):
    s = c.shape
    return c.reshape(s[0], s[1], N_MEM, XH, 2, 128).transpose(0, 1, 2, 4, 3, 5).reshape(s[0], s[1], N_MEM * 8, 128)


def _head_rows(ref, b, hd):
    halves = [ref[b, pl.ds(j * XH + hd, N_MEM, stride=8), :] for j in range(2)]
    return jnp.concatenate(halves, axis=1).astype(BF16)


def _xattn_s_body(x_ref, gpre_ref, wq_ref, k_ref, v_ref, wo_ref, gpost_ref, xo_ref, q_ref, o_ref, s_ref, *, bb):
    _xattn_s_step(pl.program_id(0), pl.num_programs(0), x_ref, gpre_ref, wq_ref, k_ref, v_ref, wo_ref, gpost_ref,
                  xo_ref, q_ref, o_ref, s_ref, bb=bb)


def _xattn_s_step(step, nsteps, x_ref, gpre_ref, wq_ref, k_ref, v_ref, wo_ref, gpost_ref, xo_ref, q_ref, o_ref, s_ref,
                  *, bb):
    @pl.when(step == 0)
    def _():
        h = _rms(x_ref[...], gpre_ref[...]).astype(BF16)
        q_ref[...] = _dot(h, wq_ref[...].astype(BF16))

    first = (lax.broadcasted_iota(jnp.int32, (8, XHD), 0) >> 2) == 0
    groups = [(pi, hd, bi) for pi in range(bb // 2) for hd in range(XH) for bi in range(2)]
    row0 = [pl.multiple_of((step * (bb // 2) + pi) * 8, 8) for pi in range(bb // 2)]
    for gi, (pi, hd, bi) in enumerate(groups):
        q8 = q_ref[pl.ds(row0[pi], 8), hd * XHD:(hd + 1) * XHD].astype(BF16)
        kh = _head_rows(k_ref, pi * 2 + bi, hd)
        s_ref[gi * 8:(gi + 1) * 8, :] = _dot_nt(q8, kh) * (XHD ** -0.5)
    s_ref[...] = _softmax_rows(s_ref[...])
    for gi, (pi, hd, bi) in enumerate(groups):
        if bi == 1:
            continue
        pv = [_dot(s_ref[(gi + b) * 8:(gi + b + 1) * 8, :].astype(BF16), _head_rows(v_ref, pi * 2 + b, hd))
              for b in range(2)]
        o_ref[pl.ds(row0[pi], 8), hd * XHD:(hd + 1) * XHD] = jnp.where(first, pv[0], pv[1])

    @pl.when(step == nsteps - 1)
    def _():
        out = _dot(o_ref[...].astype(BF16), wo_ref[...].astype(BF16))
        xo_ref[...] = x_ref[...] + _rms(out, gpost_ref[...])


def _xattn_ps_body(x_ref, gpre_ref, wqf_ref, k_ref, v_ref, wof_ref, gpost_ref,
                   xs_ref, gpres_ref, ks_ref, vs_ref, gposts_ref,
                   xo_ref, xso_ref,
                   wq_ref, wo_ref, q_ref, s_ref, p_ref, o_ref, qs_ref, os_ref, ss_ref, *, bb):
    _xattn_p_body(x_ref, gpre_ref, wqf_ref, k_ref, v_ref, wof_ref, gpost_ref, xo_ref, wq_ref, wo_ref, q_ref, s_ref,
                  p_ref, o_ref)
    step = pl.program_id(0) * pl.num_programs(1) + pl.program_id(1)
    _xattn_s_step(step, pl.num_programs(0) * pl.num_programs(1), xs_ref, gpres_ref, wq_ref, ks_ref, vs_ref, wo_ref,
                  gposts_ref, xso_ref, qs_ref, os_ref, ss_ref, bb=bb)


def _xattn_ps(l, x, gpre, wq, kb, vb, wo, gpost, xs, gpres, ks, vs, gposts):
    bb = 2
    steps = NS // bb
    tm = NB * SEQ // steps
    nt, row_spec, h_in, h_scratch = _xattn_p_parts(l, tm)
    kv_spec = pl.BlockSpec((None, bb, N_MEM * 8, 128), lambda b, t: (l, b * nt + t, 0, 0))
    return pl.pallas_call(
        functools.partial(_xattn_ps_body, bb=bb),
        grid=(NB, nt),
        in_specs=h_in + [_const((RS, D)), _const((1, D)), kv_spec, kv_spec, _const((1, D))],
        out_specs=[row_spec, _full((RS, D))],
        out_shape=[jax.ShapeDtypeStruct((NB * SEQ, D), F32), jax.ShapeDtypeStruct((RS, D), F32)],
        scratch_shapes=h_scratch + [pltpu.VMEM((RS, D), F32), pltpu.VMEM((RS, D), F32),
                                    pltpu.VMEM((bb * XH * 8, N_MEM), F32)],
        compiler_params=_params(("arbitrary", "arbitrary"), 56),
        name="xattn_ps",
    )(x, gpre, wq, kb, vb, wo, gpost, xs, gpres, ks, vs, gposts)


def _xattn_s(l, x, gpre, wq, k, v, wo, gpost):
    bb = 4
    kv_spec = pl.BlockSpec((None, bb, N_MEM * 8, 128), lambda i: (l, i, 0, 0))
    return pl.pallas_call(
        functools.partial(_xattn_s_body, bb=bb),
        grid=(NS // bb,),
        in_specs=[_const((RS, D)), _const((1, D)), _layer_const(l, (D, D)), kv_spec, kv_spec, _layer_const(l, (D, D)),
                  _const((1, D))],
        out_specs=_full((RS, D)),
        out_shape=jax.ShapeDtypeStruct((RS, D), F32),
        scratch_shapes=[pltpu.VMEM((RS, D), F32), pltpu.VMEM((RS, D), F32), pltpu.VMEM((bb * XH * 8, N_MEM), F32)],
        compiler_params=_params(("arbitrary",), 48),
        name="xattn_s",
    )(x, gpre, wq, k, v, wo, gpost)


_DOWN_SPLITS = (0, NFC)


def _ffn_p_body(x_ref, gpre_ref, wg_ref, wv_ref, cw_ref, wdn_ref, gpost_ref, xo_ref, st_ref, carry_ref, h_ref, act_ref,
                *sh_refs, tm):
    @pl.when(pl.program_id(1) == 0)
    def _():
        carry_ref[...] = jnp.zeros_like(carry_ref)

    h_ref[...] = _rms(x_ref[...], gpre_ref[...]).astype(BF16)
    for c in range(NFC):
        halves = []
        for i, (base, w_ref) in enumerate(((0, wg_ref), (DFF, wv_ref))):
            cols = slice(base + c * FC, base + (c + 1) * FC)
            up = _dot(h_ref[...], w_ref[:, c * FC:(c + 1) * FC])
            halves.append(_conv_rows(up, carry_ref, cw_ref, cols, sh_refs[(2 * c + i) % len(sh_refs)]))
            carry_ref[:, cols] = up[tm - 8:tm, :]
            st_ref[:, cols] = up[tm - 8:tm, :]
        act_ref[:, c * FC:(c + 1) * FC] = (_gelu(halves[0]) * halves[1]).astype(BF16)
    out = _dot(act_ref[...], wdn_ref[...])
    xo_ref[...] = x_ref[...] + _rms(out, gpost_ref[...])


def _ffn_p_parts(l, tm):
    nt = SEQ // tm
    row_spec = pl.BlockSpec((tm, D), lambda b, t: (b * nt + t, 0))
    half = lambda i: pl.BlockSpec((D, DFF), lambda *_: (0, i), pipeline_mode=pl.Buffered(1))
    in_specs = [row_spec, _const((1, D)), half(0), half(1), _layer_const(l, (3, 2 * DFF)),
                _const((DFF, D)), _const((1, D))]
    out_specs = [row_spec, pl.BlockSpec((None, 8, 2 * DFF), lambda b, t: (b, 0, 0))]
    out_shape = [jax.ShapeDtypeStruct((NB * SEQ, D), F32), jax.ShapeDtypeStruct((NB, 8, 2 * DFF), F32)]
    scratch = ([pltpu.VMEM((8, 2 * DFF), F32), pltpu.VMEM((tm, D), BF16), pltpu.VMEM((tm, DFF), BF16)]
               + [pltpu.VMEM((FC // 128, tm + 8, 128), F32)] * 4)
    return nt, in_specs, out_specs, out_shape, scratch


def _ffn_p(l, x, gpre, wg, wv, cw, wdn, gpost):
    tm = 512
    nt, in_specs, out_specs, out_shape, scratch = _ffn_p_parts(l, tm)
    return pl.pallas_call(
        functools.partial(_ffn_p_body, tm=tm),
        grid=(NB, nt),
        in_specs=in_specs,
        out_specs=out_specs,
        out_shape=out_shape,
        scratch_shapes=scratch,
        compiler_params=_params(("arbitrary", "arbitrary"), 52),
        name="ffn_p",
    )(x, gpre, wg, wv, cw, wdn, gpost)


def _load_time_major(x_ref, pan_ref):
    n = x_ref.shape[1] // 128
    for p in range(n):
        pan_ref[p] = x_ref[:, p * 128:(p + 1) * 128]
    return jnp.concatenate(
        [jnp.concatenate([pan_ref[p, pl.ds(t, NS, stride=TS), :] for p in range(n)], axis=1) for t in range(TS)],
        axis=0)


def _store_batch_major(o_ref, y, pan_ref):
    n = o_ref.shape[1] // 128
    for p in range(n):
        for t in range(TS):
            pan_ref[p, pl.ds(t, NS, stride=TS), :] = y[t * NS:(t + 1) * NS, p * 128:(p + 1) * 128]
    for p in range(n):
        o_ref[:, p * 128:(p + 1) * 128] = pan_ref[p]


def _conv_time_major(u, prev, w_ref):
    blocks = list(prev) + [u[t * NS:(t + 1) * NS, :] for t in range(TS)]
    w0, w1, w2 = w_ref[0:1, :], w_ref[1:2, :], w_ref[2:3, :]
    return jnp.concatenate([(w0 * blocks[t] + w1 * blocks[t + 1]) + w2 * blocks[t + 2] for t in range(TS)], axis=0)


def _ffn_s_body(x_ref, gpre_ref, w_ref, cw_ref, st_ref, wdn_ref, gpost_ref,
                xo_ref, ns_ref, wb_ref, wdb_ref, xt_ref, h_ref, acc_ref, pan_ref, yg_ref):
    c, half = pl.program_id(0), pl.program_id(1)

    @pl.when((c == 0) & (half == 0))
    def _():
        xt = _load_time_major(x_ref, pan_ref)
        xt_ref[...] = xt
        h_ref[...] = _rms(xt, gpre_ref[...]).astype(BF16)
        acc_ref[...] = jnp.zeros_like(acc_ref)

    wb_ref[...] = w_ref[...].astype(BF16)
    u = _dot(h_ref[...], wb_ref[...])
    y = _conv_time_major(u, [st_ref[:, r, :] for r in range(2)], cw_ref)
    for r in range(2):
        ns_ref[:, r, :] = u[(TS - 2 + r) * NS:(TS - 1 + r) * NS, :]

    @pl.when(half == 0)
    def _():
        yg_ref[...] = y

    @pl.when(half == 1)
    def _():
        wdb_ref[...] = wdn_ref[...].astype(BF16)
        act = (_gelu(yg_ref[...]) * y).astype(BF16)
        acc_ref[...] += _dot(act, wdb_ref[...])

    @pl.when((c == pl.num_programs(0) - 1) & (half == 1))
    def _():
        _store_batch_major(xo_ref, xt_ref[...] + _rms(acc_ref[...], gpost_ref[...]), pan_ref)


def _ffn_s(l, x, gpre, wup, cw, st, wdn, gpost):
    col = lambda c, h: h * NFC + c
    return pl.pallas_call(
        _ffn_s_body,
        grid=(NFC, 2),
        in_specs=[_const((RS, D)), _const((1, D)),
                  pl.BlockSpec((None, D, FC), lambda c, h: (l, 0, col(c, h))),
                  pl.BlockSpec((None, 3, FC), lambda c, h: (l, 0, col(c, h))),
                  pl.BlockSpec((None, NS, 2, FC), lambda c, h: (l, 0, 0, col(c, h))),
                  pl.BlockSpec((None, FC, D), lambda c, h: (l, c, 0)), _const((1, D))],
        out_specs=[_full((RS, D)), pl.BlockSpec((NS, 2, FC), lambda c, h: (0, 0, col(c, h))),
                   pl.BlockSpec((D, FC), lambda c, h: (0, col(c, h))), pl.BlockSpec((FC, D), lambda c, h: (c, 0))],
        out_shape=[jax.ShapeDtypeStruct((RS, D), F32), jax.ShapeDtypeStruct((NS, 2, 2 * DFF), F32),
                   jax.ShapeDtypeStruct((D, 2 * DFF), BF16), jax.ShapeDtypeStruct((DFF, D), BF16)],
        scratch_shapes=[pltpu.VMEM((RS, D), F32), pltpu.VMEM((RS, D), BF16), pltpu.VMEM((RS, D), F32),
                        pltpu.VMEM((D // 128, RS, 128), F32), pltpu.VMEM((RS, FC), F32)],
        compiler_params=_params(("arbitrary", "arbitrary"), 40),
        name="ffn_s",
    )(x, gpre, wup, cw, st, wdn, gpost)


def _rope_head(z, cos, sin, base):
    x1 = z[:, base:base + 128]
    x2 = z[:, base + 128:base + 256]
    return x1 * cos - x2 * sin, x1 * sin + x2 * cos


def _group_norm_gate(o, g):
    mu = jnp.mean(o, axis=-1, keepdims=True)
    d = o - mu
    var = jnp.mean(d * d, axis=-1, keepdims=True)
    return (g * jax.nn.sigmoid(g)) * (d * lax.rsqrt(var + EPS))


RCH = 256


def _ret_p_body(x_ref, cos_ref, sin_ref, gpre_ref, wq_ref, wk_ref, wv_ref, wg_ref, wout_ref, gpost_ref, xo_ref, s_ref,
                h_ref, q_ref, k_ref, kd_ref, v_ref, g_ref, y_ref, dm_ref, qd_ref, kdec_ref, *, tm):
    first = (pl.program_id(0) == 0) & (pl.program_id(1) == 0)

    @pl.when(first)
    def _():
        i = lax.broadcasted_iota(jnp.int32, (RCH, RCH), 0).astype(F32)
        j = lax.broadcasted_iota(jnp.int32, (RCH, RCH), 1).astype(F32)
        i1 = lax.broadcasted_iota(jnp.int32, (RCH, 128), 0).astype(F32)
        for hd in range(RET_H):
            lg = _LOG_G[hd]
            dm_ref[hd] = jnp.where(i >= j, jnp.exp(lg * jnp.maximum(i - j, 0.0)), 0.0)
            qd_ref[hd] = jnp.exp(lg * (i1 + 1.0))
            kdec_ref[hd] = jnp.exp(lg * (RCH - 1.0 - i1))

    @pl.when(pl.program_id(1) == 0)
    def _():
        s_ref[...] = jnp.zeros_like(s_ref)

    h_ref[...] = _rms(x_ref[...], gpre_ref[...]).astype(BF16)
    cos = cos_ref[...]
    sin = sin_ref[...]
    nchunk = tm // RCH
    for hd in range(RET_H):
        base = hd * RET_DK
        zq = _dot(h_ref[...], wq_ref[:, base:base + RET_DK])
        zk = _dot(h_ref[...], wk_ref[:, base:base + RET_DK])
        q1, q2 = _rope_head(zq, cos, sin, 0)
        k1, k2 = _rope_head(zk, cos, sin, 0)
        q_ref[:, base:base + 128] = (q1 * (RET_DK ** -0.5)).astype(BF16)
        q_ref[:, base + 128:base + 256] = (q2 * (RET_DK ** -0.5)).astype(BF16)
        k_ref[:, base:base + 128] = k1.astype(BF16)
        k_ref[:, base + 128:base + 256] = k2.astype(BF16)
        kdec = jnp.concatenate([kdec_ref[hd]] * nchunk, axis=0)
        kd_ref[:, base:base + 128] = k1 * kdec
        kd_ref[:, base + 128:base + 256] = k2 * kdec
        vcols = slice(hd * RET_DV, (hd + 1) * RET_DV)
        v_ref[:, vcols] = _dot(h_ref[...], wv_ref[:, vcols]).astype(BF16)
        g_ref[:, vcols] = _dot(h_ref[...], wg_ref[:, vcols])
    for hd in range(RET_H):
        kcols = slice(hd * RET_DK, (hd + 1) * RET_DK)
        vcols = slice(hd * RET_DV, (hd + 1) * RET_DV)
        decay_l = float(np.exp(np.float32(_LOG_G[hd]) * np.float32(RCH)))
        qd = qd_ref[hd]
        qd4 = jnp.concatenate([qd, qd, qd, qd], axis=1)
        for c in range(nchunk):
            rows = slice(c * RCH, (c + 1) * RCH)
            qc = q_ref[rows, kcols]
            vc = v_ref[rows, vcols]
            s_prev = s_ref[hd]
            sc = (_dot_nt(qc, k_ref[rows, kcols]) * dm_ref[hd]).astype(BF16)
            o = _dot(sc, vc) + _dot(qc, s_prev.astype(BF16)) * qd4
            kdt = kd_ref[rows, kcols].T.astype(BF16)
            s_ref[hd] = decay_l * s_prev + _dot(kdt, vc)
            y_ref[rows, vcols] = _group_norm_gate(o, g_ref[rows, vcols]).astype(BF16)
    out = _dot(y_ref[...], wout_ref[...])
    xo_ref[...] = x_ref[...] + _rms(out, gpost_ref[...])


def _ret_p(x, cos, sin, gpre, wq, wk, wv, wg, wout, gpost):
    tm = 512
    nt = SEQ // tm
    row_spec = pl.BlockSpec((tm, D), lambda b, t: (b * nt + t, 0))
    cs_spec = pl.BlockSpec((tm, 128), lambda b, t: (t, 0))
    return pl.pallas_call(
        functools.partial(_ret_p_body, tm=tm),
        grid=(NB, nt),
        in_specs=[row_spec, cs_spec, cs_spec, _const((1, D)), _const((D, NQ)), _const((D, NQ)), _const((D, NV)),
                  _const((D, NV)), _const((NV, D)), _const((1, D))],
        out_specs=[row_spec, pl.BlockSpec((None, RET_H, RET_DK, RET_DV), lambda b, t: (b, 0, 0, 0))],
        out_shape=[jax.ShapeDtypeStruct((NB * SEQ, D), F32), jax.ShapeDtypeStruct((NB, RET_H, RET_DK, RET_DV), F32)],
        scratch_shapes=[pltpu.VMEM((tm, D), BF16), pltpu.VMEM((tm, NQ), BF16), pltpu.VMEM((tm, NQ), BF16),
                        pltpu.VMEM((tm, NQ), F32), pltpu.VMEM((tm, NV), BF16), pltpu.VMEM((tm, NV), F32),
                        pltpu.VMEM((tm, NV), BF16), pltpu.VMEM((RET_H, RCH, RCH), F32),
                        pltpu.VMEM((RET_H, RCH, 128), F32), pltpu.VMEM((RET_H, RCH, 128), F32)],
        compiler_params=_params(("arbitrary", "arbitrary"), 56),
        name="ret_p",
    )(x, cos, sin, gpre, wq, wk, wv, wg, wout, gpost)


def _ret_s_proj_body(x_ref, cos_ref, sin_ref, gpre_ref, wqf_ref, wkf_ref, wvf_ref, wgf_ref,
                     q_ref, kdt_ref, v_ref, g_ref, oi_ref, wq_ref, wk_ref, wv_ref, wg_ref, h_ref):
    hd = pl.program_id(0)

    @pl.when(hd == 0)
    def _():
        h_ref[...] = _rms(x_ref[...], gpre_ref[...]).astype(BF16)

    for wf_ref, wb_ref in ((wqf_ref, wq_ref), (wkf_ref, wk_ref), (wvf_ref, wv_ref), (wgf_ref, wg_ref)):
        wb_ref[...] = wf_ref[...].astype(BF16)
    lg = jnp.where(hd == 0, _LOG_G[0], jnp.where(hd == 1, _LOG_G[1], jnp.where(hd == 2, _LOG_G[2], _LOG_G[3])))
    h = h_ref[...]
    cos = cos_ref[...]
    sin = sin_ref[...]
    ti = (lax.broadcasted_iota(jnp.int32, (RS, 128), 0) & (TS - 1)).astype(F32)
    i = lax.broadcasted_iota(jnp.int32, (CHUNK, CHUNK), 0)
    j = lax.broadcasted_iota(jnp.int32, (CHUNK, CHUNK), 1)
    same = ((i >> 2) == (j >> 2)) & (i >= j)
    dm = jnp.where(same, jnp.exp(lg * jnp.maximum(i - j, 0).astype(F32)), 0.0)
    q1, q2 = _rope_head(_dot(h, wq_ref[...]), cos, sin, 0)
    k1, k2 = _rope_head(_dot(h, wk_ref[...]), cos, sin, 0)
    q = jnp.concatenate([q1, q2], axis=1) * (RET_DK ** -0.5)
    k = jnp.concatenate([k1, k2], axis=1)
    q_ref[...] = q
    kdec = jnp.exp(lg * (TS - 1.0 - ti))
    kd = k * jnp.concatenate([kdec, kdec], axis=1)
    vf = _dot(h, wv_ref[...])
    v_ref[...] = vf
    v = vf.astype(BF16)
    g_ref[...] = _dot(h, wg_ref[...])
    qb = q.astype(BF16)
    kb = k.astype(BF16)
    for tl in range(RS // CHUNK):
        rows = slice(tl * CHUNK, (tl + 1) * CHUNK)
        sc = (_dot_nt(qb[rows], kb[rows]) * dm).astype(BF16)
        oi_ref[rows, :] = _dot(sc, v[rows])
        kdt_ref[tl] = kd[rows].T.astype(BF16)


def _ret_s_proj(x, cos, sin, gpre, win):
    def wcol(width, first_block):
        return pl.BlockSpec((None, D, width), lambda h: (0, 0, first_block + h))
    def hcol(rows, width):
        return pl.BlockSpec((rows, width), lambda h: (0, h))
    return pl.pallas_call(
        _ret_s_proj_body,
        grid=(RET_H,),
        in_specs=[_const((RS, D)), _const((RS, 128)), _const((RS, 128)), _const((1, D)),
                  wcol(RET_DK, 0), wcol(RET_DK, NQ // RET_DK), wcol(RET_DV, 2 * NQ // RET_DV),
                  wcol(RET_DV, (2 * NQ + NV) // RET_DV)],
        out_specs=[hcol(RS, RET_DK),
                   pl.BlockSpec((RS // CHUNK, None, RET_DK, CHUNK), lambda h: (0, h, 0, 0)),
                   hcol(RS, RET_DV), hcol(RS, RET_DV), hcol(RS, RET_DV),
                   hcol(D, RET_DK), hcol(D, RET_DK), hcol(D, RET_DV), hcol(D, RET_DV)],
        out_shape=[jax.ShapeDtypeStruct((RS, NQ), F32),
                   jax.ShapeDtypeStruct((RS // CHUNK, RET_H, RET_DK, CHUNK), BF16),
                   jax.ShapeDtypeStruct((RS, NV), F32), jax.ShapeDtypeStruct((RS, NV), F32),
                   jax.ShapeDtypeStruct((RS, NV), F32),
                   jax.ShapeDtypeStruct((D, NQ), BF16), jax.ShapeDtypeStruct((D, NQ), BF16),
                   jax.ShapeDtypeStruct((D, NV), BF16), jax.ShapeDtypeStruct((D, NV), BF16)],
        scratch_shapes=[pltpu.VMEM((RS, D), BF16)],
        compiler_params=_params(("arbitrary",), 40),
        name="ret_s_proj",
    )(x, cos, sin, gpre, win, win, win, win)


def _ret_s_core_body(step, q_ref, kdt_ref, v_ref, oi_ref, s_ref, o_ref, so_ref):
    pair_in_tile = step % (CHUNK // 8)
    row8 = lax.broadcasted_iota(jnp.int32, (8, RET_DK), 0)
    row128 = lax.broadcasted_iota(jnp.int32, (CHUNK, RET_DV), 0)
    t8 = (lax.broadcasted_iota(jnp.int32, (8, RET_DV), 0) & (TS - 1)).astype(F32)
    for hd in range(RET_H):
        lg = _LOG_G[hd]
        decay_l = float(np.exp(np.float32(lg) * np.float32(TS)))
        q8 = q_ref[:, hd * RET_DK:(hd + 1) * RET_DK]
        v128 = v_ref[:, hd * RET_DV:(hd + 1) * RET_DV]
        kdt = kdt_ref[hd]
        inter = jnp.zeros((8, RET_DV), F32)
        for bi in range(2):
            s_prev = s_ref[bi, hd]
            qm = jnp.where((row8 >> 2) == bi, q8, 0.0).astype(BF16)
            inter = inter + _dot(qm, s_prev.astype(BF16))
            vm = jnp.where((row128 >> 2) == pair_in_tile * 2 + bi, v128, 0.0).astype(BF16)
            so_ref[bi, hd] = decay_l * s_prev + _dot(kdt, vm)
        cols = slice(hd * RET_DV, (hd + 1) * RET_DV)
        o_ref[:, cols] = oi_ref[:, cols] + inter * jnp.exp(lg * (t8 + 1.0))


def _ret_s_core_parts():
    ppt = CHUNK // 8
    s_spec = pl.BlockSpec((2, RET_H, RET_DK, RET_DV), lambda i: (i, 0, 0, 0))
    in_specs = [pl.BlockSpec((8, NQ), lambda i: (i, 0)),
                pl.BlockSpec((None, RET_H, RET_DK, CHUNK), lambda i: (i // ppt, 0, 0, 0)),
                pl.BlockSpec((CHUNK, NV), lambda i: (i // ppt, 0)),
                pl.BlockSpec((8, NV), lambda i: (i, 0)),
                s_spec]
    out_specs = [pl.BlockSpec((8, NV), lambda i: (i, 0)), s_spec]
    out_shape = [jax.ShapeDtypeStruct((RS, NV), F32), jax.ShapeDtypeStruct((NS, RET_H, RET_DK, RET_DV), F32)]
    return in_specs, out_specs, out_shape


def _host_ret_s_body(*refs, host_body, n_in, n_out):
    host_in, guest_in = refs[:n_in], refs[n_in:n_in + 5]
    k = n_in + 5
    host_out, guest_out, scratch = refs[k:k + n_out], refs[k + n_out:k + n_out + 2], refs[k + n_out + 2:]
    host_body(*host_in, *host_out, *scratch)
    step = pl.program_id(0) * pl.num_programs(1) + pl.program_id(1)
    _ret_s_core_body(step, *guest_in, *guest_out)


def _with_ret_s(name, host_body, nt, h_in, h_out, h_shape, scratch, host_args, guest_args):
    g_in, g_out, g_shape = _ret_s_core_parts()

    def on_grid(spec):
        return pl.BlockSpec(spec.block_shape, lambda b, t, m=spec.index_map: m(b * nt + t))

    return pl.pallas_call(
        functools.partial(_host_ret_s_body, host_body=host_body, n_in=len(h_in), n_out=len(h_out)),
        grid=(NB, nt),
        in_specs=h_in + [on_grid(sp) for sp in g_in],
        out_specs=h_out + [on_grid(sp) for sp in g_out],
        out_shape=h_shape + g_shape,
        scratch_shapes=scratch,
        compiler_params=_params(("arbitrary", "arbitrary"), 56),
        name=name,
    )(*host_args, *guest_args)


def _ret_s_out_body(x_ref, o_ref, g_ref, woutf_ref, gpost_ref, xo_ref, wout_ref, y_ref):
    wout_ref[...] = woutf_ref[...].astype(BF16)
    for hd in range(RET_H):
        cols = slice(hd * RET_DV, (hd + 1) * RET_DV)
        y_ref[:, cols] = _group_norm_gate(o_ref[:, cols], g_ref[:, cols]).astype(BF16)
    out = _dot(y_ref[...], wout_ref[...])
    xo_ref[...] = x_ref[...] + _rms(out, gpost_ref[...])


def _ret_s_out(x, o, g, wout, gpost):
    return pl.pallas_call(
        _ret_s_out_body,
        grid=(1,),
        in_specs=[_const((RS, D)), _const((RS, NV)), _const((RS, NV)), _layer_const(0, (NV, D)), _const((1, D))],
        out_specs=[_full((RS, D)), _full((NV, D))],
        out_shape=[jax.ShapeDtypeStruct((RS, D), F32), jax.ShapeDtypeStruct((NV, D), BF16)],
        scratch_shapes=[pltpu.VMEM((RS, NV), BF16)],
        compiler_params=_params(("arbitrary",), 48),
        name="ret_s_out",
    )(x, o, g, wout, gpost)


def _rope_tables(pos):
    half = RET_DK // 2
    inv = ROPE_BASE ** (-jnp.arange(half, dtype=F32) / half)
    ang = pos.astype(F32)[:, None] * inv[None, :]
    return jnp.cos(ang), jnp.sin(ang)


def _state_rows(st):
    lead, c = st.shape[:-3], st.shape[-1]
    n = len(lead)
    s = st.reshape(lead + (NS, 2, c // 128, 128))
    s = s.transpose(tuple(range(n)) + (n + 2, n + 1, n, n + 3))
    return s.reshape(lead + (c // 128 * 2, NS, 128))


def _from_state_rows(s):
    tiles = s.shape[0] // 2
    return s.reshape(tiles, 2, NS, 128).transpose(2, 1, 0, 3).reshape(NS, 2, tiles * 128)


def kernel(x_prompt, x_sample, mem_prompt, cache_mem_k, cache_mem_v, state_shortconv, state_retention, state_ffn_conv, norm_mix_pre, norm_mix_post, w_in_even, sgu_vnorm, sgu_w, sgu_b, conv_short, w_out_even, w_in_odd, w_out_odd, norm_x_pre, norm_x_post, norm_mem, w_xq, w_xk, w_xv, w_xo, norm_ffn_pre, norm_ffn_post, w_ffn_up, conv_ffn, w_ffn_down):
    row = lambda g: g.reshape(1, -1)

    xs = x_sample.reshape(RS, D)
    w4 = jnp.tril(sgu_w[0][:, :TS, :TS])
    gw = jnp.repeat(w4.transpose(1, 2, 0), CHUNK, axis=2)
    gb = jnp.repeat(sgu_b[0][:, :TS].T, CHUNK, axis=1)
    xs, sc_s, v_s = _even_s(xs, row(norm_mix_pre[0]), w_in_even, row(sgu_vnorm[0]), gw, gb, conv_short,
                            _state_rows(state_shortconv[0]), w_out_even, row(norm_mix_post[0]))
    cos_s, sin_s = _rope_tables(PAST + (jnp.arange(RS, dtype=jnp.int32) % TS))
    cache_k, cache_v = _head_view(cache_mem_k), _head_view(cache_mem_v)
    ffn_s_states, ffn_p_states, ffn_w = [], [], []

    def sample_ffn(l, xs):
        xs, ns, wup_b, wd_b = _ffn_s(l, xs, row(norm_ffn_pre[l]), w_ffn_up, conv_ffn, state_ffn_conv, w_ffn_down,
                                     row(norm_ffn_post[l]))
        ffn_s_states.append(ns)
        ffn_w.append((wup_b, wup_b, wd_b))
        return xs

    def prompt_ffn(l, xp):
        xp, st = _ffn_p(l, xp, row(norm_ffn_pre[l]), ffn_w[l][0], ffn_w[l][1], conv_ffn, ffn_w[l][2],
                        row(norm_ffn_post[l]))
        ffn_p_states.append(st[:, 6:8, :])
        return xp

    xs = _xattn_s(0, xs, row(norm_x_pre[0]), w_xq, cache_k, cache_v, w_xo, row(norm_x_post[0]))
    xs = sample_ffn(0, xs)
    q, kdt, v, g, oi, wq_o, wk_o, wv_o, wg_o = _ret_s_proj(xs, cos_s, sin_s, row(norm_mix_pre[1]), w_in_odd)

    mem_k, mem_v, mem_kb, mem_vb = _memkv(mem_prompt.reshape(NB * N_MEM, D), norm_mem.reshape(2, 1, D), w_xk, w_xv)
    xp = x_prompt.reshape(NB * SEQ, D)
    sb_full = jnp.repeat(sgu_b[0].T, CHUNK, axis=1)
    tm = NB * SEQ // (NS // 2)
    nt, e_in, e_out, e_shape, e_scratch = _even_p_parts(tm)
    xp, sc_p, o, ret_s_state = _with_ret_s(
        "even_p_ret_s", functools.partial(_even_p_body, tm=tm), nt, e_in, e_out, e_shape, e_scratch,
        (xp, row(norm_mix_pre[0]), w_in_even, row(sgu_vnorm[0]), sgu_w, sb_full, conv_short, w_out_even,
         row(norm_mix_post[0])), (q, kdt, v, oi, state_retention[0]))
    xs, wout_o = _ret_s_out(xs, o, g, w_out_odd, row(norm_mix_post[1]))

    xp = _xattn_p(0, xp, row(norm_x_pre[0]), w_xq, mem_kb, mem_vb, w_xo, row(norm_x_post[0]))
    xp = prompt_ffn(0, xp)
    cos_p, sin_p = _rope_tables(jnp.arange(SEQ, dtype=jnp.int32))
    xp, ret_p_state = _ret_p(xp, cos_p, sin_p, row(norm_mix_pre[1]), wq_o, wk_o, wv_o, wg_o, wout_o,
                             row(norm_mix_post[1]))
    xp, xs = _xattn_ps(1, xp, row(norm_x_pre[1]), w_xq, mem_kb, mem_vb, w_xo, row(norm_x_post[1]),
                       xs, row(norm_x_pre[1]), cache_k, cache_v, row(norm_x_post[1]))
    xs = sample_ffn(1, xs)
    xp = prompt_ffn(1, xp)

    return (xp.reshape(NB, SEQ, D), xs.reshape(NS, TS, D),
            _from_head_view(mem_k), _from_head_view(mem_v),
            sc_p[None, :, 6:8, :], _from_state_rows(sc_s)[None],
            v_s.reshape(1, NS, TS, SGU_W),
            ret_p_state[None], ret_s_state[None],
            jnp.stack(ffn_p_states), jnp.stack(ffn_s_states))
```

```python
import functools

import numpy as np
import jax
import jax.numpy as jnp
from jax import lax
from jax.experimental import pallas as pl
from jax.experimental.pallas import tpu as pltpu

F32 = jnp.float32
BF16 = jnp.bfloat16

D = 1024
SEQ = 2048
NB = 8
NS = 128
TS = 4
RS = NS * TS
PAST = 16384
CHUNK = 128
EPS = 1e-6
SGU_W = 512
SC_W = 512
RET_H = 4
RET_DK = 256
RET_DV = 512
NQ = RET_H * RET_DK
NV = RET_H * RET_DV
N_MEM = 256
XH = 4
XHD = 256
DFF = 2816
FC = 256
NFC = DFF // FC
ROPE_BASE = 10000.0

_LOG_G = [float(v) for v in np.log1p(-np.exp2(np.float32(-5.0) - np.arange(RET_H, dtype=np.float32))).astype(np.float32)]

_MIB = 1024 * 1024


def _rms(x, g):
    ms = jnp.mean(x * x, axis=-1, keepdims=True)
    return (x * lax.rsqrt(ms + EPS)) * g


_GELU_C = 0.7978845608028654
_LOG2E = 1.4426950408889634


def _gelu(x):
    k0 = -2.0 * _LOG2E * _GELU_C
    k1 = k0 * 0.044715
    return x * (1.0 / (1.0 + jnp.exp2(x * (k0 + k1 * (x * x)))))


def _dot(a, b):
    return jnp.dot(a, b, preferred_element_type=F32)


def _dot_nt(a, b):
    return lax.dot_general(a, b, (((1,), (1,)), ((), ())), preferred_element_type=F32)


def _const(shape):
    n = len(shape)
    return pl.BlockSpec(shape, lambda *_: (0,) * n, pipeline_mode=pl.Buffered(1))


def _layer_const(l, shape):
    n = len(shape)
    return pl.BlockSpec((None,) + tuple(shape), lambda *_: (l,) + (0,) * n, pipeline_mode=pl.Buffered(1))


def _full(shape):
    n = len(shape)
    return pl.BlockSpec(shape, lambda *_: (0,) * n)


def _params(sem, vmem_mib):
    return pltpu.CompilerParams(dimension_semantics=sem, vmem_limit_bytes=vmem_mib * _MIB)


def _conv_rows(u, prev_ref, w_ref, cols, sh_ref):
    r, c = u.shape
    ys = []
    for j in range(c // 128):
        cj = slice(cols.start + j * 128, cols.start + (j + 1) * 128)
        uj = u[:, j * 128:(j + 1) * 128]
        sh_ref[j, 0:8, :] = prev_ref[:, cj]
        sh_ref[j, 8:r + 8, :] = uj
        u1 = sh_ref[j, 7:r + 7, :]
        u2 = sh_ref[j, 6:r + 6, :]
        ys.append((w_ref[0:1, cj] * u2 + w_ref[1:2, cj] * u1) + w_ref[2:3, cj] * uj)
    return jnp.concatenate(ys, axis=1)


def _memkv_body(mem_ref, g_ref, wk_ref, wv_ref, k_ref, v_ref, kb_ref, vb_ref, *, nb):
    mn = _rms(mem_ref[...], g_ref[...]).astype(BF16)
    for w_ref, o_ref, ob_ref in ((wk_ref, k_ref, kb_ref), (wv_ref, v_ref, vb_ref)):
        y = _dot(mn, w_ref[...].astype(BF16))
        ob_ref[...] = y.astype(BF16)
        for b in range(nb):
            for hd in range(XH):
                for j in range(2):
                    c0 = hd * XHD + j * 128
                    o_ref[b, pl.ds(j * XH + hd, N_MEM, stride=8), :] = y[b * N_MEM:(b + 1) * N_MEM, c0:c0 + 128]


def _memkv(mem, g, wk, wv):
    rows = mem.shape[0]
    nb = 2
    tm = nb * N_MEM
    o_spec = pl.BlockSpec((None, nb, N_MEM * 8, 128), lambda l, i: (l, i, 0, 0))
    ob_spec = pl.BlockSpec((None, tm, D), lambda l, i: (l, i, 0))
    w_spec = pl.BlockSpec((None, D, D), lambda l, i: (l, 0, 0))
    return pl.pallas_call(
        functools.partial(_memkv_body, nb=nb),
        grid=(2, rows // tm),
        in_specs=[pl.BlockSpec((tm, D), lambda l, i: (i, 0)),
                  pl.BlockSpec((None, 1, D), lambda l, i: (l, 0, 0)),
                  w_spec, w_spec],
        out_specs=[o_spec, o_spec, ob_spec, ob_spec],
        out_shape=[jax.ShapeDtypeStruct((2, NB, N_MEM * 8, 128), F32)] * 2
        + [jax.ShapeDtypeStruct((2, rows, D), BF16)] * 2,
        compiler_params=_params(("arbitrary", "arbitrary"), 40),
        name="memkv",
    )(mem, g, wk, wv)


def _from_head_view(c):
    s = c.shape
    return c.reshape(s[0], s[1], N_MEM, 2, XH, 128).transpose(0, 1, 2, 4, 3, 5).reshape(s[0], s[1], N_MEM, XH, XHD)


def _even_front(x, gpre_ref, win, vn_ref):
    h = _rms(x, gpre_ref[...]).astype(BF16)
    u = _gelu(_dot(h, win(0, 512)))
    v = _rms(_gelu(_dot(h, win(512, 1024))), vn_ref[...])
    bg = _dot(h, win(1024, 1536))
    p = _dot(h, win(1536, 2048)) * _dot(h, win(2048, 2560))
    return u, v, bg, p


def _even_p_body(x_ref, gpre_ref, winf_ref, vn_ref, sw_ref, sb_ref, cw_ref, woutf_ref, gpost_ref,
                 xo_ref, st_ref, win_ref, wout_ref, carry_ref, cat_ref, sh_ref, *, tm):
    @pl.when((pl.program_id(0) == 0) & (pl.program_id(1) == 0))
    def _():
        win_ref[...] = winf_ref[...].astype(BF16)
        wout_ref[...] = woutf_ref[...].astype(BF16)

    @pl.when(pl.program_id(1) == 0)
    def _():
        carry_ref[...] = jnp.zeros_like(carry_ref)

    x = x_ref[...]
    u, v, bg, p = _even_front(x, gpre_ref, lambda c0, c1: win_ref[:, c0:c1], vn_ref)
    vb = v.astype(BF16)
    ri = lax.broadcasted_iota(jnp.int32, (CHUNK, CHUNK), 0)
    ci = lax.broadcasted_iota(jnp.int32, (CHUNK, CHUNK), 1)
    for g in range(4):
        cols = slice(g * 128, (g + 1) * 128)
        w = jnp.where(ri >= ci, sw_ref[g], 0.0).astype(BF16)
        for c in range(tm // CHUNK):
            rows = slice(c * CHUNK, (c + 1) * CHUNK)
            mixed = _dot(w, vb[rows, cols]) + sb_ref[:, cols]
            cat_ref[rows, cols] = (u[rows, cols] * mixed).astype(BF16)
    cz = _conv_rows(p, carry_ref, cw_ref, slice(0, SC_W), sh_ref)
    cat_ref[:, 512:1024] = (bg * cz).astype(BF16)
    carry_ref[...] = p[tm - 8:tm, :]
    st_ref[...] = p[tm - 8:tm, :]
    out = _dot(cat_ref[...], wout_ref[...])
    xo_ref[...] = x + _rms(out, gpost_ref[...])


def _even_p_parts(tm):
    nt = SEQ // tm
    row_spec = pl.BlockSpec((tm, D), lambda b, t: (b * nt + t, 0))
    in_specs = [row_spec, _const((1, D)), _layer_const(0, (D, 2560)), _const((1, SGU_W)),
                _layer_const(0, (4, CHUNK, CHUNK)), _const((CHUNK, SGU_W)), _layer_const(0, (3, SC_W)),
                _layer_const(0, (D, D)), _const((1, D))]
    out_specs = [row_spec, pl.BlockSpec((None, 8, SC_W), lambda b, t: (b, 0, 0))]
    out_shape = [jax.ShapeDtypeStruct((NB * SEQ, D), F32), jax.ShapeDtypeStruct((NB, 8, SC_W), F32)]
    scratch = [pltpu.VMEM((D, 2560), BF16), pltpu.VMEM((D, D), BF16), pltpu.VMEM((8, SC_W), F32),
               pltpu.VMEM((tm, D), BF16), pltpu.VMEM((SC_W // 128, tm + 8, 128), F32)]
    return nt, in_specs, out_specs, out_shape, scratch


def _even_s_body(x_ref, gpre_ref, win_ref, vn_ref, gw_ref, gb_ref, cw_ref, st_ref, wout_ref, gpost_ref,
                 xo_ref, ns_ref, v_ref, pan_ref):
    xt = _load_time_major(x_ref, pan_ref)
    u, v, bg, p = _even_front(xt, gpre_ref, lambda c0, c1: win_ref[:, c0:c1].astype(BF16), vn_ref)
    _store_batch_major(v_ref, v, pan_ref)
    vt = [v[t * NS:(t + 1) * NS, :] for t in range(TS)]
    mixed = []
    for t in range(TS):
        m = gb_ref[t:t + 1, :] + gw_ref[t, 0:1, :] * vt[0]
        for s in range(1, t + 1):
            m = m + gw_ref[t, s:s + 1, :] * vt[s]
        mixed.append(m)
    a = u * jnp.concatenate(mixed, axis=0)
    prev = [jnp.concatenate([st_ref[j * 2 + r] for j in range(SC_W // 128)], axis=1) for r in range(2)]
    cz = _conv_time_major(p, prev, cw_ref)
    for j in range(SC_W // 128):
        for r in range(2):
            ns_ref[j * 2 + r] = p[(TS - 2 + r) * NS:(TS - 1 + r) * NS, j * 128:(j + 1) * 128]
    cat = jnp.concatenate([a, bg * cz], axis=1).astype(BF16)
    out = _dot(cat, wout_ref[...].astype(BF16))
    _store_batch_major(xo_ref, xt + _rms(out, gpost_ref[...]), pan_ref)


def _even_s(x, gpre, win, vn, gw, gb, cw, st, wout, gpost):
    nst = SC_W // 128 * 2
    return pl.pallas_call(
        _even_s_body,
        grid=(1,),
        in_specs=[_const((RS, D)), _const((1, D)), _layer_const(0, (D, 2560)), _const((1, SGU_W)),
                  _const((TS, TS, SGU_W)), _const((TS, SGU_W)), _layer_const(0, (3, SC_W)), _const((nst, NS, 128)),
                  _layer_const(0, (D, D)), _const((1, D))],
        out_specs=[_full((RS, D)), _full((nst, NS, 128)), _full((RS, SGU_W))],
        out_shape=[jax.ShapeDtypeStruct((RS, D), F32), jax.ShapeDtypeStruct((nst, NS, 128), F32),
                   jax.ShapeDtypeStruct((RS, SGU_W), F32)],
        scratch_shapes=[pltpu.VMEM((D // 128, RS, 128), F32)],
        compiler_params=_params(("arbitrary",), 48),
        name="even_s",
    )(x, gpre, win, vn, gw, gb, cw, st, wout, gpost)


def _softmax_rows(s):
    m = jnp.max(s, axis=-1, keepdims=True)
    e = jnp.exp(s - m)
    return e * (1.0 / jnp.sum(e, axis=-1, keepdims=True))


def _xattn_p_body(x_ref, gpre_ref, wqf_ref, k_ref, v_ref, wof_ref, gpost_ref, xo_ref, wq_ref, wo_ref, q_ref, s_ref,
                  p_ref, o_ref):
    @pl.when((pl.program_id(0) == 0) & (pl.program_id(1) == 0))
    def _():
        wq_ref[...] = wqf_ref[...].astype(BF16)
        wo_ref[...] = wof_ref[...].astype(BF16)

    ng, gr = q_ref.shape[0], q_ref.shape[1]
    for a in range(ng):
        rows = slice(a * gr, (a + 1) * gr)
        h = _rms(x_ref[rows, :], gpre_ref[...]).astype(BF16)
        q_ref[a] = _dot(h, wq_ref[...]).astype(BF16)
        for hd in range(XH):
            cols = slice(hd * XHD, (hd + 1) * XHD)
            s_ref[a, hd] = _dot_nt(q_ref[a, :, cols], k_ref[:, cols]) * (XHD ** -0.5)
        p_ref[a] = _softmax_rows(s_ref[a]).astype(BF16)
        for hd in range(XH):
            cols = slice(hd * XHD, (hd + 1) * XHD)
            o_ref[a, :, cols] = _dot(p_ref[a, hd], v_ref[:, cols]).astype(BF16)
        out = _dot(o_ref[a], wo_ref[...])
        xo_ref[rows, :] = x_ref[rows, :] + _rms(out, gpost_ref[...])


def _xattn_p_parts(l, tm, groups=1):
    nt = SEQ // tm
    gr = tm // groups
    row_spec = pl.BlockSpec((tm, D), lambda b, t: (b * nt + t, 0))
    kv_spec = pl.BlockSpec((None, N_MEM, D), lambda b, t: (l, b, 0))
    in_specs = [row_spec, _const((1, D)), _layer_const(l, (D, D)), kv_spec, kv_spec, _layer_const(l, (D, D)),
                _const((1, D))]
    scratch = [pltpu.VMEM((D, D), BF16), pltpu.VMEM((D, D), BF16), pltpu.VMEM((groups, gr, D), BF16),
               pltpu.VMEM((groups, XH, gr, N_MEM), F32), pltpu.VMEM((groups, XH, gr, N_MEM), BF16),
               pltpu.VMEM((groups, gr, D), BF16)]
    return nt, row_spec, in_specs, scratch


def _xattn_p(l, x, gpre, wq, kb, vb, wo, gpost):
    nt, row_spec, in_specs, scratch = _xattn_p_parts(l, 1024, groups=2)
    return pl.pallas_call(
        _xattn_p_body,
        grid=(NB, nt),
        in_specs=in_specs,
        out_specs=row_spec,
        out_shape=jax.ShapeDtypeStruct((NB * SEQ, D), F32),
        scratch_shapes=scratch,
        compiler_params=_params(("arbitrary", "arbitrary"), 48),
        name="xattn_p",
    )(x, gpre, wq, kb, vb, wo, gpost)


def _head_view(c):
    s = c.shape
    return c.reshape(s[0], s[1], N_MEM, XH, 2, 128).transpose(0, 1, 2, 4, 3, 5).reshape(s[0], s[1], N_MEM * 8, 128)


def _head_rows(ref, b, hd):
    halves = [ref[b, pl.ds(j * XH + hd, N_MEM, stride=8), :] for j in range(2)]
    return jnp.concatenate(halves, axis=1).astype(BF16)


def _xattn_s_body(x_ref, gpre_ref, wq_ref, k_ref, v_ref, wo_ref, gpost_ref, xo_ref, q_ref, o_ref, s_ref, *, bb):
    _xattn_s_step(pl.program_id(0), pl.num_programs(0), x_ref, gpre_ref, wq_ref, k_ref, v_ref, wo_ref, gpost_ref,
                  xo_ref, q_ref, o_ref, s_ref, bb=bb)


def _xattn_s_step(step, nsteps, x_ref, gpre_ref, wq_ref, k_ref, v_ref, wo_ref, gpost_ref, xo_ref, q_ref, o_ref, s_ref,
                  *, bb):
    @pl.when(step == 0)
    def _():
        h = _rms(x_ref[...], gpre_ref[...]).astype(BF16)
        q_ref[...] = _dot(h, wq_ref[...].astype(BF16))

    first = (lax.broadcasted_iota(jnp.int32, (8, XHD), 0) >> 2) == 0
    groups = [(pi, hd, bi) for pi in range(bb // 2) for hd in range(XH) for bi in range(2)]
    row0 = [pl.multiple_of((step * (bb // 2) + pi) * 8, 8) for pi in range(bb // 2)]
    for gi, (pi, hd, bi) in enumerate(groups):
        q8 = q_ref[pl.ds(row0[pi], 8), hd * XHD:(hd + 1) * XHD].astype(BF16)
        kh = _head_rows(k_ref, pi * 2 + bi, hd)
        s_ref[gi * 8:(gi + 1) * 8, :] = _dot_nt(q8, kh) * (XHD ** -0.5)
    s_ref[...] = _softmax_rows(s_ref[...])
    for gi, (pi, hd, bi) in enumerate(groups):
        if bi == 1:
            continue
        pv = [_dot(s_ref[(gi + b) * 8:(gi + b + 1) * 8, :].astype(BF16), _head_rows(v_ref, pi * 2 + b, hd))
              for b in range(2)]
        o_ref[pl.ds(row0[pi], 8), hd * XHD:(hd + 1) * XHD] = jnp.where(first, pv[0], pv[1])

    @pl.when(step == nsteps - 1)
    def _():
        out = _dot(o_ref[...].astype(BF16), wo_ref[...].astype(BF16))
        xo_ref[...] = x_ref[...] + _rms(out, gpost_ref[...])


def _xattn_ps_body(x_ref, gpre_ref, wqf_ref, k_ref, v_ref, wof_ref, gpost_ref,
                   xs_ref, gpres_ref, ks_ref, vs_ref, gposts_ref,
                   xo_ref, xso_ref,
                   wq_ref, wo_ref, q_ref, s_ref, p_ref, o_ref, qs_ref, os_ref, ss_ref, *, bb):
    _xattn_p_body(x_ref, gpre_ref, wqf_ref, k_ref, v_ref, wof_ref, gpost_ref, xo_ref, wq_ref, wo_ref, q_ref, s_ref,
                  p_ref, o_ref)
    step = pl.program_id(0) * pl.num_programs(1) + pl.program_id(1)
    _xattn_s_step(step, pl.num_programs(0) * pl.num_programs(1), xs_ref, gpres_ref, wq_ref, ks_ref, vs_ref, wo_ref,
                  gposts_ref, xso_ref, qs_ref, os_ref, ss_ref, bb=bb)


def _xattn_ps(l, x, gpre, wq, kb, vb, wo, gpost, xs, gpres, ks, vs, gposts):
    bb = 2
    steps = NS // bb
    tm = NB * SEQ // steps
    nt, row_spec, h_in, h_scratch = _xattn_p_parts(l, tm)
    kv_spec = pl.BlockSpec((None, bb, N_MEM * 8, 128), lambda b, t: (l, b * nt + t, 0, 0))
    return pl.pallas_call(
        functools.partial(_xattn_ps_body, bb=bb),
        grid=(NB, nt),
        in_specs=h_in + [_const((RS, D)), _const((1, D)), kv_spec, kv_spec, _const((1, D))],
        out_specs=[row_spec, _full((RS, D))],
        out_shape=[jax.ShapeDtypeStruct((NB * SEQ, D), F32), jax.ShapeDtypeStruct((RS, D), F32)],
        scratch_shapes=h_scratch + [pltpu.VMEM((RS, D), F32), pltpu.VMEM((RS, D), F32),
                                    pltpu.VMEM((bb * XH * 8, N_MEM), F32)],
        compiler_params=_params(("arbitrary", "arbitrary"), 56),
        name="xattn_ps",
    )(x, gpre, wq, kb, vb, wo, gpost, xs, gpres, ks, vs, gposts)


def _xattn_s(l, x, gpre, wq, k, v, wo, gpost):
    bb = 4
    kv_spec = pl.BlockSpec((None, bb, N_MEM * 8, 128), lambda i: (l, i, 0, 0))
    return pl.pallas_call(
        functools.partial(_xattn_s_body, bb=bb),
        grid=(NS // bb,),
        in_specs=[_const((RS, D)), _const((1, D)), _layer_const(l, (D, D)), kv_spec, kv_spec, _layer_const(l, (D, D)),
                  _const((1, D))],
        out_specs=_full((RS, D)),
        out_shape=jax.ShapeDtypeStruct((RS, D), F32),
        scratch_shapes=[pltpu.VMEM((RS, D), F32), pltpu.VMEM((RS, D), F32), pltpu.VMEM((bb * XH * 8, N_MEM), F32)],
        compiler_params=_params(("arbitrary",), 48),
        name="xattn_s",
    )(x, gpre, wq, k, v, wo, gpost)


def _ffn_p_body(x_ref, gpre_ref, wg_ref, wv_ref, cw_ref, wdn_ref, gpost_ref, xo_ref, st_ref, carry_ref, h_ref, act_ref,
                *sh_refs, tm):
    @pl.when(pl.program_id(1) == 0)
    def _():
        carry_ref[...] = jnp.zeros_like(carry_ref)

    h_ref[...] = _rms(x_ref[...], gpre_ref[...]).astype(BF16)
    for c in range(NFC):
        halves = []
        for i, (base, w_ref) in enumerate(((0, wg_ref), (DFF, wv_ref))):
            cols = slice(base + c * FC, base + (c + 1) * FC)
            up = _dot(h_ref[...], w_ref[:, c * FC:(c + 1) * FC])
            halves.append(_conv_rows(up, carry_ref, cw_ref, cols, sh_refs[(2 * c + i) % len(sh_refs)]))
            carry_ref[:, cols] = up[tm - 8:tm, :]
            st_ref[:, cols] = up[tm - 8:tm, :]
        act_ref[:, c * FC:(c + 1) * FC] = (_gelu(halves[0]) * halves[1]).astype(BF16)
    out = _dot(act_ref[...], wdn_ref[...])
    xo_ref[...] = x_ref[...] + _rms(out, gpost_ref[...])


def _ffn_p_parts(l, tm):
    nt = SEQ // tm
    row_spec = pl.BlockSpec((tm, D), lambda b, t: (b * nt + t, 0))
    half = lambda i: pl.BlockSpec((D, DFF), lambda *_: (0, i), pipeline_mode=pl.Buffered(1))
    in_specs = [row_spec, _const((1, D)), half(0), half(1), _layer_const(l, (3, 2 * DFF)),
                _const((DFF, D)), _const((1, D))]
    out_specs = [row_spec, pl.BlockSpec((None, 8, 2 * DFF), lambda b, t: (b, 0, 0))]
    out_shape = [jax.ShapeDtypeStruct((NB * SEQ, D), F32), jax.ShapeDtypeStruct((NB, 8, 2 * DFF), F32)]
    scratch = ([pltpu.VMEM((8, 2 * DFF), F32), pltpu.VMEM((tm, D), BF16), pltpu.VMEM((tm, DFF), BF16)]
               + [pltpu.VMEM((FC // 128, tm + 8, 128), F32)] * 4)
    return nt, in_specs, out_specs, out_shape, scratch


def _ffn_p(l, x, gpre, wg, wv, cw, wdn, gpost):
    tm = 512
    nt, in_specs, out_specs, out_shape, scratch = _ffn_p_parts(l, tm)
    return pl.pallas_call(
        functools.partial(_ffn_p_body, tm=tm),
        grid=(NB, nt),
        in_specs=in_specs,
        out_specs=out_specs,
        out_shape=out_shape,
        scratch_shapes=scratch,
        compiler_params=_params(("arbitrary", "arbitrary"), 52),
        name="ffn_p",
    )(x, gpre, wg, wv, cw, wdn, gpost)


def _load_time_major(x_ref, pan_ref):
    n = x_ref.shape[1] // 128
    for p in range(n):
        pan_ref[p] = x_ref[:, p * 128:(p + 1) * 128]
    return jnp.concatenate(
        [jnp.concatenate([pan_ref[p, pl.ds(t, NS, stride=TS), :] for p in range(n)], axis=1) for t in range(TS)],
        axis=0)


def _store_batch_major(o_ref, y, pan_ref):
    n = o_ref.shape[1] // 128
    for p in range(n):
        for t in range(TS):
            pan_ref[p, pl.ds(t, NS, stride=TS), :] = y[t * NS:(t + 1) * NS, p * 128:(p + 1) * 128]
    for p in range(n):
        o_ref[:, p * 128:(p + 1) * 128] = pan_ref[p]


def _conv_time_major(u, prev, w_ref):
    blocks = list(prev) + [u[t * NS:(t + 1) * NS, :] for t in range(TS)]
    w0, w1, w2 = w_ref[0:1, :], w_ref[1:2, :], w_ref[2:3, :]
    return jnp.concatenate([(w0 * blocks[t] + w1 * blocks[t + 1]) + w2 * blocks[t + 2] for t in range(TS)], axis=0)


def _ffn_s_body(x_ref, gpre_ref, w_ref, cw_ref, st_ref, wdn_ref, gpost_ref,
                xo_ref, ns_ref, wb_ref, wdb_ref, xt_ref, h_ref, acc_ref, pan_ref, yg_ref):
    c, half = pl.program_id(0), pl.program_id(1)

    @pl.when((c == 0) & (half == 0))
    def _():
        xt = _load_time_major(x_ref, pan_ref)
        xt_ref[...] = xt
        h_ref[...] = _rms(xt, gpre_ref[...]).astype(BF16)
        acc_ref[...] = jnp.zeros_like(acc_ref)

    wb_ref[...] = w_ref[...].astype(BF16)
    u = _dot(h_ref[...], wb_ref[...])
    y = _conv_time_major(u, [st_ref[:, r, :] for r in range(2)], cw_ref)
    for r in range(2):
        ns_ref[:, r, :] = u[(TS - 2 + r) * NS:(TS - 1 + r) * NS, :]

    @pl.when(half == 0)
    def _():
        yg_ref[...] = y

    @pl.when(half == 1)
    def _():
        wdb_ref[...] = wdn_ref[...].astype(BF16)
        act = (_gelu(yg_ref[...]) * y).astype(BF16)
        acc_ref[...] += _dot(act, wdb_ref[...])

    @pl.when((c == pl.num_programs(0) - 1) & (half == 1))
    def _():
        _store_batch_major(xo_ref, xt_ref[...] + _rms(acc_ref[...], gpost_ref[...]), pan_ref)


def _ffn_s(l, x, gpre, wup, cw, st, wdn, gpost):
    col = lambda c, h: h * NFC + c
    return pl.pallas_call(
        _ffn_s_body,
        grid=(NFC, 2),
        in_specs=[_const((RS, D)), _const((1, D)),
                  pl.BlockSpec((None, D, FC), lambda c, h: (l, 0, col(c, h))),
                  pl.BlockSpec((None, 3, FC), lambda c, h: (l, 0, col(c, h))),
                  pl.BlockSpec((None, NS, 2, FC), lambda c, h: (l, 0, 0, col(c, h))),
                  pl.BlockSpec((None, FC, D), lambda c, h: (l, c, 0)), _const((1, D))],
        out_specs=[_full((RS, D)), pl.BlockSpec((NS, 2, FC), lambda c, h: (0, 0, col(c, h))),
                   pl.BlockSpec((D, FC), lambda c, h: (0, col(c, h))), pl.BlockSpec((FC, D), lambda c, h: (c, 0))],
        out_shape=[jax.ShapeDtypeStruct((RS, D), F32), jax.ShapeDtypeStruct((NS, 2, 2 * DFF), F32),
                   jax.ShapeDtypeStruct((D, 2 * DFF), BF16), jax.ShapeDtypeStruct((DFF, D), BF16)],
        scratch_shapes=[pltpu.VMEM((RS, D), F32), pltpu.VMEM((RS, D), BF16), pltpu.VMEM((RS, D), F32),
                        pltpu.VMEM((D // 128, RS, 128), F32), pltpu.VMEM((RS, FC), F32)],
        compiler_params=_params(("arbitrary", "arbitrary"), 40),
        name="ffn_s",
    )(x, gpre, wup, cw, st, wdn, gpost)


def _rope_head(z, cos, sin, base):
    x1 = z[:, base:base + 128]
    x2 = z[:, base + 128:base + 256]
    return x1 * cos - x2 * sin, x1 * sin + x2 * cos


def _group_norm_gate(o, g):
    mu = jnp.mean(o, axis=-1, keepdims=True)
    d = o - mu
    var = jnp.mean(d * d, axis=-1, keepdims=True)
    return (g * jax.nn.sigmoid(g)) * (d * lax.rsqrt(var + EPS))


RCH = 256


def _ret_p_body(x_ref, cos_ref, sin_ref, gpre_ref, wq_ref, wk_ref, wv_ref, wg_ref, wout_ref, gpost_ref, xo_ref, s_ref,
                h_ref, q_ref, k_ref, kd_ref, v_ref, g_ref, y_ref, dm_ref, qd_ref, kdec_ref, *, tm):
    first = (pl.program_id(0) == 0) & (pl.program_id(1) == 0)

    @pl.when(first)
    def _():
        i = lax.broadcasted_iota(jnp.int32, (RCH, RCH), 0).astype(F32)
        j = lax.broadcasted_iota(jnp.int32, (RCH, RCH), 1).astype(F32)
        i1 = lax.broadcasted_iota(jnp.int32, (RCH, 128), 0).astype(F32)
        for hd in range(RET_H):
            lg = _LOG_G[hd]
            dm_ref[hd] = jnp.where(i >= j, jnp.exp(lg * jnp.maximum(i - j, 0.0)), 0.0)
            qd_ref[hd] = jnp.exp(lg * (i1 + 1.0))
            kdec_ref[hd] = jnp.exp(lg * (RCH - 1.0 - i1))

    @pl.when(pl.program_id(1) == 0)
    def _():
        s_ref[...] = jnp.zeros_like(s_ref)

    h_ref[...] = _rms(x_ref[...], gpre_ref[...]).astype(BF16)
    cos = cos_ref[...]
    sin = sin_ref[...]
    nchunk = tm // RCH
    for hd in range(RET_H):
        base = hd * RET_DK
        zq = _dot(h_ref[...], wq_ref[:, base:base + RET_DK])
        zk = _dot(h_ref[...], wk_ref[:, base:base + RET_DK])
        q1, q2 = _rope_head(zq, cos, sin, 0)
        k1, k2 = _rope_head(zk, cos, sin, 0)
        q_ref[:, base:base + 128] = (q1 * (RET_DK ** -0.5)).astype(BF16)
        q_ref[:, base + 128:base + 256] = (q2 * (RET_DK ** -0.5)).astype(BF16)
        k_ref[:, base:base + 128] = k1.astype(BF16)
        k_ref[:, base + 128:base + 256] = k2.astype(BF16)
        kdec = jnp.concatenate([kdec_ref[hd]] * nchunk, axis=0)
        kd_ref[:, base:base + 128] = k1 * kdec
        kd_ref[:, base + 128:base + 256] = k2 * kdec
        vcols = slice(hd * RET_DV, (hd + 1) * RET_DV)
        v_ref[:, vcols] = _dot(h_ref[...], wv_ref[:, vcols]).astype(BF16)
        g_ref[:, vcols] = _dot(h_ref[...], wg_ref[:, vcols])
    for hd in range(RET_H):
        kcols = slice(hd * RET_DK, (hd + 1) * RET_DK)
        vcols = slice(hd * RET_DV, (hd + 1) * RET_DV)
        decay_l = float(np.exp(np.float32(_LOG_G[hd]) * np.float32(RCH)))
        qd = qd_ref[hd]
        qd4 = jnp.concatenate([qd, qd, qd, qd], axis=1)
        for c in range(nchunk):
            rows = slice(c * RCH, (c + 1) * RCH)
            qc = q_ref[rows, kcols]
            vc = v_ref[rows, vcols]
            s_prev = s_ref[hd]
            sc = (_dot_nt(qc, k_ref[rows, kcols]) * dm_ref[hd]).astype(BF16)
            o = _dot(sc, vc) + _dot(qc, s_prev.astype(BF16)) * qd4
            kdt = kd_ref[rows, kcols].T.astype(BF16)
            s_ref[hd] = decay_l * s_prev + _dot(kdt, vc)
            y_ref[rows, vcols] = _group_norm_gate(o, g_ref[rows, vcols]).astype(BF16)
    out = _dot(y_ref[...], wout_ref[...])
    xo_ref[...] = x_ref[...] + _rms(out, gpost_ref[...])


def _ret_p(x, cos, sin, gpre, wq, wk, wv, wg, wout, gpost):
    tm = 512
    nt = SEQ // tm
    row_spec = pl.BlockSpec((tm, D), lambda b, t: (b * nt + t, 0))
    cs_spec = pl.BlockSpec((tm, 128), lambda b, t: (t, 0))
    return pl.pallas_call(
        functools.partial(_ret_p_body, tm=tm),
        grid=(NB, nt),
        in_specs=[row_spec, cs_spec, cs_spec, _const((1, D)), _const((D, NQ)), _const((D, NQ)), _const((D, NV)),
                  _const((D, NV)), _const((NV, D)), _const((1, D))],
        out_specs=[row_spec, pl.BlockSpec((None, RET_H, RET_DK, RET_DV), lambda b, t: (b, 0, 0, 0))],
        out_shape=[jax.ShapeDtypeStruct((NB * SEQ, D), F32), jax.ShapeDtypeStruct((NB, RET_H, RET_DK, RET_DV), F32)],
        scratch_shapes=[pltpu.VMEM((tm, D), BF16), pltpu.VMEM((tm, NQ), BF16), pltpu.VMEM((tm, NQ), BF16),
                        pltpu.VMEM((tm, NQ), F32), pltpu.VMEM((tm, NV), BF16), pltpu.VMEM((tm, NV), F32),
                        pltpu.VMEM((tm, NV), BF16), pltpu.VMEM((RET_H, RCH, RCH), F32),
                        pltpu.VMEM((RET_H, RCH, 128), F32), pltpu.VMEM((RET_H, RCH, 128), F32)],
        compiler_params=_params(("arbitrary", "arbitrary"), 56),
        name="ret_p",
    )(x, cos, sin, gpre, wq, wk, wv, wg, wout, gpost)


def _ret_s_proj_body(x_ref, cos_ref, sin_ref, gpre_ref, wqf_ref, wkf_ref, wvf_ref, wgf_ref,
                     q_ref, kdt_ref, v_ref, g_ref, oi_ref, wq_ref, wk_ref, wv_ref, wg_ref, h_ref):
    hd = pl.program_id(0)

    @pl.when(hd == 0)
    def _():
        h_ref[...] = _rms(x_ref[...], gpre_ref[...]).astype(BF16)

    for wf_ref, wb_ref in ((wqf_ref, wq_ref), (wkf_ref, wk_ref), (wvf_ref, wv_ref), (wgf_ref, wg_ref)):
        wb_ref[...] = wf_ref[...].astype(BF16)
    lg = jnp.where(hd == 0, _LOG_G[0], jnp.where(hd == 1, _LOG_G[1], jnp.where(hd == 2, _LOG_G[2], _LOG_G[3])))
    h = h_ref[...]
    cos = cos_ref[...]
    sin = sin_ref[...]
    ti = (lax.broadcasted_iota(jnp.int32, (RS, 128), 0) & (TS - 1)).astype(F32)
    i = lax.broadcasted_iota(jnp.int32, (CHUNK, CHUNK), 0)
    j = lax.broadcasted_iota(jnp.int32, (CHUNK, CHUNK), 1)
    same = ((i >> 2) == (j >> 2)) & (i >= j)
    dm = jnp.where(same, jnp.exp(lg * jnp.maximum(i - j, 0).astype(F32)), 0.0)
    q1, q2 = _rope_head(_dot(h, wq_ref[...]), cos, sin, 0)
    k1, k2 = _rope_head(_dot(h, wk_ref[...]), cos, sin, 0)
    q = jnp.concatenate([q1, q2], axis=1) * (RET_DK ** -0.5)
    k = jnp.concatenate([k1, k2], axis=1)
    q_ref[...] = q
    kdec = jnp.exp(lg * (TS - 1.0 - ti))
    kd = k * jnp.concatenate([kdec, kdec], axis=1)
    vf = _dot(h, wv_ref[...])
    v_ref[...] = vf
    v = vf.astype(BF16)
    g_ref[...] = _dot(h, wg_ref[...])
    qb = q.astype(BF16)
    kb = k.astype(BF16)
    for tl in range(RS // CHUNK):
        rows = slice(tl * CHUNK, (tl + 1) * CHUNK)
        sc = (_dot_nt(qb[rows], kb[rows]) * dm).astype(BF16)
        oi_ref[rows, :] = _dot(sc, v[rows])
        kdt_ref[tl] = kd[rows].T.astype(BF16)


def _ret_s_proj(x, cos, sin, gpre, win):
    def wcol(width, first_block):
        return pl.BlockSpec((None, D, width), lambda h: (0, 0, first_block + h))
    def hcol(rows, width):
        return pl.BlockSpec((rows, width), lambda h: (0, h))
    return pl.pallas_call(
        _ret_s_proj_body,
        grid=(RET_H,),
        in_specs=[_const((RS, D)), _const((RS, 128)), _const((RS, 128)), _const((1, D)),
                  wcol(RET_DK, 0), wcol(RET_DK, NQ // RET_DK), wcol(RET_DV, 2 * NQ // RET_DV),
                  wcol(RET_DV, (2 * NQ + NV) // RET_DV)],
        out_specs=[hcol(RS, RET_DK),
                   pl.BlockSpec((RS // CHUNK, None, RET_DK, CHUNK), lambda h: (0, h, 0, 0)),
                   hcol(RS, RET_DV), hcol(RS, RET_DV), hcol(RS, RET_DV),
                   hcol(D, RET_DK), hcol(D, RET_DK), hcol(D, RET_DV), hcol(D, RET_DV)],
        out_shape=[jax.ShapeDtypeStruct((RS, NQ), F32),
                   jax.ShapeDtypeStruct((RS // CHUNK, RET_H, RET_DK, CHUNK), BF16),
                   jax.ShapeDtypeStruct((RS, NV), F32), jax.ShapeDtypeStruct((RS, NV), F32),
                   jax.ShapeDtypeStruct((RS, NV), F32),
                   jax.ShapeDtypeStruct((D, NQ), BF16), jax.ShapeDtypeStruct((D, NQ), BF16),
                   jax.ShapeDtypeStruct((D, NV), BF16), jax.ShapeDtypeStruct((D, NV), BF16)],
        scratch_shapes=[pltpu.VMEM((RS, D), BF16)],
        compiler_params=_params(("arbitrary",), 40),
        name="ret_s_proj",
    )(x, cos, sin, gpre, win, win, win, win)


def _ret_s_core_body(step, q_ref, kdt_ref, v_ref, oi_ref, s_ref, o_ref, so_ref):
    pair_in_tile = step % (CHUNK // 8)
    row8 = lax.broadcasted_iota(jnp.int32, (8, RET_DK), 0)
    row128 = lax.broadcasted_iota(jnp.int32, (CHUNK, RET_DV), 0)
    t8 = (lax.broadcasted_iota(jnp.int32, (8, RET_DV), 0) & (TS - 1)).astype(F32)
    for hd in range(RET_H):
        lg = _LOG_G[hd]
        decay_l = float(np.exp(np.float32(lg) * np.float32(TS)))
        q8 = q_ref[:, hd * RET_DK:(hd + 1) * RET_DK]
        v128 = v_ref[:, hd * RET_DV:(hd + 1) * RET_DV]
        kdt = kdt_ref[hd]
        inter = jnp.zeros((8, RET_DV), F32)
        for bi in range(2):
            s_prev = s_ref[bi, hd]
            qm = jnp.where((row8 >> 2) == bi, q8, 0.0).astype(BF16)
            inter = inter + _dot(qm, s_prev.astype(BF16))
            vm = jnp.where((row128 >> 2) == pair_in_tile * 2 + bi, v128, 0.0).astype(BF16)
            so_ref[bi, hd] = decay_l * s_prev + _dot(kdt, vm)
        cols = slice(hd * RET_DV, (hd + 1) * RET_DV)
        o_ref[:, cols] = oi_ref[:, cols] + inter * jnp.exp(lg * (t8 + 1.0))


def _ret_s_core_parts():
    ppt = CHUNK // 8
    s_spec = pl.BlockSpec((2, RET_H, RET_DK, RET_DV), lambda i: (i, 0, 0, 0))
    in_specs = [pl.BlockSpec((8, NQ), lambda i: (i, 0)),
                pl.BlockSpec((None, RET_H, RET_DK, CHUNK), lambda i: (i // ppt, 0, 0, 0)),
                pl.BlockSpec((CHUNK, NV), lambda i: (i // ppt, 0)),
                pl.BlockSpec((8, NV), lambda i: (i, 0)),
                s_spec]
    out_specs = [pl.BlockSpec((8, NV), lambda i: (i, 0)), s_spec]
    out_shape = [jax.ShapeDtypeStruct((RS, NV), F32), jax.ShapeDtypeStruct((NS, RET_H, RET_DK, RET_DV), F32)]
    return in_specs, out_specs, out_shape


def _host_ret_s_body(*refs, host_body, n_in, n_out):
    host_in, guest_in = refs[:n_in], refs[n_in:n_in + 5]
    k = n_in + 5
    host_out, guest_out, scratch = refs[k:k + n_out], refs[k + n_out:k + n_out + 2], refs[k + n_out + 2:]
    host_body(*host_in, *host_out, *scratch)
    step = pl.program_id(0) * pl.num_programs(1) + pl.program_id(1)
    _ret_s_core_body(step, *guest_in, *guest_out)


def _with_ret_s(name, host_body, nt, h_in, h_out, h_shape, scratch, host_args, guest_args):
    g_in, g_out, g_shape = _ret_s_core_parts()

    def on_grid(spec):
        return pl.BlockSpec(spec.block_shape, lambda b, t, m=spec.index_map: m(b * nt + t))

    return pl.pallas_call(
        functools.partial(_host_ret_s_body, host_body=host_body, n_in=len(h_in), n_out=len(h_out)),
        grid=(NB, nt),
        in_specs=h_in + [on_grid(sp) for sp in g_in],
        out_specs=h_out + [on_grid(sp) for sp in g_out],
        out_shape=h_shape + g_shape,
        scratch_shapes=scratch,
        compiler_params=_params(("arbitrary", "arbitrary"), 56),
        name=name,
    )(*host_args, *guest_args)


def _ret_s_out_body(x_ref, o_ref, g_ref, woutf_ref, gpost_ref, xo_ref, wout_ref, y_ref):
    wout_ref[...] = woutf_ref[...].astype(BF16)
    for hd in range(RET_H):
        cols = slice(hd * RET_DV, (hd + 1) * RET_DV)
        y_ref[:, cols] = _group_norm_gate(o_ref[:, cols], g_ref[:, cols]).astype(BF16)
    out = _dot(y_ref[...], wout_ref[...])
    xo_ref[...] = x_ref[...] + _rms(out, gpost_ref[...])


def _ret_s_out(x, o, g, wout, gpost):
    return pl.pallas_call(
        _ret_s_out_body,
        grid=(1,),
        in_specs=[_const((RS, D)), _const((RS, NV)), _const((RS, NV)), _layer_const(0, (NV, D)), _const((1, D))],
        out_specs=[_full((RS, D)), _full((NV, D))],
        out_shape=[jax.ShapeDtypeStruct((RS, D), F32), jax.ShapeDtypeStruct((NV, D), BF16)],
        scratch_shapes=[pltpu.VMEM((RS, NV), BF16)],
        compiler_params=_params(("arbitrary",), 48),
        name="ret_s_out",
    )(x, o, g, wout, gpost)


def _rope_tables(pos):
    half = RET_DK // 2
    inv = ROPE_BASE ** (-jnp.arange(half, dtype=F32) / half)
    ang = pos.astype(F32)[:, None] * inv[None, :]
    return jnp.cos(ang), jnp.sin(ang)


def _state_rows(st):
    lead, c = st.shape[:-3], st.shape[-1]
    n = len(lead)
    s = st.reshape(lead + (NS, 2, c // 128, 128))
    s = s.transpose(tuple(range(n)) + (n + 2, n + 1, n, n + 3))
    return s.reshape(lead + (c // 128 * 2, NS, 128))


def _from_state_rows(s):
    tiles = s.shape[0] // 2
    return s.reshape(tiles, 2, NS, 128).transpose(2, 1, 0, 3).reshape(NS, 2, tiles * 128)


def kernel(x_prompt, x_sample, mem_prompt, cache_mem_k, cache_mem_v, state_shortconv, state_retention, state_ffn_conv, norm_mix_pre, norm_mix_post, w_in_even, sgu_vnorm, sgu_w, sgu_b, conv_short, w_out_even, w_in_odd, w_out_odd, norm_x_pre, norm_x_post, norm_mem, w_xq, w_xk, w_xv, w_xo, norm_ffn_pre, norm_ffn_post, w_ffn_up, conv_ffn, w_ffn_down):
    row = lambda g: g.reshape(1, -1)

    xs = x_sample.reshape(RS, D)
    w4 = jnp.tril(sgu_w[0][:, :TS, :TS])
    gw = jnp.repeat(w4.transpose(1, 2, 0), CHUNK, axis=2)
    gb = jnp.repeat(sgu_b[0][:, :TS].T, CHUNK, axis=1)
    xs, sc_s, v_s = _even_s(xs, row(norm_mix_pre[0]), w_in_even, row(sgu_vnorm[0]), gw, gb, conv_short,
                            _state_rows(state_shortconv[0]), w_out_even, row(norm_mix_post[0]))
    cos_s, sin_s = _rope_tables(PAST + (jnp.arange(RS, dtype=jnp.int32) % TS))
    cache_k, cache_v = _head_view(cache_mem_k), _head_view(cache_mem_v)
    ffn_s_states, ffn_p_states, ffn_w = [], [], []

    def sample_ffn(l, xs):
        xs, ns, wup_b, wd_b = _ffn_s(l, xs, row(norm_ffn_pre[l]), w_ffn_up, conv_ffn, state_ffn_conv, w_ffn_down,
                                     row(norm_ffn_post[l]))
        ffn_s_states.append(ns)
        ffn_w.append((wup_b, wup_b, wd_b))
        return xs

    def prompt_ffn(l, xp):
        xp, st = _ffn_p(l, xp, row(norm_ffn_pre[l]), ffn_w[l][0], ffn_w[l][1], conv_ffn, ffn_w[l][2],
                        row(norm_ffn_post[l]))
        ffn_p_states.append(st[:, 6:8, :])
        return xp

    xs = _xattn_s(0, xs, row(norm_x_pre[0]), w_xq, cache_k, cache_v, w_xo, row(norm_x_post[0]))
    xs = sample_ffn(0, xs)
    q, kdt, v, g, oi, wq_o, wk_o, wv_o, wg_o = _ret_s_proj(xs, cos_s, sin_s, row(norm_mix_pre[1]), w_in_odd)

    mem_k, mem_v, mem_kb, mem_vb = _memkv(mem_prompt.reshape(NB * N_MEM, D), norm_mem.reshape(2, 1, D), w_xk, w_xv)
    xp = x_prompt.reshape(NB * SEQ, D)
    sb_full = jnp.repeat(sgu_b[0].T, CHUNK, axis=1)
    tm = NB * SEQ // (NS // 2)
    nt, e_in, e_out, e_shape, e_scratch = _even_p_parts(tm)
    xp, sc_p, o, ret_s_state = _with_ret_s(
        "even_p_ret_s", functools.partial(_even_p_body, tm=tm), nt, e_in, e_out, e_shape, e_scratch,
        (xp, row(norm_mix_pre[0]), w_in_even, row(sgu_vnorm[0]), sgu_w, sb_full, conv_short, w_out_even,
         row(norm_mix_post[0])), (q, kdt, v, oi, state_retention[0]))
    xs, wout_o = _ret_s_out(xs, o, g, w_out_odd, row(norm_mix_post[1]))

    xp = _xattn_p(0, xp, row(norm_x_pre[0]), w_xq, mem_kb, mem_vb, w_xo, row(norm_x_post[0]))
    xp = prompt_ffn(0, xp)
    cos_p, sin_p = _rope_tables(jnp.arange(SEQ, dtype=jnp.int32))
    xp, ret_p_state = _ret_p(xp, cos_p, sin_p, row(norm_mix_pre[1]), wq_o, wk_o, wv_o, wg_o, wout_o,
                             row(norm_mix_post[1]))
    xp, xs = _xattn_ps(1, xp, row(norm_x_pre[1]), w_xq, mem_kb, mem_vb, w_xo, row(norm_x_post[1]),
                       xs, row(norm_x_pre[1]), cache_k, cache_v, row(norm_x_post[1]))
    xs = sample_ffn(1, xs)
    xp = prompt_ffn(1, xp)

    return (xp.reshape(NB, SEQ, D), xs.reshape(NS, TS, D),
            _from_head_view(mem_k), _from_head_view(mem_v),
            sc_p[None, :, 6:8, :], _from_state_rows(sc_s)[None],
            v_s.reshape(1, NS, TS, SGU_W),
            ret_p_state[None], ret_s_state[None],
            jnp.stack(ffn_p_states), jnp.stack(ffn_s_states))
```

```python
import functools

import numpy as np
import jax
import jax.numpy as jnp
from jax import lax
from jax.experimental import pallas as pl
from jax.experimental.pallas import tpu as pltpu

F32 = jnp.float32
BF16 = jnp.bfloat16

D = 1024
SEQ = 2048
NB = 8
NS = 128
TS = 4
RS = NS * TS
PAST = 16384
CHUNK = 128
EPS = 1e-6
SGU_W = 512
SC_W = 512
RET_H = 4
RET_DK = 256
RET_DV = 512
NQ = RET_H * RET_DK
NV = RET_H * RET_DV
N_MEM = 256
XH = 4
XHD = 256
DFF = 2816
FC = 256
NFC = DFF // FC
ROPE_BASE = 10000.0

_LOG_G = [float(v) for v in np.log1p(-np.exp2(np.float32(-5.0) - np.arange(RET_H, dtype=np.float32))).astype(np.float32)]

_MIB = 1024 * 1024


def _rms(x, g):
    ms = jnp.mean(x * x, axis=-1, keepdims=True)
    return (x * lax.rsqrt(ms + EPS)) * g


_GELU_C = 0.7978845608028654
_LOG2E = 1.4426950408889634


def _gelu(x):
    k0 = -2.0 * _LOG2E * _GELU_C
    k1 = k0 * 0.044715
    return x * (1.0 / (1.0 + jnp.exp2(x * (k0 + k1 * (x * x)))))


def _dot(a, b):
    return jnp.dot(a, b, preferred_element_type=F32)


def _dot_nt(a, b):
    return lax.dot_general(a, b, (((1,), (1,)), ((), ())), preferred_element_type=F32)


def _const(shape):
    n = len(shape)
    return pl.BlockSpec(shape, lambda *_: (0,) * n, pipeline_mode=pl.Buffered(1))


def _layer_const(l, shape):
    n = len(shape)
    return pl.BlockSpec((None,) + tuple(shape), lambda *_: (l,) + (0,) * n, pipeline_mode=pl.Buffered(1))


def _full(shape):
    n = len(shape)
    return pl.BlockSpec(shape, lambda *_: (0,) * n)


def _params(sem, vmem_mib):
    return pltpu.CompilerParams(dimension_semantics=sem, vmem_limit_bytes=vmem_mib * _MIB)


def _conv_rows(u, prev_ref, w_ref, cols, sh_ref):
    r, c = u.shape
    ys = []
    for j in range(c // 128):
        cj = slice(cols.start + j * 128, cols.start + (j + 1) * 128)
        uj = u[:, j * 128:(j + 1) * 128]
        sh_ref[j, 0:8, :] = prev_ref[:, cj]
        sh_ref[j, 8:r + 8, :] = uj
        u1 = sh_ref[j, 7:r + 7, :]
        u2 = sh_ref[j, 6:r + 6, :]
        ys.append((w_ref[0:1, cj] * u2 + w_ref[1:2, cj] * u1) + w_ref[2:3, cj] * uj)
    return jnp.concatenate(ys, axis=1)


def _memkv_body(mem_ref, g_ref, wk_ref, wv_ref, k_ref, v_ref, kb_ref, vb_ref, *, nb):
    mn = _rms(mem_ref[...], g_ref[...]).astype(BF16)
    for w_ref, o_ref, ob_ref in ((wk_ref, k_ref, kb_ref), (wv_ref, v_ref, vb_ref)):
        y = _dot(mn, w_ref[...].astype(BF16))
        ob_ref[...] = y.astype(BF16)
        for b in range(nb):
            for hd in range(XH):
                for j in range(2):
                    c0 = hd * XHD + j * 128
                    o_ref[b, pl.ds(j * XH + hd, N_MEM, stride=8), :] = y[b * N_MEM:(b + 1) * N_MEM, c0:c0 + 128]


def _memkv(mem, g, wk, wv):
    rows = mem.shape[0]
    nb = 2
    tm = nb * N_MEM
    o_spec = pl.BlockSpec((None, nb, N_MEM * 8, 128), lambda l, i: (l, i, 0, 0))
    ob_spec = pl.BlockSpec((None, tm, D), lambda l, i: (l, i, 0))
    w_spec = pl.BlockSpec((None, D, D), lambda l, i: (l, 0, 0))
    return pl.pallas_call(
        functools.partial(_memkv_body, nb=nb),
        grid=(2, rows // tm),
        in_specs=[pl.BlockSpec((tm, D), lambda l, i: (i, 0)),
                  pl.BlockSpec((None, 1, D), lambda l, i: (l, 0, 0)),
                  w_spec, w_spec],
        out_specs=[o_spec, o_spec, ob_spec, ob_spec],
        out_shape=[jax.ShapeDtypeStruct((2, NB, N_MEM * 8, 128), F32)] * 2
        + [jax.ShapeDtypeStruct((2, rows, D), BF16)] * 2,
        compiler_params=_params(("arbitrary", "arbitrary"), 40),
        name="memkv",
    )(mem, g, wk, wv)


def _from_head_view(c):
    s = c.shape
    return c.reshape(s[0], s[1], N_MEM, 2, XH, 128).transpose(0, 1, 2, 4, 3, 5).reshape(s[0], s[1], N_MEM, XH, XHD)


def _even_front(x, gpre_ref, win, vn_ref):
    h = _rms(x, gpre_ref[...]).astype(BF16)
    u = _gelu(_dot(h, win(0, 512)))
    v = _rms(_gelu(_dot(h, win(512, 1024))), vn_ref[...])
    bg = _dot(h, win(1024, 1536))
    p = _dot(h, win(1536, 2048)) * _dot(h, win(2048, 2560))
    return u, v, bg, p


def _even_p_body(x_ref, gpre_ref, winf_ref, vn_ref, sw_ref, sb_ref, cw_ref, woutf_ref, gpost_ref,
                 xo_ref, st_ref, win_ref, wout_ref, carry_ref, cat_ref, sh_ref, *, tm):
    @pl.when((pl.program_id(0) == 0) & (pl.program_id(1) == 0))
    def _():
        win_ref[...] = winf_ref[...].astype(BF16)
        wout_ref[...] = woutf_ref[...].astype(BF16)

    @pl.when(pl.program_id(1) == 0)
    def _():
        carry_ref[...] = jnp.zeros_like(carry_ref)

    x = x_ref[...]
    u, v, bg, p = _even_front(x, gpre_ref, lambda c0, c1: win_ref[:, c0:c1], vn_ref)
    vb = v.astype(BF16)
    ri = lax.broadcasted_iota(jnp.int32, (CHUNK, CHUNK), 0)
    ci = lax.broadcasted_iota(jnp.int32, (CHUNK, CHUNK), 1)
    for g in range(4):
        cols = slice(g * 128, (g + 1) * 128)
        w = jnp.where(ri >= ci, sw_ref[g], 0.0).astype(BF16)
        for c in range(tm // CHUNK):
            rows = slice(c * CHUNK, (c + 1) * CHUNK)
            mixed = _dot(w, vb[rows, cols]) + sb_ref[:, cols]
            cat_ref[rows, cols] = (u[rows, cols] * mixed).astype(BF16)
    cz = _conv_rows(p, carry_ref, cw_ref, slice(0, SC_W), sh_ref)
    cat_ref[:, 512:1024] = (bg * cz).astype(BF16)
    carry_ref[...] = p[tm - 8:tm, :]
    st_ref[...] = p[tm - 8:tm, :]
    out = _dot(cat_ref[...], wout_ref[...])
    xo_ref[...] = x + _rms(out, gpost_ref[...])


def _even_p_parts(tm):
    nt = SEQ // tm
    row_spec = pl.BlockSpec((tm, D), lambda b, t: (b * nt + t, 0))
    in_specs = [row_spec, _const((1, D)), _layer_const(0, (D, 2560)), _const((1, SGU_W)),
                _layer_const(0, (4, CHUNK, CHUNK)), _const((CHUNK, SGU_W)), _layer_const(0, (3, SC_W)),
                _layer_const(0, (D, D)), _const((1, D))]
    out_specs = [row_spec, pl.BlockSpec((None, 8, SC_W), lambda b, t: (b, 0, 0))]
    out_shape = [jax.ShapeDtypeStruct((NB * SEQ, D), F32), jax.ShapeDtypeStruct((NB, 8, SC_W), F32)]
    scratch = [pltpu.VMEM((D, 2560), BF16), pltpu.VMEM((D, D), BF16), pltpu.VMEM((8, SC_W), F32),
               pltpu.VMEM((tm, D), BF16), pltpu.VMEM((SC_W // 128, tm + 8, 128), F32)]
    return nt, in_specs, out_specs, out_shape, scratch


def _even_s_body(x_ref, gpre_ref, win_ref, vn_ref, gw_ref, gb_ref, cw_ref, st_ref, wout_ref, gpost_ref,
                 xo_ref, ns_ref, v_ref, pan_ref):
    xt = _load_time_major(x_ref, pan_ref)
    u, v, bg, p = _even_front(xt, gpre_ref, lambda c0, c1: win_ref[:, c0:c1].astype(BF16), vn_ref)
    _store_batch_major(v_ref, v, pan_ref)
    vt = [v[t * NS:(t + 1) * NS, :] for t in range(TS)]
    mixed = []
    for t in range(TS):
        m = gb_ref[t:t + 1, :] + gw_ref[t, 0:1, :] * vt[0]
        for s in range(1, t + 1):
            m = m + gw_ref[t, s:s + 1, :] * vt[s]
        mixed.append(m)
    a = u * jnp.concatenate(mixed, axis=0)
    prev = [jnp.concatenate([st_ref[j * 2 + r] for j in range(SC_W // 128)], axis=1) for r in range(2)]
    cz = _conv_time_major(p, prev, cw_ref)
    for j in range(SC_W // 128):
        for r in range(2):
            ns_ref[j * 2 + r] = p[(TS - 2 + r) * NS:(TS - 1 + r) * NS, j * 128:(j + 1) * 128]
    cat = jnp.concatenate([a, bg * cz], axis=1).astype(BF16)
    out = _dot(cat, wout_ref[...].astype(BF16))
    _store_batch_major(xo_ref, xt + _rms(out, gpost_ref[...]), pan_ref)


def _even_s(x, gpre, win, vn, gw, gb, cw, st, wout, gpost):
    nst = SC_W // 128 * 2
    return pl.pallas_call(
        _even_s_body,
        grid=(1,),
        in_specs=[_const((RS, D)), _const((1, D)), _layer_const(0, (D, 2560)), _const((1, SGU_W)),
                  _const((TS, TS, SGU_W)), _const((TS, SGU_W)), _layer_const(0, (3, SC_W)), _const((nst, NS, 128)),
                  _layer_const(0, (D, D)), _const((1, D))],
        out_specs=[_full((RS, D)), _full((nst, NS, 128)), _full((RS, SGU_W))],
        out_shape=[jax.ShapeDtypeStruct((RS, D), F32), jax.ShapeDtypeStruct((nst, NS, 128), F32),
                   jax.ShapeDtypeStruct((RS, SGU_W), F32)],
        scratch_shapes=[pltpu.VMEM((D // 128, RS, 128), F32)],
        compiler_params=_params(("arbitrary",), 48),
        name="even_s",
    )(x, gpre, win, vn, gw, gb, cw, st, wout, gpost)


def _softmax_rows(s):
    m = jnp.max(s, axis=-1, keepdims=True)
    e = jnp.exp(s - m)
    return e * (1.0 / jnp.sum(e, axis=-1, keepdims=True))


def _xattn_p_body(x_ref, gpre_ref, wqf_ref, k_ref, v_ref, wof_ref, gpost_ref, xo_ref, wq_ref, wo_ref, q_ref, s_ref,
                  p_ref, o_ref):
    @pl.when((pl.program_id(0) == 0) & (pl.program_id(1) == 0))
    def _():
        wq_ref[...] = wqf_ref[...].astype(BF16)
        wo_ref[...] = wof_ref[...].astype(BF16)

    ng, gr = q_ref.shape[0], q_ref.shape[1]
    for a in range(ng):
        rows = slice(a * gr, (a + 1) * gr)
        h = _rms(x_ref[rows, :], gpre_ref[...]).astype(BF16)
        q_ref[a] = _dot(h, wq_ref[...]).astype(BF16)
        for hd in range(XH):
            cols = slice(hd * XHD, (hd + 1) * XHD)
            s_ref[a, hd] = _dot_nt(q_ref[a, :, cols], k_ref[:, cols]) * (XHD ** -0.5)
        p_ref[a] = _softmax_rows(s_ref[a]).astype(BF16)
        for hd in range(XH):
            cols = slice(hd * XHD, (hd + 1) * XHD)
            o_ref[a, :, cols] = _dot(p_ref[a, hd], v_ref[:, cols]).astype(BF16)
        out = _dot(o_ref[a], wo_ref[...])
        xo_ref[rows, :] = x_ref[rows, :] + _rms(out, gpost_ref[...])


def _xattn_p_parts(l, tm, groups=1):
    nt = SEQ // tm
    gr = tm // groups
    row_spec = pl.BlockSpec((tm, D), lambda b, t: (b * nt + t, 0))
    kv_spec = pl.BlockSpec((None, N_MEM, D), lambda b, t: (l, b, 0))
    in_specs = [row_spec, _const((1, D)), _layer_const(l, (D, D)), kv_spec, kv_spec, _layer_const(l, (D, D)),
                _const((1, D))]
    scratch = [pltpu.VMEM((D, D), BF16), pltpu.VMEM((D, D), BF16), pltpu.VMEM((groups, gr, D), BF16),
               pltpu.VMEM((groups, XH, gr, N_MEM), F32), pltpu.VMEM((groups, XH, gr, N_MEM), BF16),
               pltpu.VMEM((groups, gr, D), BF16)]
    return nt, row_spec, in_specs, scratch


def _xattn_p(l, x, gpre, wq, kb, vb, wo, gpost):
    nt, row_spec, in_specs, scratch = _xattn_p_parts(l, 1024, groups=2)
    return pl.pallas_call(
        _xattn_p_body,
        grid=(NB, nt),
        in_specs=in_specs,
        out_specs=row_spec,
        out_shape=jax.ShapeDtypeStruct((NB * SEQ, D), F32),
        scratch_shapes=scratch,
        compiler_params=_params(("arbitrary", "arbitrary"), 48),
        name="xattn_p",
    )(x, gpre, wq, kb, vb, wo, gpost)


def _head_view(c):
    s = c.shape
    return c.reshape(s[0], s[1], N_MEM, XH, 2, 128).transpose(0, 1, 2, 4, 3, 5).reshape(s[0], s[1], N_MEM * 8, 128)


def _head_rows(ref, b, hd):
    halves = [ref[b, pl.ds(j * XH + hd, N_MEM, stride=8), :] for j in range(2)]
    return jnp.concatenate(halves, axis=1).astype(BF16)


def _xattn_s_body(x_ref, gpre_ref, wq_ref, k_ref, v_ref, wo_ref, gpost_ref, xo_ref, q_ref, o_ref, s_ref, *, bb):
    _xattn_s_step(pl.program_id(0), pl.num_programs(0), x_ref, gpre_ref, wq_ref, k_ref, v_ref, wo_ref, gpost_ref,
                  xo_ref, q_ref, o_ref, s_ref, bb=bb)


def _xattn_s_step(step, nsteps, x_ref, gpre_ref, wq_ref, k_ref, v_ref, wo_ref, gpost_ref, xo_ref, q_ref, o_ref, s_ref,
                  *, bb):
    @pl.when(step == 0)
    def _():
        h = _rms(x_ref[...], gpre_ref[...]).astype(BF16)
        q_ref[...] = _dot(h, wq_ref[...].astype(BF16))

    first = (lax.broadcasted_iota(jnp.int32, (8, XHD), 0) >> 2) == 0
    groups = [(pi, hd, bi) for pi in range(bb // 2) for hd in range(XH) for bi in range(2)]
    row0 = [pl.multiple_of((step * (bb // 2) + pi) * 8, 8) for pi in range(bb // 2)]
    for gi, (pi, hd, bi) in enumerate(groups):
        q8 = q_ref[pl.ds(row0[pi], 8), hd * XHD:(hd + 1) * XHD].astype(BF16)
        kh = _head_rows(k_ref, pi * 2 + bi, hd)
        s_ref[gi * 8:(gi + 1) * 8, :] = _dot_nt(q8, kh) * (XHD ** -0.5)
    s_ref[...] = _softmax_rows(s_ref[...])
    for gi, (pi, hd, bi) in enumerate(groups):
        if bi == 1:
            continue
        pv = [_dot(s_ref[(gi + b) * 8:(gi + b + 1) * 8, :].astype(BF16), _head_rows(v_ref, pi * 2 + b, hd))
              for b in range(2)]
        o_ref[pl.ds(row0[pi], 8), hd * XHD:(hd + 1) * XHD] = jnp.where(first, pv[0], pv[1])

    @pl.when(step == nsteps - 1)
    def _():
        out = _dot(o_ref[...].astype(BF16), wo_ref[...].astype(BF16))
        xo_ref[...] = x_ref[...] + _rms(out, gpost_ref[...])


def _xattn_ps_body(x_ref, gpre_ref, wqf_ref, k_ref, v_ref, wof_ref, gpost_ref,
                   xs_ref, gpres_ref, ks_ref, vs_ref, gposts_ref,
                   xo_ref, xso_ref,
                   wq_ref, wo_ref, q_ref, s_ref, p_ref, o_ref, qs_ref, os_ref, ss_ref, *, bb):
    _xattn_p_body(x_ref, gpre_ref, wqf_ref, k_ref, v_ref, wof_ref, gpost_ref, xo_ref, wq_ref, wo_ref, q_ref, s_ref,
                  p_ref, o_ref)
    step = pl.program_id(0) * pl.num_programs(1) + pl.program_id(1)
    _xattn_s_step(step, pl.num_programs(0) * pl.num_programs(1), xs_ref, gpres_ref, wq_ref, ks_ref, vs_ref, wo_ref,
                  gposts_ref, xso_ref, qs_ref, os_ref, ss_ref, bb=bb)


def _xattn_ps(l, x, gpre, wq, kb, vb, wo, gpost, xs, gpres, ks, vs, gposts):
    bb = 2
    steps = NS // bb
    tm = NB * SEQ // steps
    nt, row_spec, h_in, h_scratch = _xattn_p_parts(l, tm)
    kv_spec = pl.BlockSpec((None, bb, N_MEM * 8, 128), lambda b, t: (l, b * nt + t, 0, 0))
    return pl.pallas_call(
        functools.partial(_xattn_ps_body, bb=bb),
        grid=(NB, nt),
        in_specs=h_in + [_const((RS, D)), _const((1, D)), kv_spec, kv_spec, _const((1, D))],
        out_specs=[row_spec, _full((RS, D))],
        out_shape=[jax.ShapeDtypeStruct((NB * SEQ, D), F32), jax.ShapeDtypeStruct((RS, D), F32)],
        scratch_shapes=h_scratch + [pltpu.VMEM((RS, D), F32), pltpu.VMEM((RS, D), F32),
                                    pltpu.VMEM((bb * XH * 8, N_MEM), F32)],
        compiler_params=_params(("arbitrary", "arbitrary"), 56),
        name="xattn_ps",
    )(x, gpre, wq, kb, vb, wo, gpost, xs, gpres, ks, vs, gposts)


def _xattn_s(l, x, gpre, wq, k, v, wo, gpost):
    bb = 4
    kv_spec = pl.BlockSpec((None, bb, N_MEM * 8, 128), lambda i: (l, i, 0, 0))
    return pl.pallas_call(
        functools.partial(_xattn_s_body, bb=bb),
        grid=(NS // bb,),
        in_specs=[_const((RS, D)), _const((1, D)), _layer_const(l, (D, D)), kv_spec, kv_spec, _layer_const(l, (D, D)),
                  _const((1, D))],
        out_specs=_full((RS, D)),
        out_shape=jax.ShapeDtypeStruct((RS, D), F32),
        scratch_shapes=[pltpu.VMEM((RS, D), F32), pltpu.VMEM((RS, D), F32), pltpu.VMEM((bb * XH * 8, N_MEM), F32)],
        compiler_params=_params(("arbitrary",), 48),
        name="xattn_s",
    )(x, gpre, wq, k, v, wo, gpost)


def _ffn_p_body(x_ref, gpre_ref, wg_ref, wv_ref, cw_ref, wdn_ref, gpost_ref, xo_ref, st_ref, carry_ref, h_ref, act_ref,
                *sh_refs, tm):
    @pl.when(pl.program_id(1) == 0)
    def _():
        carry_ref[...] = jnp.zeros_like(carry_ref)

    ng, gr = h_ref.shape[0], h_ref.shape[1]
    nsh = len(sh_refs) // ng
    for a in range(ng):
        rows = slice(a * gr, (a + 1) * gr)
        h_ref[a] = _rms(x_ref[rows, :], gpre_ref[...]).astype(BF16)
        for c in range(NFC):
            halves = []
            for i, (base, w_ref) in enumerate(((0, wg_ref), (DFF, wv_ref))):
                cols = slice(base + c * FC, base + (c + 1) * FC)
                up = _dot(h_ref[a], w_ref[:, c * FC:(c + 1) * FC])
                halves.append(_conv_rows(up, carry_ref, cw_ref, cols, sh_refs[a * nsh + (2 * c + i) % nsh]))
                carry_ref[:, cols] = up[gr - 8:gr, :]
                if a == ng - 1:
                    st_ref[:, cols] = up[gr - 8:gr, :]
            act_ref[a, :, c * FC:(c + 1) * FC] = (_gelu(halves[0]) * halves[1]).astype(BF16)
        out = _dot(act_ref[a], wdn_ref[...])
        xo_ref[rows, :] = x_ref[rows, :] + _rms(out, gpost_ref[...])


def _ffn_p_parts(l, tm, groups=1):
    nt = SEQ // tm
    gr = tm // groups
    row_spec = pl.BlockSpec((tm, D), lambda b, t: (b * nt + t, 0))
    half = lambda i: pl.BlockSpec((D, DFF), lambda *_: (0, i), pipeline_mode=pl.Buffered(1))
    in_specs = [row_spec, _const((1, D)), half(0), half(1), _layer_const(l, (3, 2 * DFF)),
                _const((DFF, D)), _const((1, D))]
    out_specs = [row_spec, pl.BlockSpec((None, 8, 2 * DFF), lambda b, t: (b, 0, 0))]
    out_shape = [jax.ShapeDtypeStruct((NB * SEQ, D), F32), jax.ShapeDtypeStruct((NB, 8, 2 * DFF), F32)]
    scratch = ([pltpu.VMEM((8, 2 * DFF), F32), pltpu.VMEM((groups, gr, D), BF16),
                pltpu.VMEM((groups, gr, DFF), BF16)]
               + [pltpu.VMEM((FC // 128, gr + 8, 128), F32)] * (4 * groups))
    return nt, in_specs, out_specs, out_shape, scratch


def _ffn_p(l, x, gpre, wg, wv, cw, wdn, gpost):
    tm = 1024
    nt, in_specs, out_specs, out_shape, scratch = _ffn_p_parts(l, tm, groups=2)
    return pl.pallas_call(
        functools.partial(_ffn_p_body, tm=tm),
        grid=(NB, nt),
        in_specs=in_specs,
        out_specs=out_specs,
        out_shape=out_shape,
        scratch_shapes=scratch,
        compiler_params=_params(("arbitrary", "arbitrary"), 52),
        name="ffn_p",
    )(x, gpre, wg, wv, cw, wdn, gpost)


def _load_time_major(x_ref, pan_ref):
    n = x_ref.shape[1] // 128
    for p in range(n):
        pan_ref[p] = x_ref[:, p * 128:(p + 1) * 128]
    return jnp.concatenate(
        [jnp.concatenate([pan_ref[p, pl.ds(t, NS, stride=TS), :] for p in range(n)], axis=1) for t in range(TS)],
        axis=0)


def _store_batch_major(o_ref, y, pan_ref):
    n = o_ref.shape[1] // 128
    for p in range(n):
        for t in range(TS):
            pan_ref[p, pl.ds(t, NS, stride=TS), :] = y[t * NS:(t + 1) * NS, p * 128:(p + 1) * 128]
    for p in range(n):
        o_ref[:, p * 128:(p + 1) * 128] = pan_ref[p]


def _conv_time_major(u, prev, w_ref):
    blocks = list(prev) + [u[t * NS:(t + 1) * NS, :] for t in range(TS)]
    w0, w1, w2 = w_ref[0:1, :], w_ref[1:2, :], w_ref[2:3, :]
    return jnp.concatenate([(w0 * blocks[t] + w1 * blocks[t + 1]) + w2 * blocks[t + 2] for t in range(TS)], axis=0)


def _ffn_s_body(x_ref, gpre_ref, w_ref, cw_ref, st_ref, wdn_ref, gpost_ref,
                xo_ref, ns_ref, wb_ref, wdb_ref, xt_ref, h_ref, acc_ref, pan_ref, yg_ref):
    c, half = pl.program_id(0), pl.program_id(1)

    @pl.when((c == 0) & (half == 0))
    def _():
        xt = _load_time_major(x_ref, pan_ref)
        xt_ref[...] = xt
        h_ref[...] = _rms(xt, gpre_ref[...]).astype(BF16)
        acc_ref[...] = jnp.zeros_like(acc_ref)

    wb_ref[...] = w_ref[...].astype(BF16)
    u = _dot(h_ref[...], wb_ref[...])
    y = _conv_time_major(u, [st_ref[:, r, :] for r in range(2)], cw_ref)
    for r in range(2):
        ns_ref[:, r, :] = u[(TS - 2 + r) * NS:(TS - 1 + r) * NS, :]

    @pl.when(half == 0)
    def _():
        yg_ref[...] = y

    @pl.when(half == 1)
    def _():
        wdb_ref[...] = wdn_ref[...].astype(BF16)
        act = (_gelu(yg_ref[...]) * y).astype(BF16)
        acc_ref[...] += _dot(act, wdb_ref[...])

    @pl.when((c == pl.num_programs(0) - 1) & (half == 1))
    def _():
        _store_batch_major(xo_ref, xt_ref[...] + _rms(acc_ref[...], gpost_ref[...]), pan_ref)


def _ffn_s(l, x, gpre, wup, cw, st, wdn, gpost):
    col = lambda c, h: h * NFC + c
    return pl.pallas_call(
        _ffn_s_body,
        grid=(NFC, 2),
        in_specs=[_const((RS, D)), _const((1, D)),
                  pl.BlockSpec((None, D, FC), lambda c, h: (l, 0, col(c, h))),
                  pl.BlockSpec((None, 3, FC), lambda c, h: (l, 0, col(c, h))),
                  pl.BlockSpec((None, NS, 2, FC), lambda c, h: (l, 0, 0, col(c, h))),
                  pl.BlockSpec((None, FC, D), lambda c, h: (l, c, 0)), _const((1, D))],
        out_specs=[_full((RS, D)), pl.BlockSpec((NS, 2, FC), lambda c, h: (0, 0, col(c, h))),
                   pl.BlockSpec((D, FC), lambda c, h: (0, col(c, h))), pl.BlockSpec((FC, D), lambda c, h: (c, 0))],
        out_shape=[jax.ShapeDtypeStruct((RS, D), F32), jax.ShapeDtypeStruct((NS, 2, 2 * DFF), F32),
                   jax.ShapeDtypeStruct((D, 2 * DFF), BF16), jax.ShapeDtypeStruct((DFF, D), BF16)],
        scratch_shapes=[pltpu.VMEM((RS, D), F32), pltpu.VMEM((RS, D), BF16), pltpu.VMEM((RS, D), F32),
                        pltpu.VMEM((D // 128, RS, 128), F32), pltpu.VMEM((RS, FC), F32)],
        compiler_params=_params(("arbitrary", "arbitrary"), 40),
        name="ffn_s",
    )(x, gpre, wup, cw, st, wdn, gpost)


def _rope_head(z, cos, sin, base):
    x1 = z[:, base:base + 128]
    x2 = z[:, base + 128:base + 256]
    return x1 * cos - x2 * sin, x1 * sin + x2 * cos


def _group_norm_gate(o, g):
    mu = jnp.mean(o, axis=-1, keepdims=True)
    d = o - mu
    var = jnp.mean(d * d, axis=-1, keepdims=True)
    return (g * jax.nn.sigmoid(g)) * (d * lax.rsqrt(var + EPS))


RCH = 256


def _ret_p_body(x_ref, cos_ref, sin_ref, gpre_ref, wq_ref, wk_ref, wv_ref, wg_ref, wout_ref, gpost_ref, xo_ref, s_ref,
                h_ref, q_ref, k_ref, kd_ref, v_ref, g_ref, y_ref, dm_ref, qd_ref, kdec_ref, *, tm):
    first = (pl.program_id(0) == 0) & (pl.program_id(1) == 0)

    @pl.when(first)
    def _():
        i = lax.broadcasted_iota(jnp.int32, (RCH, RCH), 0).astype(F32)
        j = lax.broadcasted_iota(jnp.int32, (RCH, RCH), 1).astype(F32)
        i1 = lax.broadcasted_iota(jnp.int32, (RCH, 128), 0).astype(F32)
        for hd in range(RET_H):
            lg = _LOG_G[hd]
            dm_ref[hd] = jnp.where(i >= j, jnp.exp(lg * jnp.maximum(i - j, 0.0)), 0.0)
            qd_ref[hd] = jnp.exp(lg * (i1 + 1.0))
            kdec_ref[hd] = jnp.exp(lg * (RCH - 1.0 - i1))

    @pl.when(pl.program_id(1) == 0)
    def _():
        s_ref[...] = jnp.zeros_like(s_ref)

    h_ref[...] = _rms(x_ref[...], gpre_ref[...]).astype(BF16)
    cos = cos_ref[...]
    sin = sin_ref[...]
    nchunk = tm // RCH
    for hd in range(RET_H):
        base = hd * RET_DK
        zq = _dot(h_ref[...], wq_ref[:, base:base + RET_DK])
        zk = _dot(h_ref[...], wk_ref[:, base:base + RET_DK])
        q1, q2 = _rope_head(zq, cos, sin, 0)
        k1, k2 = _rope_head(zk, cos, sin, 0)
        q_ref[:, base:base + 128] = (q1 * (RET_DK ** -0.5)).astype(BF16)
        q_ref[:, base + 128:base + 256] = (q2 * (RET_DK ** -0.5)).astype(BF16)
        k_ref[:, base:base + 128] = k1.astype(BF16)
        k_ref[:, base + 128:base + 256] = k2.astype(BF16)
        kdec = jnp.concatenate([kdec_ref[hd]] * nchunk, axis=0)
        kd_ref[:, base:base + 128] = k1 * kdec
        kd_ref[:, base + 128:base + 256] = k2 * kdec
        vcols = slice(hd * RET_DV, (hd + 1) * RET_DV)
        v_ref[:, vcols] = _dot(h_ref[...], wv_ref[:, vcols]).astype(BF16)
        g_ref[:, vcols] = _dot(h_ref[...], wg_ref[:, vcols])
    for hd in range(RET_H):
        kcols = slice(hd * RET_DK, (hd + 1) * RET_DK)
        vcols = slice(hd * RET_DV, (hd + 1) * RET_DV)
        decay_l = float(np.exp(np.float32(_LOG_G[hd]) * np.float32(RCH)))
        qd = qd_ref[hd]
        qd4 = jnp.concatenate([qd, qd, qd, qd], axis=1)
        for c in range(nchunk):
            rows = slice(c * RCH, (c + 1) * RCH)
            qc = q_ref[rows, kcols]
            vc = v_ref[rows, vcols]
            s_prev = s_ref[hd]
            sc = (_dot_nt(qc, k_ref[rows, kcols]) * dm_ref[hd]).astype(BF16)
            o = _dot(sc, vc) + _dot(qc, s_prev.astype(BF16)) * qd4
            kdt = kd_ref[rows, kcols].T.astype(BF16)
            s_ref[hd] = decay_l * s_prev + _dot(kdt, vc)
            y_ref[rows, vcols] = _group_norm_gate(o, g_ref[rows, vcols]).astype(BF16)
    out = _dot(y_ref[...], wout_ref[...])
    xo_ref[...] = x_ref[...] + _rms(out, gpost_ref[...])


def _ret_p(x, cos, sin, gpre, wq, wk, wv, wg, wout, gpost):
    tm = 512
    nt = SEQ // tm
    row_spec = pl.BlockSpec((tm, D), lambda b, t: (b * nt + t, 0))
    cs_spec = pl.BlockSpec((tm, 128), lambda b, t: (t, 0))
    return pl.pallas_call(
        functools.partial(_ret_p_body, tm=tm),
        grid=(NB, nt),
        in_specs=[row_spec, cs_spec, cs_spec, _const((1, D)), _const((D, NQ)), _const((D, NQ)), _const((D, NV)),
                  _const((D, NV)), _const((NV, D)), _const((1, D))],
        out_specs=[row_spec, pl.BlockSpec((None, RET_H, RET_DK, RET_DV), lambda b, t: (b, 0, 0, 0))],
        out_shape=[jax.ShapeDtypeStruct((NB * SEQ, D), F32), jax.ShapeDtypeStruct((NB, RET_H, RET_DK, RET_DV), F32)],
        scratch_shapes=[pltpu.VMEM((tm, D), BF16), pltpu.VMEM((tm, NQ), BF16), pltpu.VMEM((tm, NQ), BF16),
                        pltpu.VMEM((tm, NQ), F32), pltpu.VMEM((tm, NV), BF16), pltpu.VMEM((tm, NV), F32),
                        pltpu.VMEM((tm, NV), BF16), pltpu.VMEM((RET_H, RCH, RCH), F32),
                        pltpu.VMEM((RET_H, RCH, 128), F32), pltpu.VMEM((RET_H, RCH, 128), F32)],
        compiler_params=_params(("arbitrary", "arbitrary"), 56),
        name="ret_p",
    )(x, cos, sin, gpre, wq, wk, wv, wg, wout, gpost)


def _ret_s_proj_body(x_ref, cos_ref, sin_ref, gpre_ref, wqf_ref, wkf_ref, wvf_ref, wgf_ref,
                     q_ref, kdt_ref, v_ref, g_ref, oi_ref, wq_ref, wk_ref, wv_ref, wg_ref, h_ref):
    hd = pl.program_id(0)

    @pl.when(hd == 0)
    def _():
        h_ref[...] = _rms(x_ref[...], gpre_ref[...]).astype(BF16)

    for wf_ref, wb_ref in ((wqf_ref, wq_ref), (wkf_ref, wk_ref), (wvf_ref, wv_ref), (wgf_ref, wg_ref)):
        wb_ref[...] = wf_ref[...].astype(BF16)
    lg = jnp.where(hd == 0, _LOG_G[0], jnp.where(hd == 1, _LOG_G[1], jnp.where(hd == 2, _LOG_G[2], _LOG_G[3])))
    h = h_ref[...]
    cos = cos_ref[...]
    sin = sin_ref[...]
    ti = (lax.broadcasted_iota(jnp.int32, (RS, 128), 0) & (TS - 1)).astype(F32)
    i = lax.broadcasted_iota(jnp.int32, (CHUNK, CHUNK), 0)
    j = lax.broadcasted_iota(jnp.int32, (CHUNK, CHUNK), 1)
    same = ((i >> 2) == (j >> 2)) & (i >= j)
    dm = jnp.where(same, jnp.exp(lg * jnp.maximum(i - j, 0).astype(F32)), 0.0)
    q1, q2 = _rope_head(_dot(h, wq_ref[...]), cos, sin, 0)
    k1, k2 = _rope_head(_dot(h, wk_ref[...]), cos, sin, 0)
    q = jnp.concatenate([q1, q2], axis=1) * (RET_DK ** -0.5)
    k = jnp.concatenate([k1, k2], axis=1)
    q_ref[...] = q
    kdec = jnp.exp(lg * (TS - 1.0 - ti))
    kd = k * jnp.concatenate([kdec, kdec], axis=1)
    vf = _dot(h, wv_ref[...])
    v_ref[...] = vf
    v = vf.astype(BF16)
    g_ref[...] = _dot(h, wg_ref[...])
    qb = q.astype(BF16)
    kb = k.astype(BF16)
    for tl in range(RS // CHUNK):
        rows = slice(tl * CHUNK, (tl + 1) * CHUNK)
        sc = (_dot_nt(qb[rows], kb[rows]) * dm).astype(BF16)
        oi_ref[rows, :] = _dot(sc, v[rows])
        kdt_ref[tl] = kd[rows].T.astype(BF16)


def _ret_s_proj(x, cos, sin, gpre, win):
    def wcol(width, first_block):
        return pl.BlockSpec((None, D, width), lambda h: (0, 0, first_block + h))
    def hcol(rows, width):
        return pl.BlockSpec((rows, width), lambda h: (0, h))
    return pl.pallas_call(
        _ret_s_proj_body,
        grid=(RET_H,),
        in_specs=[_const((RS, D)), _const((RS, 128)), _const((RS, 128)), _const((1, D)),
                  wcol(RET_DK, 0), wcol(RET_DK, NQ // RET_DK), wcol(RET_DV, 2 * NQ // RET_DV),
                  wcol(RET_DV, (2 * NQ + NV) // RET_DV)],
        out_specs=[hcol(RS, RET_DK),
                   pl.BlockSpec((RS // CHUNK, None, RET_DK, CHUNK), lambda h: (0, h, 0, 0)),
                   hcol(RS, RET_DV), hcol(RS, RET_DV), hcol(RS, RET_DV),
                   hcol(D, RET_DK), hcol(D, RET_DK), hcol(D, RET_DV), hcol(D, RET_DV)],
        out_shape=[jax.ShapeDtypeStruct((RS, NQ), F32),
                   jax.ShapeDtypeStruct((RS // CHUNK, RET_H, RET_DK, CHUNK), BF16),
                   jax.ShapeDtypeStruct((RS, NV), F32), jax.ShapeDtypeStruct((RS, NV), F32),
                   jax.ShapeDtypeStruct((RS, NV), F32),
                   jax.ShapeDtypeStruct((D, NQ), BF16), jax.ShapeDtypeStruct((D, NQ), BF16),
                   jax.ShapeDtypeStruct((D, NV), BF16), jax.ShapeDtypeStruct((D, NV), BF16)],
        scratch_shapes=[pltpu.VMEM((RS, D), BF16)],
        compiler_params=_params(("arbitrary",), 40),
        name="ret_s_proj",
    )(x, cos, sin, gpre, win, win, win, win)


def _ret_s_core_body(step, q_ref, kdt_ref, v_ref, oi_ref, s_ref, o_ref, so_ref):
    pair_in_tile = step % (CHUNK // 8)
    row8 = lax.broadcasted_iota(jnp.int32, (8, RET_DK), 0)
    row128 = lax.broadcasted_iota(jnp.int32, (CHUNK, RET_DV), 0)
    t8 = (lax.broadcasted_iota(jnp.int32, (8, RET_DV), 0) & (TS - 1)).astype(F32)
    for hd in range(RET_H):
        lg = _LOG_G[hd]
        decay_l = float(np.exp(np.float32(lg) * np.float32(TS)))
        q8 = q_ref[:, hd * RET_DK:(hd + 1) * RET_DK]
        v128 = v_ref[:, hd * RET_DV:(hd + 1) * RET_DV]
        kdt = kdt_ref[hd]
        inter = jnp.zeros((8, RET_DV), F32)
        for bi in range(2):
            s_prev = s_ref[bi, hd]
            qm = jnp.where((row8 >> 2) == bi, q8, 0.0).astype(BF16)
            inter = inter + _dot(qm, s_prev.astype(BF16))
            vm = jnp.where((row128 >> 2) == pair_in_tile * 2 + bi, v128, 0.0).astype(BF16)
            so_ref[bi, hd] = decay_l * s_prev + _dot(kdt, vm)
        cols = slice(hd * RET_DV, (hd + 1) * RET_DV)
        o_ref[:, cols] = oi_ref[:, cols] + inter * jnp.exp(lg * (t8 + 1.0))


def _ret_s_core_parts():
    ppt = CHUNK // 8
    s_spec = pl.BlockSpec((2, RET_H, RET_DK, RET_DV), lambda i: (i, 0, 0, 0))
    in_specs = [pl.BlockSpec((8, NQ), lambda i: (i, 0)),
                pl.BlockSpec((None, RET_H, RET_DK, CHUNK), lambda i: (i // ppt, 0, 0, 0)),
                pl.BlockSpec((CHUNK, NV), lambda i: (i // ppt, 0)),
                pl.BlockSpec((8, NV), lambda i: (i, 0)),
                s_spec]
    out_specs = [pl.BlockSpec((8, NV), lambda i: (i, 0)), s_spec]
    out_shape = [jax.ShapeDtypeStruct((RS, NV), F32), jax.ShapeDtypeStruct((NS, RET_H, RET_DK, RET_DV), F32)]
    return in_specs, out_specs, out_shape


def _host_ret_s_body(*refs, host_body, n_in, n_out):
    host_in, guest_in = refs[:n_in], refs[n_in:n_in + 5]
    k = n_in + 5
    host_out, guest_out, scratch = refs[k:k + n_out], refs[k + n_out:k + n_out + 2], refs[k + n_out + 2:]
    host_body(*host_in, *host_out, *scratch)
    step = pl.program_id(0) * pl.num_programs(1) + pl.program_id(1)
    _ret_s_core_body(step, *guest_in, *guest_out)


def _with_ret_s(name, host_body, nt, h_in, h_out, h_shape, scratch, host_args, guest_args):
    g_in, g_out, g_shape = _ret_s_core_parts()

    def on_grid(spec):
        return pl.BlockSpec(spec.block_shape, lambda b, t, m=spec.index_map: m(b * nt + t))

    return pl.pallas_call(
        functools.partial(_host_ret_s_body, host_body=host_body, n_in=len(h_in), n_out=len(h_out)),
        grid=(NB, nt),
        in_specs=h_in + [on_grid(sp) for sp in g_in],
        out_specs=h_out + [on_grid(sp) for sp in g_out],
        out_shape=h_shape + g_shape,
        scratch_shapes=scratch,
        compiler_params=_params(("arbitrary", "arbitrary"), 56),
        name=name,
    )(*host_args, *guest_args)


def _ret_s_out_body(x_ref, o_ref, g_ref, woutf_ref, gpost_ref, xo_ref, wout_ref, y_ref):
    wout_ref[...] = woutf_ref[...].astype(BF16)
    for hd in range(RET_H):
        cols = slice(hd * RET_DV, (hd + 1) * RET_DV)
        y_ref[:, cols] = _group_norm_gate(o_ref[:, cols], g_ref[:, cols]).astype(BF16)
    out = _dot(y_ref[...], wout_ref[...])
    xo_ref[...] = x_ref[...] + _rms(out, gpost_ref[...])


def _ret_s_out(x, o, g, wout, gpost):
    return pl.pallas_call(
        _ret_s_out_body,
        grid=(1,),
        in_specs=[_const((RS, D)), _const((RS, NV)), _const((RS, NV)), _layer_const(0, (NV, D)), _const((1, D))],
        out_specs=[_full((RS, D)), _full((NV, D))],
        out_shape=[jax.ShapeDtypeStruct((RS, D), F32), jax.ShapeDtypeStruct((NV, D), BF16)],
        scratch_shapes=[pltpu.VMEM((RS, NV), BF16)],
        compiler_params=_params(("arbitrary",), 48),
        name="ret_s_out",
    )(x, o, g, wout, gpost)


def _rope_tables(pos):
    half = RET_DK // 2
    inv = ROPE_BASE ** (-jnp.arange(half, dtype=F32) / half)
    ang = pos.astype(F32)[:, None] * inv[None, :]
    return jnp.cos(ang), jnp.sin(ang)


def _state_rows(st):
    lead, c = st.shape[:-3], st.shape[-1]
    n = len(lead)
    s = st.reshape(lead + (NS, 2, c // 128, 128))
    s = s.transpose(tuple(range(n)) + (n + 2, n + 1, n, n + 3))
    return s.reshape(lead + (c // 128 * 2, NS, 128))


def _from_state_rows(s):
    tiles = s.shape[0] // 2
    return s.reshape(tiles, 2, NS, 128).transpose(2, 1, 0, 3).reshape(NS, 2, tiles * 128)


def kernel(x_prompt, x_sample, mem_prompt, cache_mem_k, cache_mem_v, state_shortconv, state_retention, state_ffn_conv, norm_mix_pre, norm_mix_post, w_in_even, sgu_vnorm, sgu_w, sgu_b, conv_short, w_out_even, w_in_odd, w_out_odd, norm_x_pre, norm_x_post, norm_mem, w_xq, w_xk, w_xv, w_xo, norm_ffn_pre, norm_ffn_post, w_ffn_up, conv_ffn, w_ffn_down):
    row = lambda g: g.reshape(1, -1)

    xs = x_sample.reshape(RS, D)
    w4 = jnp.tril(sgu_w[0][:, :TS, :TS])
    gw = jnp.repeat(w4.transpose(1, 2, 0), CHUNK, axis=2)
    gb = jnp.repeat(sgu_b[0][:, :TS].T, CHUNK, axis=1)
    xs, sc_s, v_s = _even_s(xs, row(norm_mix_pre[0]), w_in_even, row(sgu_vnorm[0]), gw, gb, conv_short,
                            _state_rows(state_shortconv[0]), w_out_even, row(norm_mix_post[0]))
    cos_s, sin_s = _rope_tables(PAST + (jnp.arange(RS, dtype=jnp.int32) % TS))
    cache_k, cache_v = _head_view(cache_mem_k), _head_view(cache_mem_v)
    ffn_s_states, ffn_p_states, ffn_w = [], [], []

    def sample_ffn(l, xs):
        xs, ns, wup_b, wd_b = _ffn_s(l, xs, row(norm_ffn_pre[l]), w_ffn_up, conv_ffn, state_ffn_conv, w_ffn_down,
                                     row(norm_ffn_post[l]))
        ffn_s_states.append(ns)
        ffn_w.append((wup_b, wup_b, wd_b))
        return xs

    def prompt_ffn(l, xp):
        xp, st = _ffn_p(l, xp, row(norm_ffn_pre[l]), ffn_w[l][0], ffn_w[l][1], conv_ffn, ffn_w[l][2],
                        row(norm_ffn_post[l]))
        ffn_p_states.append(st[:, 6:8, :])
        return xp

    xs = _xattn_s(0, xs, row(norm_x_pre[0]), w_xq, cache_k, cache_v, w_xo, row(norm_x_post[0]))
    xs = sample_ffn(0, xs)
    q, kdt, v, g, oi, wq_o, wk_o, wv_o, wg_o = _ret_s_proj(xs, cos_s, sin_s, row(norm_mix_pre[1]), w_in_odd)

    mem_k, mem_v, mem_kb, mem_vb = _memkv(mem_prompt.reshape(NB * N_MEM, D), norm_mem.reshape(2, 1, D), w_xk, w_xv)
    xp = x_prompt.reshape(NB * SEQ, D)
    sb_full = jnp.repeat(sgu_b[0].T, CHUNK, axis=1)
    tm = NB * SEQ // (NS // 2)
    nt, e_in, e_out, e_shape, e_scratch = _even_p_parts(tm)
    xp, sc_p, o, ret_s_state = _with_ret_s(
        "even_p_ret_s", functools.partial(_even_p_body, tm=tm), nt, e_in, e_out, e_shape, e_scratch,
        (xp, row(norm_mix_pre[0]), w_in_even, row(sgu_vnorm[0]), sgu_w, sb_full, conv_short, w_out_even,
         row(norm_mix_post[0])), (q, kdt, v, oi, state_retention[0]))
    xs, wout_o = _ret_s_out(xs, o, g, w_out_odd, row(norm_mix_post[1]))

    xp = _xattn_p(0, xp, row(norm_x_pre[0]), w_xq, mem_kb, mem_vb, w_xo, row(norm_x_post[0]))
    xp = prompt_ffn(0, xp)
    cos_p, sin_p = _rope_tables(jnp.arange(SEQ, dtype=jnp.int32))
    xp, ret_p_state = _ret_p(xp, cos_p, sin_p, row(norm_mix_pre[1]), wq_o, wk_o, wv_o, wg_o, wout_o,
                             row(norm_mix_post[1]))
    xp, xs = _xattn_ps(1, xp, row(norm_x_pre[1]), w_xq, mem_kb, mem_vb, w_xo, row(norm_x_post[1]),
                       xs, row(norm_x_pre[1]), cache_k, cache_v, row(norm_x_post[1]))
    xs = sample_ffn(1, xs)
    xp = prompt_ffn(1, xp)

    return (xp.reshape(NB, SEQ, D), xs.reshape(NS, TS, D),
            _from_head_view(mem_k), _from_head_view(mem_v),
            sc_p[None, :, 6:8, :], _from_state_rows(sc_s)[None],
            v_s.reshape(1, NS, TS, SGU_W),
            ret_p_state[None], ret_s_state[None],
            jnp.stack(ffn_p_states), jnp.stack(ffn_s_states))
```

```python
import functools

import numpy as np
import jax
import jax.numpy as jnp
from jax import lax
from jax.experimental import pallas as pl
from jax.experimental.pallas import tpu as pltpu

F32 = jnp.float32
BF16 = jnp.bfloat16

D = 1024
SEQ = 2048
NB = 8
NS = 128
TS = 4
RS = NS * TS
PAST = 16384
CHUNK = 128
EPS = 1e-6
SGU_W = 512
SC_W = 512
RET_H = 4
RET_DK = 256
RET_DV = 512
NQ = RET_H * RET_DK
NV = RET_H * RET_DV
N_MEM = 256
XH = 4
XHD = 256
DFF = 2816
FC = 256
NFC = DFF // FC
ROPE_BASE = 10000.0

_LOG_G = [float(v) for v in np.log1p(-np.exp2(np.float32(-5.0) - np.arange(RET_H, dtype=np.float32))).astype(np.float32)]

_MIB = 1024 * 1024


def _rms(x, g):
    ms = jnp.mean(x * x, axis=-1, keepdims=True)
    return (x * lax.rsqrt(ms + EPS)) * g


_GELU_C = 0.7978845608028654
_LOG2E = 1.4426950408889634


def _gelu(x):
    k0 = -2.0 * _LOG2E * _GELU_C
    k1 = k0 * 0.044715
    return x * (1.0 / (1.0 + jnp.exp2(x * (k0 + k1 * (x * x)))))


def _dot(a, b):
    return jnp.dot(a, b, preferred_element_type=F32)


def _dot_nt(a, b):
    return lax.dot_general(a, b, (((1,), (1,)), ((), ())), preferred_element_type=F32)


def _const(shape):
    n = len(shape)
    return pl.BlockSpec(shape, lambda *_: (0,) * n, pipeline_mode=pl.Buffered(1))


def _layer_const(l, shape):
    n = len(shape)
    return pl.BlockSpec((None,) + tuple(shape), lambda *_: (l,) + (0,) * n, pipeline_mode=pl.Buffered(1))


def _full(shape):
    n = len(shape)
    return pl.BlockSpec(shape, lambda *_: (0,) * n)


def _params(sem, vmem_mib):
    return pltpu.CompilerParams(dimension_semantics=sem, vmem_limit_bytes=vmem_mib * _MIB)


def _conv_rows(u, prev_ref, w_ref, cols, sh_ref):
    r, c = u.shape
    ys = []
    for j in range(c // 128):
        cj = slice(cols.start + j * 128, cols.start + (j + 1) * 128)
        uj = u[:, j * 128:(j + 1) * 128]
        sh_ref[j, 0:8, :] = prev_ref[:, cj]
        sh_ref[j, 8:r + 8, :] = uj
        u1 = sh_ref[j, 7:r + 7, :]
        u2 = sh_ref[j, 6:r + 6, :]
        ys.append((w_ref[0:1, cj] * u2 + w_ref[1:2, cj] * u1) + w_ref[2:3, cj] * uj)
    return jnp.concatenate(ys, axis=1)


def _memkv_body(mem_ref, g_ref, wk_ref, wv_ref, k_ref, v_ref, kb_ref, vb_ref, *, nb):
    mn = _rms(mem_ref[...], g_ref[...]).astype(BF16)
    for w_ref, o_ref, ob_ref in ((wk_ref, k_ref, kb_ref), (wv_ref, v_ref, vb_ref)):
        y = _dot(mn, w_ref[...].astype(BF16))
        ob_ref[...] = y.astype(BF16)
        for b in range(nb):
            for hd in range(XH):
                for j in range(2):
                    c0 = hd * XHD + j * 128
                    o_ref[b, pl.ds(j * XH + hd, N_MEM, stride=8), :] = y[b * N_MEM:(b + 1) * N_MEM, c0:c0 + 128]


def _memkv(mem, g, wk, wv):
    rows = mem.shape[0]
    nb = 2
    tm = nb * N_MEM
    o_spec = pl.BlockSpec((None, nb, N_MEM * 8, 128), lambda l, i: (l, i, 0, 0))
    ob_spec = pl.BlockSpec((None, tm, D), lambda l, i: (l, i, 0))
    w_spec = pl.BlockSpec((None, D, D), lambda l, i: (l, 0, 0))
    return pl.pallas_call(
        functools.partial(_memkv_body, nb=nb),
        grid=(2, rows // tm),
        in_specs=[pl.BlockSpec((tm, D), lambda l, i: (i, 0)),
                  pl.BlockSpec((None, 1, D), lambda l, i: (l, 0, 0)),
                  w_spec, w_spec],
        out_specs=[o_spec, o_spec, ob_spec, ob_spec],
        out_shape=[jax.ShapeDtypeStruct((2, NB, N_MEM * 8, 128), F32)] * 2
        + [jax.ShapeDtypeStruct((2, rows, D), BF16)] * 2,
        compiler_params=_params(("arbitrary", "arbitrary"), 40),
        name="memkv",
    )(mem, g, wk, wv)


def _from_head_view(c):
    s = c.shape
    return c.reshape(s[0], s[1], N_MEM, 2, XH, 128).transpose(0, 1, 2, 4, 3, 5).reshape(s[0], s[1], N_MEM, XH, XHD)


def _even_front(x, gpre_ref, win, vn_ref):
    h = _rms(x, gpre_ref[...]).astype(BF16)
    u = _gelu(_dot(h, win(0, 512)))
    v = _rms(_gelu(_dot(h, win(512, 1024))), vn_ref[...])
    bg = _dot(h, win(1024, 1536))
    p = _dot(h, win(1536, 2048)) * _dot(h, win(2048, 2560))
    return u, v, bg, p


def _even_p_body(x_ref, gpre_ref, winf_ref, vn_ref, sw_ref, sb_ref, cw_ref, woutf_ref, gpost_ref,
                 xo_ref, st_ref, win_ref, wout_ref, carry_ref, cat_ref, sh_ref, *, tm):
    @pl.when((pl.program_id(0) == 0) & (pl.program_id(1) == 0))
    def _():
        win_ref[...] = winf_ref[...].astype(BF16)
        wout_ref[...] = woutf_ref[...].astype(BF16)

    @pl.when(pl.program_id(1) == 0)
    def _():
        carry_ref[...] = jnp.zeros_like(carry_ref)

    x = x_ref[...]
    u, v, bg, p = _even_front(x, gpre_ref, lambda c0, c1: win_ref[:, c0:c1], vn_ref)
    vb = v.astype(BF16)
    ri = lax.broadcasted_iota(jnp.int32, (CHUNK, CHUNK), 0)
    ci = lax.broadcasted_iota(jnp.int32, (CHUNK, CHUNK), 1)
    for g in range(4):
        cols = slice(g * 128, (g + 1) * 128)
        w = jnp.where(ri >= ci, sw_ref[g], 0.0).astype(BF16)
        for c in range(tm // CHUNK):
            rows = slice(c * CHUNK, (c + 1) * CHUNK)
            mixed = _dot(w, vb[rows, cols]) + sb_ref[:, cols]
            cat_ref[rows, cols] = (u[rows, cols] * mixed).astype(BF16)
    cz = _conv_rows(p, carry_ref, cw_ref, slice(0, SC_W), sh_ref)
    cat_ref[:, 512:1024] = (bg * cz).astype(BF16)
    carry_ref[...] = p[tm - 8:tm, :]
    st_ref[...] = p[tm - 8:tm, :]
    out = _dot(cat_ref[...], wout_ref[...])
    xo_ref[...] = x + _rms(out, gpost_ref[...])


def _even_p_parts(tm):
    nt = SEQ // tm
    row_spec = pl.BlockSpec((tm, D), lambda b, t: (b * nt + t, 0))
    in_specs = [row_spec, _const((1, D)), _layer_const(0, (D, 2560)), _const((1, SGU_W)),
                _layer_const(0, (4, CHUNK, CHUNK)), _const((CHUNK, SGU_W)), _layer_const(0, (3, SC_W)),
                _layer_const(0, (D, D)), _const((1, D))]
    out_specs = [row_spec, pl.BlockSpec((None, 8, SC_W), lambda b, t: (b, 0, 0))]
    out_shape = [jax.ShapeDtypeStruct((NB * SEQ, D), F32), jax.ShapeDtypeStruct((NB, 8, SC_W), F32)]
    scratch = [pltpu.VMEM((D, 2560), BF16), pltpu.VMEM((D, D), BF16), pltpu.VMEM((8, SC_W), F32),
               pltpu.VMEM((tm, D), BF16), pltpu.VMEM((SC_W // 128, tm + 8, 128), F32)]
    return nt, in_specs, out_specs, out_shape, scratch


def _even_s_body(x_ref, gpre_ref, win_ref, vn_ref, gw_ref, gb_ref, cw_ref, st_ref, wout_ref, gpost_ref,
                 xo_ref, ns_ref, v_ref, pan_ref):
    xt = _load_time_major(x_ref, pan_ref)
    u, v, bg, p = _even_front(xt, gpre_ref, lambda c0, c1: win_ref[:, c0:c1].astype(BF16), vn_ref)
    _store_batch_major(v_ref, v, pan_ref)
    vt = [v[t * NS:(t + 1) * NS, :] for t in range(TS)]
    mixed = []
    for t in range(TS):
        m = gb_ref[t:t + 1, :] + gw_ref[t, 0:1, :] * vt[0]
        for s in range(1, t + 1):
            m = m + gw_ref[t, s:s + 1, :] * vt[s]
        mixed.append(m)
    a = u * jnp.concatenate(mixed, axis=0)
    prev = [jnp.concatenate([st_ref[j * 2 + r] for j in range(SC_W // 128)], axis=1) for r in range(2)]
    cz = _conv_time_major(p, prev, cw_ref)
    for j in range(SC_W // 128):
        for r in range(2):
            ns_ref[j * 2 + r] = p[(TS - 2 + r) * NS:(TS - 1 + r) * NS, j * 128:(j + 1) * 128]
    cat = jnp.concatenate([a, bg * cz], axis=1).astype(BF16)
    out = _dot(cat, wout_ref[...].astype(BF16))
    _store_batch_major(xo_ref, xt + _rms(out, gpost_ref[...]), pan_ref)


def _even_s(x, gpre, win, vn, gw, gb, cw, st, wout, gpost):
    nst = SC_W // 128 * 2
    return pl.pallas_call(
        _even_s_body,
        grid=(1,),
        in_specs=[_const((RS, D)), _const((1, D)), _layer_const(0, (D, 2560)), _const((1, SGU_W)),
                  _const((TS, TS, SGU_W)), _const((TS, SGU_W)), _layer_const(0, (3, SC_W)), _const((nst, NS, 128)),
                  _layer_const(0, (D, D)), _const((1, D))],
        out_specs=[_full((RS, D)), _full((nst, NS, 128)), _full((RS, SGU_W))],
        out_shape=[jax.ShapeDtypeStruct((RS, D), F32), jax.ShapeDtypeStruct((nst, NS, 128), F32),
                   jax.ShapeDtypeStruct((RS, SGU_W), F32)],
        scratch_shapes=[pltpu.VMEM((D // 128, RS, 128), F32)],
        compiler_params=_params(("arbitrary",), 48),
        name="even_s",
    )(x, gpre, win, vn, gw, gb, cw, st, wout, gpost)


def _softmax_rows(s):
    m = jnp.max(s, axis=-1, keepdims=True)
    e = jnp.exp(s - m)
    return e * (1.0 / jnp.sum(e, axis=-1, keepdims=True))


def _xattn_p_body(x_ref, gpre_ref, wqf_ref, k_ref, v_ref, wof_ref, gpost_ref, xo_ref, wq_ref, wo_ref, q_ref, s_ref,
                  p_ref, o_ref):
    @pl.when((pl.program_id(0) == 0) & (pl.program_id(1) == 0))
    def _():
        wq_ref[...] = wqf_ref[...].astype(BF16)
        wo_ref[...] = wof_ref[...].astype(BF16)

    ng, gr = q_ref.shape[0], q_ref.shape[1]
    for a in range(ng):
        rows = slice(a * gr, (a + 1) * gr)
        h = _rms(x_ref[rows, :], gpre_ref[...]).astype(BF16)
        q_ref[a] = _dot(h, wq_ref[...]).astype(BF16)
        for hd in range(XH):
            cols = slice(hd * XHD, (hd + 1) * XHD)
            s_ref[a, hd] = _dot_nt(q_ref[a, :, cols], k_ref[:, cols]) * (XHD ** -0.5)
        p_ref[a] = _softmax_rows(s_ref[a]).astype(BF16)
        for hd in range(XH):
            cols = slice(hd * XHD, (hd + 1) * XHD)
            o_ref[a, :, cols] = _dot(p_ref[a, hd], v_ref[:, cols]).astype(BF16)
        out = _dot(o_ref[a], wo_ref[...])
        xo_ref[rows, :] = x_ref[rows, :] + _rms(out, gpost_ref[...])


def _xattn_p_parts(l, tm, groups=1):
    nt = SEQ // tm
    gr = tm // groups
    row_spec = pl.BlockSpec((tm, D), lambda b, t: (b * nt + t, 0))
    kv_spec = pl.BlockSpec((None, N_MEM, D), lambda b, t: (l, b, 0))
    in_specs = [row_spec, _const((1, D)), _layer_const(l, (D, D)), kv_spec, kv_spec, _layer_const(l, (D, D)),
                _const((1, D))]
    scratch = [pltpu.VMEM((D, D), BF16), pltpu.VMEM((D, D), BF16), pltpu.VMEM((groups, gr, D), BF16),
               pltpu.VMEM((groups, XH, gr, N_MEM), F32), pltpu.VMEM((groups, XH, gr, N_MEM), BF16),
               pltpu.VMEM((groups, gr, D), BF16)]
    return nt, row_spec, in_specs, scratch


def _xattn_p(l, x, gpre, wq, kb, vb, wo, gpost):
    nt, row_spec, in_specs, scratch = _xattn_p_parts(l, 1024, groups=2)
    return pl.pallas_call(
        _xattn_p_body,
        grid=(NB, nt),
        in_specs=in_specs,
        out_specs=row_spec,
        out_shape=jax.ShapeDtypeStruct((NB * SEQ, D), F32),
        scratch_shapes=scratch,
        compiler_params=_params(("arbitrary", "arbitrary"), 48),
        name="xattn_p",
    )(x, gpre, wq, kb, vb, wo, gpost)


def _head_view(c):
    s = c.shape
    return c.reshape(s[0], s[1], N_MEM, XH, 2, 128).transpose(0, 1, 2, 4, 3, 5).reshape(s[0], s[1], N_MEM * 8, 128)


def _head_rows(ref, b, hd):
    halves = [ref[b, pl.ds(j * XH + hd, N_MEM, stride=8), :] for j in range(2)]
    return jnp.concatenate(halves, axis=1).astype(BF16)


def _xattn_s_body(x_ref, gpre_ref, wq_ref, k_ref, v_ref, wo_ref, gpost_ref, xo_ref, q_ref, o_ref, s_ref, *, bb):
    _xattn_s_step(pl.program_id(0), pl.num_programs(0), x_ref, gpre_ref, wq_ref, k_ref, v_ref, wo_ref, gpost_ref,
                  xo_ref, q_ref, o_ref, s_ref, bb=bb)


def _xattn_s_step(step, nsteps, x_ref, gpre_ref, wq_ref, k_ref, v_ref, wo_ref, gpost_ref, xo_ref, q_ref, o_ref, s_ref,
                  *, bb):
    @pl.when(step == 0)
    def _():
        h = _rms(x_ref[...], gpre_ref[...]).astype(BF16)
        q_ref[...] = _dot(h, wq_ref[...].astype(BF16))

    first = (lax.broadcasted_iota(jnp.int32, (8, XHD), 0) >> 2) == 0
    groups = [(pi, hd, bi) for pi in range(bb // 2) for hd in range(XH) for bi in range(2)]
    row0 = [pl.multiple_of((step * (bb // 2) + pi) * 8, 8) for pi in range(bb // 2)]
    for gi, (pi, hd, bi) in enumerate(groups):
        q8 = q_ref[pl.ds(row0[pi], 8), hd * XHD:(hd + 1) * XHD].astype(BF16)
        kh = _head_rows(k_ref, pi * 2 + bi, hd)
        s_ref[gi * 8:(gi + 1) * 8, :] = _dot_nt(q8, kh) * (XHD ** -0.5)
    s_ref[...] = _softmax_rows(s_ref[...])
    for gi, (pi, hd, bi) in enumerate(groups):
        if bi == 1:
            continue
        pv = [_dot(s_ref[(gi + b) * 8:(gi + b + 1) * 8, :].astype(BF16), _head_rows(v_ref, pi * 2 + b, hd))
              for b in range(2)]
        o_ref[pl.ds(row0[pi], 8), hd * XHD:(hd + 1) * XHD] = jnp.where(first, pv[0], pv[1])

    @pl.when(step == nsteps - 1)
    def _():
        out = _dot(o_ref[...].astype(BF16), wo_ref[...].astype(BF16))
        xo_ref[...] = x_ref[...] + _rms(out, gpost_ref[...])


def _xattn_ps_body(x_ref, gpre_ref, wqf_ref, k_ref, v_ref, wof_ref, gpost_ref,
                   xs_ref, gpres_ref, ks_ref, vs_ref, gposts_ref,
                   xo_ref, xso_ref,
                   wq_ref, wo_ref, q_ref, s_ref, p_ref, o_ref, qs_ref, os_ref, ss_ref, *, bb):
    _xattn_p_body(x_ref, gpre_ref, wqf_ref, k_ref, v_ref, wof_ref, gpost_ref, xo_ref, wq_ref, wo_ref, q_ref, s_ref,
                  p_ref, o_ref)
    step = pl.program_id(0) * pl.num_programs(1) + pl.program_id(1)
    _xattn_s_step(step, pl.num_programs(0) * pl.num_programs(1), xs_ref, gpres_ref, wq_ref, ks_ref, vs_ref, wo_ref,
                  gposts_ref, xso_ref, qs_ref, os_ref, ss_ref, bb=bb)


def _xattn_ps(l, x, gpre, wq, kb, vb, wo, gpost, xs, gpres, ks, vs, gposts):
    bb = 2
    steps = NS // bb
    tm = NB * SEQ // steps
    nt, row_spec, h_in, h_scratch = _xattn_p_parts(l, tm)
    kv_spec = pl.BlockSpec((None, bb, N_MEM * 8, 128), lambda b, t: (l, b * nt + t, 0, 0))
    return pl.pallas_call(
        functools.partial(_xattn_ps_body, bb=bb),
        grid=(NB, nt),
        in_specs=h_in + [_const((RS, D)), _const((1, D)), kv_spec, kv_spec, _const((1, D))],
        out_specs=[row_spec, _full((RS, D))],
        out_shape=[jax.ShapeDtypeStruct((NB * SEQ, D), F32), jax.ShapeDtypeStruct((RS, D), F32)],
        scratch_shapes=h_scratch + [pltpu.VMEM((RS, D), F32), pltpu.VMEM((RS, D), F32),
                                    pltpu.VMEM((bb * XH * 8, N_MEM), F32)],
        compiler_params=_params(("arbitrary", "arbitrary"), 56),
        name="xattn_ps",
    )(x, gpre, wq, kb, vb, wo, gpost, xs, gpres, ks, vs, gposts)


def _xattn_s(l, x, gpre, wq, k, v, wo, gpost):
    bb = 4
    kv_spec = pl.BlockSpec((None, bb, N_MEM * 8, 128), lambda i: (l, i, 0, 0))
    return pl.pallas_call(
        functools.partial(_xattn_s_body, bb=bb),
        grid=(NS // bb,),
        in_specs=[_const((RS, D)), _const((1, D)), _layer_const(l, (D, D)), kv_spec, kv_spec, _layer_const(l, (D, D)),
                  _const((1, D))],
        out_specs=_full((RS, D)),
        out_shape=jax.ShapeDtypeStruct((RS, D), F32),
        scratch_shapes=[pltpu.VMEM((RS, D), F32), pltpu.VMEM((RS, D), F32), pltpu.VMEM((bb * XH * 8, N_MEM), F32)],
        compiler_params=_params(("arbitrary",), 48),
        name="xattn_s",
    )(x, gpre, wq, k, v, wo, gpost)


def _ffn_p_body(x_ref, gpre_ref, wg_ref, wv_ref, cw_ref, wdn_ref, gpost_ref, xo_ref, st_ref, carry_ref, h_ref, act_ref,
                *sh_refs, tm):
    @pl.when(pl.program_id(1) == 0)
    def _():
        carry_ref[...] = jnp.zeros_like(carry_ref)

    h_ref[...] = _rms(x_ref[...], gpre_ref[...]).astype(BF16)
    for c in range(NFC):
        halves = []
        for i, (base, w_ref) in enumerate(((0, wg_ref), (DFF, wv_ref))):
            cols = slice(base + c * FC, base + (c + 1) * FC)
            up = _dot(h_ref[...], w_ref[:, c * FC:(c + 1) * FC])
            halves.append(_conv_rows(up, carry_ref, cw_ref, cols, sh_refs[(2 * c + i) % len(sh_refs)]))
            carry_ref[:, cols] = up[tm - 8:tm, :]
            st_ref[:, cols] = up[tm - 8:tm, :]
        act_ref[:, c * FC:(c + 1) * FC] = (_gelu(halves[0]) * halves[1]).astype(BF16)
    out = _dot(act_ref[...], wdn_ref[...])
    xo_ref[...] = x_ref[...] + _rms(out, gpost_ref[...])


def _ffn_p_parts(l, tm):
    nt = SEQ // tm
    row_spec = pl.BlockSpec((tm, D), lambda b, t: (b * nt + t, 0))
    in_specs = [row_spec, _const((1, D)), _const((D, DFF)), _const((D, DFF)), _layer_const(l, (3, 2 * DFF)),
                _const((DFF, D)), _const((1, D))]
    out_specs = [row_spec, pl.BlockSpec((None, 8, 2 * DFF), lambda b, t: (b, 0, 0))]
    out_shape = [jax.ShapeDtypeStruct((NB * SEQ, D), F32), jax.ShapeDtypeStruct((NB, 8, 2 * DFF), F32)]
    scratch = ([pltpu.VMEM((8, 2 * DFF), F32), pltpu.VMEM((tm, D), BF16), pltpu.VMEM((tm, DFF), BF16)]
               + [pltpu.VMEM((FC // 128, tm + 8, 128), F32)] * 4)
    return nt, in_specs, out_specs, out_shape, scratch


def _ffn_p(l, x, gpre, wg, wv, cw, wdn, gpost):
    tm = 512
    nt, in_specs, out_specs, out_shape, scratch = _ffn_p_parts(l, tm)
    return pl.pallas_call(
        functools.partial(_ffn_p_body, tm=tm),
        grid=(NB, nt),
        in_specs=in_specs,
        out_specs=out_specs,
        out_shape=out_shape,
        scratch_shapes=scratch,
        compiler_params=_params(("arbitrary", "arbitrary"), 52),
        name="ffn_p",
    )(x, gpre, wg, wv, cw, wdn, gpost)


def _load_time_major(x_ref, pan_ref):
    n = x_ref.shape[1] // 128
    for p in range(n):
        pan_ref[p] = x_ref[:, p * 128:(p + 1) * 128]
    return jnp.concatenate(
        [jnp.concatenate([pan_ref[p, pl.ds(t, NS, stride=TS), :] for p in range(n)], axis=1) for t in range(TS)],
        axis=0)


def _store_batch_major(o_ref, y, pan_ref):
    n = o_ref.shape[1] // 128
    for p in range(n):
        for t in range(TS):
            pan_ref[p, pl.ds(t, NS, stride=TS), :] = y[t * NS:(t + 1) * NS, p * 128:(p + 1) * 128]
    for p in range(n):
        o_ref[:, p * 128:(p + 1) * 128] = pan_ref[p]


def _conv_time_major(u, prev, w_ref):
    blocks = list(prev) + [u[t * NS:(t + 1) * NS, :] for t in range(TS)]
    w0, w1, w2 = w_ref[0:1, :], w_ref[1:2, :], w_ref[2:3, :]
    return jnp.concatenate([(w0 * blocks[t] + w1 * blocks[t + 1]) + w2 * blocks[t + 2] for t in range(TS)], axis=0)


def _ffn_s_body(x_ref, gpre_ref, wg_ref, wv_ref, cg_ref, cv_ref, sg_ref, sv_ref, wdn_ref, gpost_ref,
                xo_ref, ng_ref, nv_ref, wgb_ref, wvb_ref, wdb_ref, xt_ref, h_ref, acc_ref, pan_ref):
    c = pl.program_id(0)

    @pl.when(c == 0)
    def _():
        xt = _load_time_major(x_ref, pan_ref)
        xt_ref[...] = xt
        h_ref[...] = _rms(xt, gpre_ref[...]).astype(BF16)
        acc_ref[...] = jnp.zeros_like(acc_ref)

    wgb_ref[...] = wg_ref[...].astype(BF16)
    wvb_ref[...] = wv_ref[...].astype(BF16)
    wdb_ref[...] = wdn_ref[...].astype(BF16)
    ys = []
    for wb_ref, cw_ref, s_ref, n_ref in ((wgb_ref, cg_ref, sg_ref, ng_ref), (wvb_ref, cv_ref, sv_ref, nv_ref)):
        u = _dot(h_ref[...], wb_ref[...])
        ys.append(_conv_time_major(u, [s_ref[:, r, :] for r in range(2)], cw_ref))
        for r in range(2):
            n_ref[:, r, :] = u[(TS - 2 + r) * NS:(TS - 1 + r) * NS, :]
    act = (_gelu(ys[0]) * ys[1]).astype(BF16)
    acc_ref[...] += _dot(act, wdb_ref[...])

    @pl.when(c == pl.num_programs(0) - 1)
    def _():
        _store_batch_major(xo_ref, xt_ref[...] + _rms(acc_ref[...], gpost_ref[...]), pan_ref)


def _ffn_s(l, x, gpre, wup, cw, st, wdn, gpost):
    gate = lambda *lead: pl.BlockSpec((None,) + lead + (FC,), lambda c: (l,) + (0,) * len(lead) + (c,))
    val = lambda *lead: pl.BlockSpec((None,) + lead + (FC,), lambda c: (l,) + (0,) * len(lead) + (NFC + c,))
    s_out = pl.BlockSpec((NS, 2, FC), lambda c: (0, 0, c))
    wb_out = pl.BlockSpec((D, FC), lambda c: (0, c))
    return pl.pallas_call(
        _ffn_s_body,
        grid=(NFC,),
        in_specs=[_const((RS, D)), _const((1, D)), gate(D), val(D), gate(3), val(3), gate(NS, 2), val(NS, 2),
                  pl.BlockSpec((None, FC, D), lambda c: (l, c, 0)), _const((1, D))],
        out_specs=[_full((RS, D)), s_out, s_out, wb_out, wb_out, pl.BlockSpec((FC, D), lambda c: (c, 0))],
        out_shape=[jax.ShapeDtypeStruct((RS, D), F32),
                   jax.ShapeDtypeStruct((NS, 2, DFF), F32), jax.ShapeDtypeStruct((NS, 2, DFF), F32),
                   jax.ShapeDtypeStruct((D, DFF), BF16), jax.ShapeDtypeStruct((D, DFF), BF16),
                   jax.ShapeDtypeStruct((DFF, D), BF16)],
        scratch_shapes=[pltpu.VMEM((RS, D), F32), pltpu.VMEM((RS, D), BF16), pltpu.VMEM((RS, D), F32),
                        pltpu.VMEM((D // 128, RS, 128), F32)],
        compiler_params=_params(("arbitrary",), 40),
        name="ffn_s",
    )(x, gpre, wup, wup, cw, cw, st, st, wdn, gpost)


def _rope_head(z, cos, sin, base):
    x1 = z[:, base:base + 128]
    x2 = z[:, base + 128:base + 256]
    return x1 * cos - x2 * sin, x1 * sin + x2 * cos


def _group_norm_gate(o, g):
    mu = jnp.mean(o, axis=-1, keepdims=True)
    d = o - mu
    var = jnp.mean(d * d, axis=-1, keepdims=True)
    return (g * jax.nn.sigmoid(g)) * (d * lax.rsqrt(var + EPS))


RCH = 256


def _ret_p_body(x_ref, cos_ref, sin_ref, gpre_ref, wq_ref, wk_ref, wv_ref, wg_ref, wout_ref, gpost_ref, xo_ref, s_ref,
                h_ref, q_ref, k_ref, kd_ref, v_ref, g_ref, y_ref, dm_ref, qd_ref, kdec_ref, *, tm):
    first = (pl.program_id(0) == 0) & (pl.program_id(1) == 0)

    @pl.when(first)
    def _():
        i = lax.broadcasted_iota(jnp.int32, (RCH, RCH), 0).astype(F32)
        j = lax.broadcasted_iota(jnp.int32, (RCH, RCH), 1).astype(F32)
        i1 = lax.broadcasted_iota(jnp.int32, (RCH, 128), 0).astype(F32)
        for hd in range(RET_H):
            lg = _LOG_G[hd]
            dm_ref[hd] = jnp.where(i >= j, jnp.exp(lg * jnp.maximum(i - j, 0.0)), 0.0)
            qd_ref[hd] = jnp.exp(lg * (i1 + 1.0))
            kdec_ref[hd] = jnp.exp(lg * (RCH - 1.0 - i1))

    @pl.when(pl.program_id(1) == 0)
    def _():
        s_ref[...] = jnp.zeros_like(s_ref)

    h_ref[...] = _rms(x_ref[...], gpre_ref[...]).astype(BF16)
    cos = cos_ref[...]
    sin = sin_ref[...]
    nchunk = tm // RCH
    for hd in range(RET_H):
        base = hd * RET_DK
        zq = _dot(h_ref[...], wq_ref[:, base:base + RET_DK])
        zk = _dot(h_ref[...], wk_ref[:, base:base + RET_DK])
        q1, q2 = _rope_head(zq, cos, sin, 0)
        k1, k2 = _rope_head(zk, cos, sin, 0)
        q_ref[:, base:base + 128] = (q1 * (RET_DK ** -0.5)).astype(BF16)
        q_ref[:, base + 128:base + 256] = (q2 * (RET_DK ** -0.5)).astype(BF16)
        k_ref[:, base:base + 128] = k1.astype(BF16)
        k_ref[:, base + 128:base + 256] = k2.astype(BF16)
        kdec = jnp.concatenate([kdec_ref[hd]] * nchunk, axis=0)
        kd_ref[:, base:base + 128] = k1 * kdec
        kd_ref[:, base + 128:base + 256] = k2 * kdec
        vcols = slice(hd * RET_DV, (hd + 1) * RET_DV)
        v_ref[:, vcols] = _dot(h_ref[...], wv_ref[:, vcols]).astype(BF16)
        g_ref[:, vcols] = _dot(h_ref[...], wg_ref[:, vcols])
    for hd in range(RET_H):
        kcols = slice(hd * RET_DK, (hd + 1) * RET_DK)
        vcols = slice(hd * RET_DV, (hd + 1) * RET_DV)
        decay_l = float(np.exp(np.float32(_LOG_G[hd]) * np.float32(RCH)))
        qd = qd_ref[hd]
        qd4 = jnp.concatenate([qd, qd, qd, qd], axis=1)
        for c in range(nchunk):
            rows = slice(c * RCH, (c + 1) * RCH)
            qc = q_ref[rows, kcols]
            vc = v_ref[rows, vcols]
            s_prev = s_ref[hd]
            sc = (_dot_nt(qc, k_ref[rows, kcols]) * dm_ref[hd]).astype(BF16)
            o = _dot(sc, vc) + _dot(qc, s_prev.astype(BF16)) * qd4
            kdt = kd_ref[rows, kcols].T.astype(BF16)
            s_ref[hd] = decay_l * s_prev + _dot(kdt, vc)
            y_ref[rows, vcols] = _group_norm_gate(o, g_ref[rows, vcols]).astype(BF16)
    out = _dot(y_ref[...], wout_ref[...])
    xo_ref[...] = x_ref[...] + _rms(out, gpost_ref[...])


def _ret_p(x, cos, sin, gpre, wq, wk, wv, wg, wout, gpost):
    tm = 512
    nt = SEQ // tm
    row_spec = pl.BlockSpec((tm, D), lambda b, t: (b * nt + t, 0))
    cs_spec = pl.BlockSpec((tm, 128), lambda b, t: (t, 0))
    return pl.pallas_call(
        functools.partial(_ret_p_body, tm=tm),
        grid=(NB, nt),
        in_specs=[row_spec, cs_spec, cs_spec, _const((1, D)), _const((D, NQ)), _const((D, NQ)), _const((D, NV)),
                  _const((D, NV)), _const((NV, D)), _const((1, D))],
        out_specs=[row_spec, pl.BlockSpec((None, RET_H, RET_DK, RET_DV), lambda b, t: (b, 0, 0, 0))],
        out_shape=[jax.ShapeDtypeStruct((NB * SEQ, D), F32), jax.ShapeDtypeStruct((NB, RET_H, RET_DK, RET_DV), F32)],
        scratch_shapes=[pltpu.VMEM((tm, D), BF16), pltpu.VMEM((tm, NQ), BF16), pltpu.VMEM((tm, NQ), BF16),
                        pltpu.VMEM((tm, NQ), F32), pltpu.VMEM((tm, NV), BF16), pltpu.VMEM((tm, NV), F32),
                        pltpu.VMEM((tm, NV), BF16), pltpu.VMEM((RET_H, RCH, RCH), F32),
                        pltpu.VMEM((RET_H, RCH, 128), F32), pltpu.VMEM((RET_H, RCH, 128), F32)],
        compiler_params=_params(("arbitrary", "arbitrary"), 56),
        name="ret_p",
    )(x, cos, sin, gpre, wq, wk, wv, wg, wout, gpost)


def _ret_s_proj_body(x_ref, cos_ref, sin_ref, gpre_ref, wqf_ref, wkf_ref, wvf_ref, wgf_ref,
                     q_ref, kdt_ref, v_ref, g_ref, oi_ref, wq_ref, wk_ref, wv_ref, wg_ref, h_ref):
    hd = pl.program_id(0)

    @pl.when(hd == 0)
    def _():
        h_ref[...] = _rms(x_ref[...], gpre_ref[...]).astype(BF16)

    for wf_ref, wb_ref in ((wqf_ref, wq_ref), (wkf_ref, wk_ref), (wvf_ref, wv_ref), (wgf_ref, wg_ref)):
        wb_ref[...] = wf_ref[...].astype(BF16)
    lg = jnp.where(hd == 0, _LOG_G[0], jnp.where(hd == 1, _LOG_G[1], jnp.where(hd == 2, _LOG_G[2], _LOG_G[3])))
    h = h_ref[...]
    cos = cos_ref[...]
    sin = sin_ref[...]
    ti = (lax.broadcasted_iota(jnp.int32, (RS, 128), 0) & (TS - 1)).astype(F32)
    i = lax.broadcasted_iota(jnp.int32, (CHUNK, CHUNK), 0)
    j = lax.broadcasted_iota(jnp.int32, (CHUNK, CHUNK), 1)
    same = ((i >> 2) == (j >> 2)) & (i >= j)
    dm = jnp.where(same, jnp.exp(lg * jnp.maximum(i - j, 0).astype(F32)), 0.0)
    q1, q2 = _rope_head(_dot(h, wq_ref[...]), cos, sin, 0)
    k1, k2 = _rope_head(_dot(h, wk_ref[...]), cos, sin, 0)
    q = jnp.concatenate([q1, q2], axis=1) * (RET_DK ** -0.5)
    k = jnp.concatenate([k1, k2], axis=1)
    q_ref[...] = q
    kdec = jnp.exp(lg * (TS - 1.0 - ti))
    kd = k * jnp.concatenate([kdec, kdec], axis=1)
    vf = _dot(h, wv_ref[...])
    v_ref[...] = vf
    v = vf.astype(BF16)
    g_ref[...] = _dot(h, wg_ref[...])
    qb = q.astype(BF16)
    kb = k.astype(BF16)
    for tl in range(RS // CHUNK):
        rows = slice(tl * CHUNK, (tl + 1) * CHUNK)
        sc = (_dot_nt(qb[rows], kb[rows]) * dm).astype(BF16)
        oi_ref[rows, :] = _dot(sc, v[rows])
        kdt_ref[tl] = kd[rows].T.astype(BF16)


def _ret_s_proj(x, cos, sin, gpre, win):
    def wcol(width, first_block):
        return pl.BlockSpec((None, D, width), lambda h: (0, 0, first_block + h))
    def hcol(rows, width):
        return pl.BlockSpec((rows, width), lambda h: (0, h))
    return pl.pallas_call(
        _ret_s_proj_body,
        grid=(RET_H,),
        in_specs=[_const((RS, D)), _const((RS, 128)), _const((RS, 128)), _const((1, D)),
                  wcol(RET_DK, 0), wcol(RET_DK, NQ // RET_DK), wcol(RET_DV, 2 * NQ // RET_DV),
                  wcol(RET_DV, (2 * NQ + NV) // RET_DV)],
        out_specs=[hcol(RS, RET_DK),
                   pl.BlockSpec((RS // CHUNK, None, RET_DK, CHUNK), lambda h: (0, h, 0, 0)),
                   hcol(RS, RET_DV), hcol(RS, RET_DV), hcol(RS, RET_DV),
                   hcol(D, RET_DK), hcol(D, RET_DK), hcol(D, RET_DV), hcol(D, RET_DV)],
        out_shape=[jax.ShapeDtypeStruct((RS, NQ), F32),
                   jax.ShapeDtypeStruct((RS // CHUNK, RET_H, RET_DK, CHUNK), BF16),
                   jax.ShapeDtypeStruct((RS, NV), F32), jax.ShapeDtypeStruct((RS, NV), F32),
                   jax.ShapeDtypeStruct((RS, NV), F32),
                   jax.ShapeDtypeStruct((D, NQ), BF16), jax.ShapeDtypeStruct((D, NQ), BF16),
                   jax.ShapeDtypeStruct((D, NV), BF16), jax.ShapeDtypeStruct((D, NV), BF16)],
        scratch_shapes=[pltpu.VMEM((RS, D), BF16)],
        compiler_params=_params(("arbitrary",), 40),
        name="ret_s_proj",
    )(x, cos, sin, gpre, win, win, win, win)


def _ret_s_core_body(step, q_ref, kdt_ref, v_ref, oi_ref, s_ref, o_ref, so_ref):
    pair_in_tile = step % (CHUNK // 8)
    row8 = lax.broadcasted_iota(jnp.int32, (8, RET_DK), 0)
    row128 = lax.broadcasted_iota(jnp.int32, (CHUNK, RET_DV), 0)
    t8 = (lax.broadcasted_iota(jnp.int32, (8, RET_DV), 0) & (TS - 1)).astype(F32)
    for hd in range(RET_H):
        lg = _LOG_G[hd]
        decay_l = float(np.exp(np.float32(lg) * np.float32(TS)))
        q8 = q_ref[:, hd * RET_DK:(hd + 1) * RET_DK]
        v128 = v_ref[:, hd * RET_DV:(hd + 1) * RET_DV]
        kdt = kdt_ref[hd]
        inter = jnp.zeros((8, RET_DV), F32)
        for bi in range(2):
            s_prev = s_ref[bi, hd]
            qm = jnp.where((row8 >> 2) == bi, q8, 0.0).astype(BF16)
            inter = inter + _dot(qm, s_prev.astype(BF16))
            vm = jnp.where((row128 >> 2) == pair_in_tile * 2 + bi, v128, 0.0).astype(BF16)
            so_ref[bi, hd] = decay_l * s_prev + _dot(kdt, vm)
        cols = slice(hd * RET_DV, (hd + 1) * RET_DV)
        o_ref[:, cols] = oi_ref[:, cols] + inter * jnp.exp(lg * (t8 + 1.0))


def _ret_s_core_parts():
    ppt = CHUNK // 8
    s_spec = pl.BlockSpec((2, RET_H, RET_DK, RET_DV), lambda i: (i, 0, 0, 0))
    in_specs = [pl.BlockSpec((8, NQ), lambda i: (i, 0)),
                pl.BlockSpec((None, RET_H, RET_DK, CHUNK), lambda i: (i // ppt, 0, 0, 0)),
                pl.BlockSpec((CHUNK, NV), lambda i: (i // ppt, 0)),
                pl.BlockSpec((8, NV), lambda i: (i, 0)),
                s_spec]
    out_specs = [pl.BlockSpec((8, NV), lambda i: (i, 0)), s_spec]
    out_shape = [jax.ShapeDtypeStruct((RS, NV), F32), jax.ShapeDtypeStruct((NS, RET_H, RET_DK, RET_DV), F32)]
    return in_specs, out_specs, out_shape


def _host_ret_s_body(*refs, host_body, n_in, n_out):
    host_in, guest_in = refs[:n_in], refs[n_in:n_in + 5]
    k = n_in + 5
    host_out, guest_out, scratch = refs[k:k + n_out], refs[k + n_out:k + n_out + 2], refs[k + n_out + 2:]
    host_body(*host_in, *host_out, *scratch)
    step = pl.program_id(0) * pl.num_programs(1) + pl.program_id(1)
    _ret_s_core_body(step, *guest_in, *guest_out)


def _with_ret_s(name, host_body, nt, h_in, h_out, h_shape, scratch, host_args, guest_args):
    g_in, g_out, g_shape = _ret_s_core_parts()

    def on_grid(spec):
        return pl.BlockSpec(spec.block_shape, lambda b, t, m=spec.index_map: m(b * nt + t))

    return pl.pallas_call(
        functools.partial(_host_ret_s_body, host_body=host_body, n_in=len(h_in), n_out=len(h_out)),
        grid=(NB, nt),
        in_specs=h_in + [on_grid(sp) for sp in g_in],
        out_specs=h_out + [on_grid(sp) for sp in g_out],
        out_shape=h_shape + g_shape,
        scratch_shapes=scratch,
        compiler_params=_params(("arbitrary", "arbitrary"), 56),
        name=name,
    )(*host_args, *guest_args)


def _ret_s_out_body(x_ref, o_ref, g_ref, woutf_ref, gpost_ref, xo_ref, wout_ref, y_ref):
    wout_ref[...] = woutf_ref[...].astype(BF16)
    for hd in range(RET_H):
        cols = slice(hd * RET_DV, (hd + 1) * RET_DV)
        y_ref[:, cols] = _group_norm_gate(o_ref[:, cols], g_ref[:, cols]).astype(BF16)
    out = _dot(y_ref[...], wout_ref[...])
    xo_ref[...] = x_ref[...] + _rms(out, gpost_ref[...])


def _ret_s_out(x, o, g, wout, gpost):
    return pl.pallas_call(
        _ret_s_out_body,
        grid=(1,),
        in_specs=[_const((RS, D)), _const((RS, NV)), _const((RS, NV)), _layer_const(0, (NV, D)), _const((1, D))],
        out_specs=[_full((RS, D)), _full((NV, D))],
        out_shape=[jax.ShapeDtypeStruct((RS, D), F32), jax.ShapeDtypeStruct((NV, D), BF16)],
        scratch_shapes=[pltpu.VMEM((RS, NV), BF16)],
        compiler_params=_params(("arbitrary",), 48),
        name="ret_s_out",
    )(x, o, g, wout, gpost)


def _rope_tables(pos):
    half = RET_DK // 2
    inv = ROPE_BASE ** (-jnp.arange(half, dtype=F32) / half)
    ang = pos.astype(F32)[:, None] * inv[None, :]
    return jnp.cos(ang), jnp.sin(ang)


def _state_rows(st):
    lead, c = st.shape[:-3], st.shape[-1]
    n = len(lead)
    s = st.reshape(lead + (NS, 2, c // 128, 128))
    s = s.transpose(tuple(range(n)) + (n + 2, n + 1, n, n + 3))
    return s.reshape(lead + (c // 128 * 2, NS, 128))


def _from_state_rows(s):
    tiles = s.shape[0] // 2
    return s.reshape(tiles, 2, NS, 128).transpose(2, 1, 0, 3).reshape(NS, 2, tiles * 128)


def kernel(x_prompt, x_sample, mem_prompt, cache_mem_k, cache_mem_v, state_shortconv, state_retention, state_ffn_conv, norm_mix_pre, norm_mix_post, w_in_even, sgu_vnorm, sgu_w, sgu_b, conv_short, w_out_even, w_in_odd, w_out_odd, norm_x_pre, norm_x_post, norm_mem, w_xq, w_xk, w_xv, w_xo, norm_ffn_pre, norm_ffn_post, w_ffn_up, conv_ffn, w_ffn_down):
    row = lambda g: g.reshape(1, -1)

    xs = x_sample.reshape(RS, D)
    w4 = jnp.tril(sgu_w[0][:, :TS, :TS])
    gw = jnp.repeat(w4.transpose(1, 2, 0), CHUNK, axis=2)
    gb = jnp.repeat(sgu_b[0][:, :TS].T, CHUNK, axis=1)
    xs, sc_s, v_s = _even_s(xs, row(norm_mix_pre[0]), w_in_even, row(sgu_vnorm[0]), gw, gb, conv_short,
                            _state_rows(state_shortconv[0]), w_out_even, row(norm_mix_post[0]))
    cos_s, sin_s = _rope_tables(PAST + (jnp.arange(RS, dtype=jnp.int32) % TS))
    cache_k, cache_v = _head_view(cache_mem_k), _head_view(cache_mem_v)
    ffn_s_states, ffn_p_states, ffn_w = [], [], []

    def sample_ffn(l, xs):
        xs, ng, nv, wg_b, wv_b, wd_b = _ffn_s(l, xs, row(norm_ffn_pre[l]), w_ffn_up, conv_ffn, state_ffn_conv,
                                              w_ffn_down, row(norm_ffn_post[l]))
        ffn_s_states.append(jnp.concatenate([ng, nv], axis=-1))
        ffn_w.append((wg_b, wv_b, wd_b))
        return xs

    def prompt_ffn(l, xp):
        xp, st = _ffn_p(l, xp, row(norm_ffn_pre[l]), ffn_w[l][0], ffn_w[l][1], conv_ffn, ffn_w[l][2],
                        row(norm_ffn_post[l]))
        ffn_p_states.append(st[:, 6:8, :])
        return xp

    xs = _xattn_s(0, xs, row(norm_x_pre[0]), w_xq, cache_k, cache_v, w_xo, row(norm_x_post[0]))
    xs = sample_ffn(0, xs)
    q, kdt, v, g, oi, wq_o, wk_o, wv_o, wg_o = _ret_s_proj(xs, cos_s, sin_s, row(norm_mix_pre[1]), w_in_odd)

    mem_k, mem_v, mem_kb, mem_vb = _memkv(mem_prompt.reshape(NB * N_MEM, D), norm_mem.reshape(2, 1, D), w_xk, w_xv)
    xp = x_prompt.reshape(NB * SEQ, D)
    sb_full = jnp.repeat(sgu_b[0].T, CHUNK, axis=1)
    tm = NB * SEQ // (NS // 2)
    nt, e_in, e_out, e_shape, e_scratch = _even_p_parts(tm)
    xp, sc_p, o, ret_s_state = _with_ret_s(
        "even_p_ret_s", functools.partial(_even_p_body, tm=tm), nt, e_in, e_out, e_shape, e_scratch,
        (xp, row(norm_mix_pre[0]), w_in_even, row(sgu_vnorm[0]), sgu_w, sb_full, conv_short, w_out_even,
         row(norm_mix_post[0])), (q, kdt, v, oi, state_retention[0]))
    xs, wout_o = _ret_s_out(xs, o, g, w_out_odd, row(norm_mix_post[1]))

    xp = _xattn_p(0, xp, row(norm_x_pre[0]), w_xq, mem_kb, mem_vb, w_xo, row(norm_x_post[0]))
    xp = prompt_ffn(0, xp)
    cos_p, sin_p = _rope_tables(jnp.arange(SEQ, dtype=jnp.int32))
    xp, ret_p_state = _ret_p(xp, cos_p, sin_p, row(norm_mix_pre[1]), wq_o, wk_o, wv_o, wg_o, wout_o,
                             row(norm_mix_post[1]))
    xp, xs = _xattn_ps(1, xp, row(norm_x_pre[1]), w_xq, mem_kb, mem_vb, w_xo, row(norm_x_post[1]),
                       xs, row(norm_x_pre[1]), cache_k, cache_v, row(norm_x_post[1]))
    xs = sample_ffn(1, xs)
    xp = prompt_ffn(1, xp)

    return (xp.reshape(NB, SEQ, D), xs.reshape(NS, TS, D),
            _from_head_view(mem_k), _from_head_view(mem_v),
            sc_p[None, :, 6:8, :], _from_state_rows(sc_s)[None],
            v_s.reshape(1, NS, TS, SGU_W),
            ret_p_state[None], ret_s_state[None],
            jnp.stack(ffn_p_states), jnp.stack(ffn_s_states))
```

```python
import functools

import numpy as np
import jax
import jax.numpy as jnp
from jax import lax
from jax.experimental import pallas as pl
from jax.experimental.pallas import tpu as pltpu

F32 = jnp.float32
BF16 = jnp.bfloat16

D = 1024
SEQ = 2048
NB = 8
NS = 128
TS = 4
RS = NS * TS
PAST = 16384
CHUNK = 128
EPS = 1e-6
SGU_W = 512
SC_W = 512
RET_H = 4
RET_DK = 256
RET_DV = 512
NQ = RET_H * RET_DK
NV = RET_H * RET_DV
N_MEM = 256
XH = 4
XHD = 256
DFF = 2816
FC = 256
NFC = DFF // FC
ROPE_BASE = 10000.0

_LOG_G = [float(v) for v in np.log1p(-np.exp2(np.float32(-5.0) - np.arange(RET_H, dtype=np.float32))).astype(np.float32)]

_MIB = 1024 * 1024


def _rms(x, g):
    ms = jnp.mean(x * x, axis=-1, keepdims=True)
    return (x * lax.rsqrt(ms + EPS)) * g


_GELU_C = 0.7978845608028654
_LOG2E = 1.4426950408889634


def _gelu(x):
    k0 = -2.0 * _LOG2E * _GELU_C
    k1 = k0 * 0.044715
    return x * (1.0 / (1.0 + jnp.exp2(x * (k0 + k1 * (x * x)))))


def _dot(a, b):
    return jnp.dot(a, b, preferred_element_type=F32)


def _dot_nt(a, b):
    return lax.dot_general(a, b, (((1,), (1,)), ((), ())), preferred_element_type=F32)


def _const(shape):
    n = len(shape)
    return pl.BlockSpec(shape, lambda *_: (0,) * n, pipeline_mode=pl.Buffered(1))


def _layer_const(l, shape):
    n = len(shape)
    return pl.BlockSpec((None,) + tuple(shape), lambda *_: (l,) + (0,) * n, pipeline_mode=pl.Buffered(1))


def _full(shape):
    n = len(shape)
    return pl.BlockSpec(shape, lambda *_: (0,) * n)


def _params(sem, vmem_mib):
    return pltpu.CompilerParams(dimension_semantics=sem, vmem_limit_bytes=vmem_mib * _MIB)


def _conv_rows(u, prev_ref, w_ref, cols, sh_ref):
    r, c = u.shape
    ys = []
    for j in range(c // 128):
        cj = slice(cols.start + j * 128, cols.start + (j + 1) * 128)
        uj = u[:, j * 128:(j + 1) * 128]
        sh_ref[j, 0:8, :] = prev_ref[:, cj]
        sh_ref[j, 8:r + 8, :] = uj
        u1 = sh_ref[j, 7:r + 7, :]
        u2 = sh_ref[j, 6:r + 6, :]
        ys.append((w_ref[0:1, cj] * u2 + w_ref[1:2, cj] * u1) + w_ref[2:3, cj] * uj)
    return jnp.concatenate(ys, axis=1)


def _memkv_body(mem_ref, g_ref, wk_ref, wv_ref, k_ref, v_ref, kb_ref, vb_ref, *, nb):
    mn = _rms(mem_ref[...], g_ref[...]).astype(BF16)
    for w_ref, o_ref, ob_ref in ((wk_ref, k_ref, kb_ref), (wv_ref, v_ref, vb_ref)):
        y = _dot(mn, w_ref[...].astype(BF16))
        ob_ref[...] = y.astype(BF16)
        for b in range(nb):
            for hd in range(XH):
                for j in range(2):
                    c0 = hd * XHD + j * 128
                    o_ref[b, pl.ds(j * XH + hd, N_MEM, stride=8), :] = y[b * N_MEM:(b + 1) * N_MEM, c0:c0 + 128]


def _memkv(mem, g, wk, wv):
    rows = mem.shape[0]
    nb = 2
    tm = nb * N_MEM
    o_spec = pl.BlockSpec((None, nb, N_MEM * 8, 128), lambda l, i: (l, i, 0, 0))
    ob_spec = pl.BlockSpec((None, tm, D), lambda l, i: (l, i, 0))
    w_spec = pl.BlockSpec((None, D, D), lambda l, i: (l, 0, 0))
    return pl.pallas_call(
        functools.partial(_memkv_body, nb=nb),
        grid=(2, rows // tm),
        in_specs=[pl.BlockSpec((tm, D), lambda l, i: (i, 0)),
                  pl.BlockSpec((None, 1, D), lambda l, i: (l, 0, 0)),
                  w_spec, w_spec],
        out_specs=[o_spec, o_spec, ob_spec, ob_spec],
        out_shape=[jax.ShapeDtypeStruct((2, NB, N_MEM * 8, 128), F32)] * 2
        + [jax.ShapeDtypeStruct((2, rows, D), BF16)] * 2,
        compiler_params=_params(("arbitrary", "arbitrary"), 40),
        name="memkv",
    )(mem, g, wk, wv)


def _from_head_view(c):
    s = c.shape
    return c.reshape(s[0], s[1], N_MEM, 2, XH, 128).transpose(0, 1, 2, 4, 3, 5).reshape(s[0], s[1], N_MEM, XH, XHD)


def _even_front(x, gpre_ref, win, vn_ref):
    h = _rms(x, gpre_ref[...]).astype(BF16)
    u = _gelu(_dot(h, win(0, 512)))
    v = _rms(_gelu(_dot(h, win(512, 1024))), vn_ref[...])
    bg = _dot(h, win(1024, 1536))
    p = _dot(h, win(1536, 2048)) * _dot(h, win(2048, 2560))
    return u, v, bg, p


def _even_p_body(x_ref, gpre_ref, winf_ref, vn_ref, sw_ref, sb_ref, cw_ref, woutf_ref, gpost_ref,
                 xo_ref, st_ref, win_ref, wout_ref, carry_ref, cat_ref, sh_ref, *, tm):
    @pl.when((pl.program_id(0) == 0) & (pl.program_id(1) == 0))
    def _():
        win_ref[...] = winf_ref[...].astype(BF16)
        wout_ref[...] = woutf_ref[...].astype(BF16)

    @pl.when(pl.program_id(1) == 0)
    def _():
        carry_ref[...] = jnp.zeros_like(carry_ref)

    x = x_ref[...]
    u, v, bg, p = _even_front(x, gpre_ref, lambda c0, c1: win_ref[:, c0:c1], vn_ref)
    vb = v.astype(BF16)
    ri = lax.broadcasted_iota(jnp.int32, (CHUNK, CHUNK), 0)
    ci = lax.broadcasted_iota(jnp.int32, (CHUNK, CHUNK), 1)
    for g in range(4):
        cols = slice(g * 128, (g + 1) * 128)
        w = jnp.where(ri >= ci, sw_ref[g], 0.0).astype(BF16)
        for c in range(tm // CHUNK):
            rows = slice(c * CHUNK, (c + 1) * CHUNK)
            mixed = _dot(w, vb[rows, cols]) + sb_ref[:, cols]
            cat_ref[rows, cols] = (u[rows, cols] * mixed).astype(BF16)
    cz = _conv_rows(p, carry_ref, cw_ref, slice(0, SC_W), sh_ref)
    cat_ref[:, 512:1024] = (bg * cz).astype(BF16)
    carry_ref[...] = p[tm - 8:tm, :]
    st_ref[...] = p[tm - 8:tm, :]
    out = _dot(cat_ref[...], wout_ref[...])
    xo_ref[...] = x + _rms(out, gpost_ref[...])


def _even_p_parts(tm):
    nt = SEQ // tm
    row_spec = pl.BlockSpec((tm, D), lambda b, t: (b * nt + t, 0))
    in_specs = [row_spec, _const((1, D)), _layer_const(0, (D, 2560)), _const((1, SGU_W)),
                _layer_const(0, (4, CHUNK, CHUNK)), _const((CHUNK, SGU_W)), _layer_const(0, (3, SC_W)),
                _layer_const(0, (D, D)), _const((1, D))]
    out_specs = [row_spec, pl.BlockSpec((None, 8, SC_W), lambda b, t: (b, 0, 0))]
    out_shape = [jax.ShapeDtypeStruct((NB * SEQ, D), F32), jax.ShapeDtypeStruct((NB, 8, SC_W), F32)]
    scratch = [pltpu.VMEM((D, 2560), BF16), pltpu.VMEM((D, D), BF16), pltpu.VMEM((8, SC_W), F32),
               pltpu.VMEM((tm, D), BF16), pltpu.VMEM((SC_W // 128, tm + 8, 128), F32)]
    return nt, in_specs, out_specs, out_shape, scratch


def _even_s_body(x_ref, gpre_ref, win_ref, vn_ref, gw_ref, gb_ref, cw_ref, st_ref, wout_ref, gpost_ref,
                 xo_ref, ns_ref, v_ref, pan_ref):
    xt = _load_time_major(x_ref, pan_ref)
    u, v, bg, p = _even_front(xt, gpre_ref, lambda c0, c1: win_ref[:, c0:c1].astype(BF16), vn_ref)
    _store_batch_major(v_ref, v, pan_ref)
    vt = [v[t * NS:(t + 1) * NS, :] for t in range(TS)]
    mixed = []
    for t in range(TS):
        m = gb_ref[t:t + 1, :] + gw_ref[t, 0:1, :] * vt[0]
        for s in range(1, t + 1):
            m = m + gw_ref[t, s:s + 1, :] * vt[s]
        mixed.append(m)
    a = u * jnp.concatenate(mixed, axis=0)
    prev = [jnp.concatenate([st_ref[j * 2 + r] for j in range(SC_W // 128)], axis=1) for r in range(2)]
    cz = _conv_time_major(p, prev, cw_ref)
    for j in range(SC_W // 128):
        for r in range(2):
            ns_ref[j * 2 + r] = p[(TS - 2 + r) * NS:(TS - 1 + r) * NS, j * 128:(j + 1) * 128]
    cat = jnp.concatenate([a, bg * cz], axis=1).astype(BF16)
    out = _dot(cat, wout_ref[...].astype(BF16))
    _store_batch_major(xo_ref, xt + _rms(out, gpost_ref[...]), pan_ref)


def _even_s(x, gpre, win, vn, gw, gb, cw, st, wout, gpost):
    nst = SC_W // 128 * 2
    return pl.pallas_call(
        _even_s_body,
        grid=(1,),
        in_specs=[_const((RS, D)), _const((1, D)), _layer_const(0, (D, 2560)), _const((1, SGU_W)),
                  _const((TS, TS, SGU_W)), _const((TS, SGU_W)), _layer_const(0, (3, SC_W)), _const((nst, NS, 128)),
                  _layer_const(0, (D, D)), _const((1, D))],
        out_specs=[_full((RS, D)), _full((nst, NS, 128)), _full((RS, SGU_W))],
        out_shape=[jax.ShapeDtypeStruct((RS, D), F32), jax.ShapeDtypeStruct((nst, NS, 128), F32),
                   jax.ShapeDtypeStruct((RS, SGU_W), F32)],
        scratch_shapes=[pltpu.VMEM((D // 128, RS, 128), F32)],
        compiler_params=_params(("arbitrary",), 48),
        name="even_s",
    )(x, gpre, win, vn, gw, gb, cw, st, wout, gpost)


def _softmax_rows(s):
    m = jnp.max(s, axis=-1, keepdims=True)
    e = jnp.exp(s - m)
    return e * (1.0 / jnp.sum(e, axis=-1, keepdims=True))


def _xattn_p_body(x_ref, gpre_ref, wqf_ref, k_ref, v_ref, wof_ref, gpost_ref, xo_ref, wq_ref, wo_ref, q_ref, s_ref,
                  p_ref, o_ref):
    @pl.when((pl.program_id(0) == 0) & (pl.program_id(1) == 0))
    def _():
        wq_ref[...] = wqf_ref[...].astype(BF16)
        wo_ref[...] = wof_ref[...].astype(BF16)

    ng, gr = q_ref.shape[0], q_ref.shape[1]
    for a in range(ng):
        rows = slice(a * gr, (a + 1) * gr)
        h = _rms(x_ref[rows, :], gpre_ref[...]).astype(BF16)
        q_ref[a] = _dot(h, wq_ref[...]).astype(BF16)
        for hd in range(XH):
            cols = slice(hd * XHD, (hd + 1) * XHD)
            s_ref[a, hd] = _dot_nt(q_ref[a, :, cols], k_ref[:, cols]) * (XHD ** -0.5)
        p_ref[a] = _softmax_rows(s_ref[a]).astype(BF16)
        for hd in range(XH):
            cols = slice(hd * XHD, (hd + 1) * XHD)
            o_ref[a, :, cols] = _dot(p_ref[a, hd], v_ref[:, cols]).astype(BF16)
        out = _dot(o_ref[a], wo_ref[...])
        xo_ref[rows, :] = x_ref[rows, :] + _rms(out, gpost_ref[...])


def _xattn_p_parts(l, tm, groups=1):
    nt = SEQ // tm
    gr = tm // groups
    row_spec = pl.BlockSpec((tm, D), lambda b, t: (b * nt + t, 0))
    kv_spec = pl.BlockSpec((None, N_MEM, D), lambda b, t: (l, b, 0))
    in_specs = [row_spec, _const((1, D)), _layer_const(l, (D, D)), kv_spec, kv_spec, _layer_const(l, (D, D)),
                _const((1, D))]
    scratch = [pltpu.VMEM((D, D), BF16), pltpu.VMEM((D, D), BF16), pltpu.VMEM((groups, gr, D), BF16),
               pltpu.VMEM((groups, XH, gr, N_MEM), F32), pltpu.VMEM((groups, XH, gr, N_MEM), BF16),
               pltpu.VMEM((groups, gr, D), BF16)]
    return nt, row_spec, in_specs, scratch


def _xattn_p(l, x, gpre, wq, kb, vb, wo, gpost):
    nt, row_spec, in_specs, scratch = _xattn_p_parts(l, 1024, groups=2)
    return pl.pallas_call(
        _xattn_p_body,
        grid=(NB, nt),
        in_specs=in_specs,
        out_specs=row_spec,
        out_shape=jax.ShapeDtypeStruct((NB * SEQ, D), F32),
        scratch_shapes=scratch,
        compiler_params=_params(("arbitrary", "arbitrary"), 48),
        name="xattn_p",
    )(x, gpre, wq, kb, vb, wo, gpost)


def _head_view(c):
    s = c.shape
    return c.reshape(s[0], s[1], N_MEM, XH, 2, 128).transpose(0, 1, 2, 4, 3, 5).reshape(s[0], s[1], N_MEM * 8, 128)


def _head_rows(ref, b, hd):
    halves = [ref[b, pl.ds(j * XH + hd, N_MEM, stride=8), :] for j in range(2)]
    return jnp.concatenate(halves, axis=1).astype(BF16)


def _xattn_s_body(x_ref, gpre_ref, wq_ref, k_ref, v_ref, wo_ref, gpost_ref, xo_ref, q_ref, o_ref, s_ref, *, bb):
    _xattn_s_step(pl.program_id(0), pl.num_programs(0), x_ref, gpre_ref, wq_ref, k_ref, v_ref, wo_ref, gpost_ref,
                  xo_ref, q_ref, o_ref, s_ref, bb=bb)


def _xattn_s_step(step, nsteps, x_ref, gpre_ref, wq_ref, k_ref, v_ref, wo_ref, gpost_ref, xo_ref, q_ref, o_ref, s_ref,
                  *, bb):
    @pl.when(step == 0)
    def _():
        h = _rms(x_ref[...], gpre_ref[...]).astype(BF16)
        q_ref[...] = _dot(h, wq_ref[...].astype(BF16))

    first = (lax.broadcasted_iota(jnp.int32, (8, XHD), 0) >> 2) == 0
    groups = [(pi, hd, bi) for pi in range(bb // 2) for hd in range(XH) for bi in range(2)]
    row0 = [pl.multiple_of((step * (bb // 2) + pi) * 8, 8) for pi in range(bb // 2)]
    for gi, (pi, hd, bi) in enumerate(groups):
        q8 = q_ref[pl.ds(row0[pi], 8), hd * XHD:(hd + 1) * XHD].astype(BF16)
        kh = _head_rows(k_ref, pi * 2 + bi, hd)
        s_ref[gi * 8:(gi + 1) * 8, :] = _dot_nt(q8, kh) * (XHD ** -0.5)
    s_ref[...] = _softmax_rows(s_ref[...])
    for gi, (pi, hd, bi) in enumerate(groups):
        if bi == 1:
            continue
        pv = [_dot(s_ref[(gi + b) * 8:(gi + b + 1) * 8, :].astype(BF16), _head_rows(v_ref, pi * 2 + b, hd))
              for b in range(2)]
        o_ref[pl.ds(row0[pi], 8), hd * XHD:(hd + 1) * XHD] = jnp.where(first, pv[0], pv[1])

    @pl.when(step == nsteps - 1)
    def _():
        out = _dot(o_ref[...].astype(BF16), wo_ref[...].astype(BF16))
        xo_ref[...] = x_ref[...] + _rms(out, gpost_ref[...])


def _xattn_ps_body(x_ref, gpre_ref, wqf_ref, k_ref, v_ref, wof_ref, gpost_ref,
                   xs_ref, gpres_ref, ks_ref, vs_ref, gposts_ref,
                   xo_ref, xso_ref,
                   wq_ref, wo_ref, q_ref, s_ref, p_ref, o_ref, qs_ref, os_ref, ss_ref, *, bb):
    _xattn_p_body(x_ref, gpre_ref, wqf_ref, k_ref, v_ref, wof_ref, gpost_ref, xo_ref, wq_ref, wo_ref, q_ref, s_ref,
                  p_ref, o_ref)
    step = pl.program_id(0) * pl.num_programs(1) + pl.program_id(1)
    _xattn_s_step(step, pl.num_programs(0) * pl.num_programs(1), xs_ref, gpres_ref, wq_ref, ks_ref, vs_ref, wo_ref,
                  gposts_ref, xso_ref, qs_ref, os_ref, ss_ref, bb=bb)


def _xattn_ps(l, x, gpre, wq, kb, vb, wo, gpost, xs, gpres, ks, vs, gposts):
    bb = 2
    steps = NS // bb
    tm = NB * SEQ // steps
    nt, row_spec, h_in, h_scratch = _xattn_p_parts(l, tm)
    kv_spec = pl.BlockSpec((None, bb, N_MEM * 8, 128), lambda b, t: (l, b * nt + t, 0, 0))
    return pl.pallas_call(
        functools.partial(_xattn_ps_body, bb=bb),
        grid=(NB, nt),
        in_specs=h_in + [_const((RS, D)), _const((1, D)), kv_spec, kv_spec, _const((1, D))],
        out_specs=[row_spec, _full((RS, D))],
        out_shape=[jax.ShapeDtypeStruct((NB * SEQ, D), F32), jax.ShapeDtypeStruct((RS, D), F32)],
        scratch_shapes=h_scratch + [pltpu.VMEM((RS, D), F32), pltpu.VMEM((RS, D), F32),
                                    pltpu.VMEM((bb * XH * 8, N_MEM), F32)],
        compiler_params=_params(("arbitrary", "arbitrary"), 56),
        name="xattn_ps",
    )(x, gpre, wq, kb, vb, wo, gpost, xs, gpres, ks, vs, gposts)


def _xattn_s(l, x, gpre, wq, k, v, wo, gpost):
    bb = 4
    kv_spec = pl.BlockSpec((None, bb, N_MEM * 8, 128), lambda i: (l, i, 0, 0))
    return pl.pallas_call(
        functools.partial(_xattn_s_body, bb=bb),
        grid=(NS // bb,),
        in_specs=[_const((RS, D)), _const((1, D)), _layer_const(l, (D, D)), kv_spec, kv_spec, _layer_const(l, (D, D)),
                  _const((1, D))],
        out_specs=_full((RS, D)),
        out_shape=jax.ShapeDtypeStruct((RS, D), F32),
        scratch_shapes=[pltpu.VMEM((RS, D), F32), pltpu.VMEM((RS, D), F32), pltpu.VMEM((bb * XH * 8, N_MEM), F32)],
        compiler_params=_params(("arbitrary",), 48),
        name="xattn_s",
    )(x, gpre, wq, k, v, wo, gpost)


def _ffn_p_body(x_ref, xn_ref, z_ref, gpre_ref, wg_ref, wv_ref, cw_ref, wdn_ref, gpost_ref, xo_ref, st_ref, carry_ref,
                h_ref, act_ref, *sh_refs, tm):
    step = pl.program_id(0) * pl.num_programs(1) + pl.program_id(1)
    cur = step % 2

    @pl.when(step == 0)
    def _():
        h_ref[0] = _rms(x_ref[...], gpre_ref[...]).astype(BF16)

    @pl.when(pl.program_id(1) == 0)
    def _():
        carry_ref[...] = jnp.zeros_like(carry_ref)

    for c in range(NFC):
        halves = []
        for i, (base, w_ref) in enumerate(((0, wg_ref), (DFF, wv_ref))):
            cols = slice(base + c * FC, base + (c + 1) * FC)
            up = _dot(h_ref[cur], w_ref[:, c * FC:(c + 1) * FC])
            halves.append(_conv_rows(up, carry_ref, cw_ref, cols, sh_refs[(2 * c + i) % len(sh_refs)]))
            carry_ref[:, cols] = up[tm - 8:tm, :]
            st_ref[:, cols] = up[tm - 8:tm, :]
        act_ref[:, c * FC:(c + 1) * FC] = (_gelu(halves[0]) * halves[1]).astype(BF16)
    hn = _rms(xn_ref[...], gpre_ref[...])
    h_ref[1 - cur] = hn.astype(BF16)
    zero = lax.bitcast_convert_type(
        lax.bitcast_convert_type(jnp.max(hn, axis=0, keepdims=True), jnp.int32) & z_ref[...], F32)
    out = _dot(act_ref[...], wdn_ref[...]) + zero
    xo_ref[...] = x_ref[...] + _rms(out, gpost_ref[...])


def _ffn_p_parts(l, tm):
    nt = SEQ // tm
    row_spec = pl.BlockSpec((tm, D), lambda b, t: (b * nt + t, 0))
    next_spec = pl.BlockSpec((tm, D), lambda b, t: (jnp.minimum(b * nt + t + 1, NB * nt - 1), 0))
    in_specs = [row_spec, next_spec, _const((1, D)), _const((1, D)), _const((D, DFF)), _const((D, DFF)),
                _layer_const(l, (3, 2 * DFF)), _const((DFF, D)), _const((1, D))]
    out_specs = [row_spec, pl.BlockSpec((None, 8, 2 * DFF), lambda b, t: (b, 0, 0))]
    out_shape = [jax.ShapeDtypeStruct((NB * SEQ, D), F32), jax.ShapeDtypeStruct((NB, 8, 2 * DFF), F32)]
    scratch = ([pltpu.VMEM((8, 2 * DFF), F32), pltpu.VMEM((2, tm, D), BF16), pltpu.VMEM((tm, DFF), BF16)]
               + [pltpu.VMEM((FC // 128, tm + 8, 128), F32)] * 4)
    return nt, in_specs, out_specs, out_shape, scratch


def _ffn_p(l, x, gpre, wg, wv, cw, wdn, gpost):
    tm = 512
    nt, in_specs, out_specs, out_shape, scratch = _ffn_p_parts(l, tm)
    return pl.pallas_call(
        functools.partial(_ffn_p_body, tm=tm),
        grid=(NB, nt),
        in_specs=in_specs,
        out_specs=out_specs,
        out_shape=out_shape,
        scratch_shapes=scratch,
        compiler_params=_params(("arbitrary", "arbitrary"), 56),
        name="ffn_p",
    )(x, x, jnp.zeros((1, D), jnp.int32), gpre, wg, wv, cw, wdn, gpost)


def _load_time_major(x_ref, pan_ref):
    n = x_ref.shape[1] // 128
    for p in range(n):
        pan_ref[p] = x_ref[:, p * 128:(p + 1) * 128]
    return jnp.concatenate(
        [jnp.concatenate([pan_ref[p, pl.ds(t, NS, stride=TS), :] for p in range(n)], axis=1) for t in range(TS)],
        axis=0)


def _store_batch_major(o_ref, y, pan_ref):
    n = o_ref.shape[1] // 128
    for p in range(n):
        for t in range(TS):
            pan_ref[p, pl.ds(t, NS, stride=TS), :] = y[t * NS:(t + 1) * NS, p * 128:(p + 1) * 128]
    for p in range(n):
        o_ref[:, p * 128:(p + 1) * 128] = pan_ref[p]


def _conv_time_major(u, prev, w_ref):
    blocks = list(prev) + [u[t * NS:(t + 1) * NS, :] for t in range(TS)]
    w0, w1, w2 = w_ref[0:1, :], w_ref[1:2, :], w_ref[2:3, :]
    return jnp.concatenate([(w0 * blocks[t] + w1 * blocks[t + 1]) + w2 * blocks[t + 2] for t in range(TS)], axis=0)


def _ffn_s_body(x_ref, gpre_ref, wg_ref, wv_ref, cg_ref, cv_ref, sg_ref, sv_ref, wdn_ref, gpost_ref,
                xo_ref, ng_ref, nv_ref, wgb_ref, wvb_ref, wdb_ref, xt_ref, h_ref, acc_ref, pan_ref):
    c = pl.program_id(0)

    @pl.when(c == 0)
    def _():
        xt = _load_time_major(x_ref, pan_ref)
        xt_ref[...] = xt
        h_ref[...] = _rms(xt, gpre_ref[...]).astype(BF16)
        acc_ref[...] = jnp.zeros_like(acc_ref)

    wgb_ref[...] = wg_ref[...].astype(BF16)
    wvb_ref[...] = wv_ref[...].astype(BF16)
    wdb_ref[...] = wdn_ref[...].astype(BF16)
    ys = []
    for wb_ref, cw_ref, s_ref, n_ref in ((wgb_ref, cg_ref, sg_ref, ng_ref), (wvb_ref, cv_ref, sv_ref, nv_ref)):
        u = _dot(h_ref[...], wb_ref[...])
        ys.append(_conv_time_major(u, [s_ref[:, r, :] for r in range(2)], cw_ref))
        for r in range(2):
            n_ref[:, r, :] = u[(TS - 2 + r) * NS:(TS - 1 + r) * NS, :]
    act = (_gelu(ys[0]) * ys[1]).astype(BF16)
    acc_ref[...] += _dot(act, wdb_ref[...])

    @pl.when(c == pl.num_programs(0) - 1)
    def _():
        _store_batch_major(xo_ref, xt_ref[...] + _rms(acc_ref[...], gpost_ref[...]), pan_ref)


def _ffn_s(l, x, gpre, wup, cw, st, wdn, gpost):
    gate = lambda *lead: pl.BlockSpec((None,) + lead + (FC,), lambda c: (l,) + (0,) * len(lead) + (c,))
    val = lambda *lead: pl.BlockSpec((None,) + lead + (FC,), lambda c: (l,) + (0,) * len(lead) + (NFC + c,))
    s_out = pl.BlockSpec((NS, 2, FC), lambda c: (0, 0, c))
    wb_out = pl.BlockSpec((D, FC), lambda c: (0, c))
    return pl.pallas_call(
        _ffn_s_body,
        grid=(NFC,),
        in_specs=[_const((RS, D)), _const((1, D)), gate(D), val(D), gate(3), val(3), gate(NS, 2), val(NS, 2),
                  pl.BlockSpec((None, FC, D), lambda c: (l, c, 0)), _const((1, D))],
        out_specs=[_full((RS, D)), s_out, s_out, wb_out, wb_out, pl.BlockSpec((FC, D), lambda c: (c, 0))],
        out_shape=[jax.ShapeDtypeStruct((RS, D), F32),
                   jax.ShapeDtypeStruct((NS, 2, DFF), F32), jax.ShapeDtypeStruct((NS, 2, DFF), F32),
                   jax.ShapeDtypeStruct((D, DFF), BF16), jax.ShapeDtypeStruct((D, DFF), BF16),
                   jax.ShapeDtypeStruct((DFF, D), BF16)],
        scratch_shapes=[pltpu.VMEM((RS, D), F32), pltpu.VMEM((RS, D), BF16), pltpu.VMEM((RS, D), F32),
                        pltpu.VMEM((D // 128, RS, 128), F32)],
        compiler_params=_params(("arbitrary",), 40),
        name="ffn_s",
    )(x, gpre, wup, wup, cw, cw, st, st, wdn, gpost)


def _rope_head(z, cos, sin, base):
    x1 = z[:, base:base + 128]
    x2 = z[:, base + 128:base + 256]
    return x1 * cos - x2 * sin, x1 * sin + x2 * cos


def _group_norm_gate(o, g):
    mu = jnp.mean(o, axis=-1, keepdims=True)
    d = o - mu
    var = jnp.mean(d * d, axis=-1, keepdims=True)
    return (g * jax.nn.sigmoid(g)) * (d * lax.rsqrt(var + EPS))


RCH = 256


def _ret_p_body(x_ref, cos_ref, sin_ref, gpre_ref, wq_ref, wk_ref, wv_ref, wg_ref, wout_ref, gpost_ref, xo_ref, s_ref,
                h_ref, q_ref, k_ref, kd_ref, v_ref, g_ref, y_ref, dm_ref, qd_ref, kdec_ref, *, tm):
    first = (pl.program_id(0) == 0) & (pl.program_id(1) == 0)

    @pl.when(first)
    def _():
        i = lax.broadcasted_iota(jnp.int32, (RCH, RCH), 0).astype(F32)
        j = lax.broadcasted_iota(jnp.int32, (RCH, RCH), 1).astype(F32)
        i1 = lax.broadcasted_iota(jnp.int32, (RCH, 128), 0).astype(F32)
        for hd in range(RET_H):
            lg = _LOG_G[hd]
            dm_ref[hd] = jnp.where(i >= j, jnp.exp(lg * jnp.maximum(i - j, 0.0)), 0.0)
            qd_ref[hd] = jnp.exp(lg * (i1 + 1.0))
            kdec_ref[hd] = jnp.exp(lg * (RCH - 1.0 - i1))

    @pl.when(pl.program_id(1) == 0)
    def _():
        s_ref[...] = jnp.zeros_like(s_ref)

    h_ref[...] = _rms(x_ref[...], gpre_ref[...]).astype(BF16)
    cos = cos_ref[...]
    sin = sin_ref[...]
    nchunk = tm // RCH
    for hd in range(RET_H):
        base = hd * RET_DK
        zq = _dot(h_ref[...], wq_ref[:, base:base + RET_DK])
        zk = _dot(h_ref[...], wk_ref[:, base:base + RET_DK])
        q1, q2 = _rope_head(zq, cos, sin, 0)
        k1, k2 = _rope_head(zk, cos, sin, 0)
        q_ref[:, base:base + 128] = (q1 * (RET_DK ** -0.5)).astype(BF16)
        q_ref[:, base + 128:base + 256] = (q2 * (RET_DK ** -0.5)).astype(BF16)
        k_ref[:, base:base + 128] = k1.astype(BF16)
        k_ref[:, base + 128:base + 256] = k2.astype(BF16)
        kdec = jnp.concatenate([kdec_ref[hd]] * nchunk, axis=0)
        kd_ref[:, base:base + 128] = k1 * kdec
        kd_ref[:, base + 128:base + 256] = k2 * kdec
        vcols = slice(hd * RET_DV, (hd + 1) * RET_DV)
        v_ref[:, vcols] = _dot(h_ref[...], wv_ref[:, vcols]).astype(BF16)
        g_ref[:, vcols] = _dot(h_ref[...], wg_ref[:, vcols])
    for hd in range(RET_H):
        kcols = slice(hd * RET_DK, (hd + 1) * RET_DK)
        vcols = slice(hd * RET_DV, (hd + 1) * RET_DV)
        decay_l = float(np.exp(np.float32(_LOG_G[hd]) * np.float32(RCH)))
        qd = qd_ref[hd]
        qd4 = jnp.concatenate([qd, qd, qd, qd], axis=1)
        for c in range(nchunk):
            rows = slice(c * RCH, (c + 1) * RCH)
            qc = q_ref[rows, kcols]
            vc = v_ref[rows, vcols]
            s_prev = s_ref[hd]
            sc = (_dot_nt(qc, k_ref[rows, kcols]) * dm_ref[hd]).astype(BF16)
            o = _dot(sc, vc) + _dot(qc, s_prev.astype(BF16)) * qd4
            kdt = kd_ref[rows, kcols].T.astype(BF16)
            s_ref[hd] = decay_l * s_prev + _dot(kdt, vc)
            y_ref[rows, vcols] = _group_norm_gate(o, g_ref[rows, vcols]).astype(BF16)
    out = _dot(y_ref[...], wout_ref[...])
    xo_ref[...] = x_ref[...] + _rms(out, gpost_ref[...])


def _ret_p(x, cos, sin, gpre, wq, wk, wv, wg, wout, gpost):
    tm = 512
    nt = SEQ // tm
    row_spec = pl.BlockSpec((tm, D), lambda b, t: (b * nt + t, 0))
    cs_spec = pl.BlockSpec((tm, 128), lambda b, t: (t, 0))
    return pl.pallas_call(
        functools.partial(_ret_p_body, tm=tm),
        grid=(NB, nt),
        in_specs=[row_spec, cs_spec, cs_spec, _const((1, D)), _const((D, NQ)), _const((D, NQ)), _const((D, NV)),
                  _const((D, NV)), _const((NV, D)), _const((1, D))],
        out_specs=[row_spec, pl.BlockSpec((None, RET_H, RET_DK, RET_DV), lambda b, t: (b, 0, 0, 0))],
        out_shape=[jax.ShapeDtypeStruct((NB * SEQ, D), F32), jax.ShapeDtypeStruct((NB, RET_H, RET_DK, RET_DV), F32)],
        scratch_shapes=[pltpu.VMEM((tm, D), BF16), pltpu.VMEM((tm, NQ), BF16), pltpu.VMEM((tm, NQ), BF16),
                        pltpu.VMEM((tm, NQ), F32), pltpu.VMEM((tm, NV), BF16), pltpu.VMEM((tm, NV), F32),
                        pltpu.VMEM((tm, NV), BF16), pltpu.VMEM((RET_H, RCH, RCH), F32),
                        pltpu.VMEM((RET_H, RCH, 128), F32), pltpu.VMEM((RET_H, RCH, 128), F32)],
        compiler_params=_params(("arbitrary", "arbitrary"), 56),
        name="ret_p",
    )(x, cos, sin, gpre, wq, wk, wv, wg, wout, gpost)


def _ret_s_proj_body(x_ref, cos_ref, sin_ref, gpre_ref, wqf_ref, wkf_ref, wvf_ref, wgf_ref,
                     q_ref, kdt_ref, v_ref, g_ref, oi_ref, wq_ref, wk_ref, wv_ref, wg_ref, h_ref):
    hd = pl.program_id(0)

    @pl.when(hd == 0)
    def _():
        h_ref[...] = _rms(x_ref[...], gpre_ref[...]).astype(BF16)

    for wf_ref, wb_ref in ((wqf_ref, wq_ref), (wkf_ref, wk_ref), (wvf_ref, wv_ref), (wgf_ref, wg_ref)):
        wb_ref[...] = wf_ref[...].astype(BF16)
    lg = jnp.where(hd == 0, _LOG_G[0], jnp.where(hd == 1, _LOG_G[1], jnp.where(hd == 2, _LOG_G[2], _LOG_G[3])))
    h = h_ref[...]
    cos = cos_ref[...]
    sin = sin_ref[...]
    ti = (lax.broadcasted_iota(jnp.int32, (RS, 128), 0) & (TS - 1)).astype(F32)
    i = lax.broadcasted_iota(jnp.int32, (CHUNK, CHUNK), 0)
    j = lax.broadcasted_iota(jnp.int32, (CHUNK, CHUNK), 1)
    same = ((i >> 2) == (j >> 2)) & (i >= j)
    dm = jnp.where(same, jnp.exp(lg * jnp.maximum(i - j, 0).astype(F32)), 0.0)
    q1, q2 = _rope_head(_dot(h, wq_ref[...]), cos, sin, 0)
    k1, k2 = _rope_head(_dot(h, wk_ref[...]), cos, sin, 0)
    q = jnp.concatenate([q1, q2], axis=1) * (RET_DK ** -0.5)
    k = jnp.concatenate([k1, k2], axis=1)
    q_ref[...] = q
    kdec = jnp.exp(lg * (TS - 1.0 - ti))
    kd = k * jnp.concatenate([kdec, kdec], axis=1)
    vf = _dot(h, wv_ref[...])
    v_ref[...] = vf
    v = vf.astype(BF16)
    g_ref[...] = _dot(h, wg_ref[...])
    qb = q.astype(BF16)
    kb = k.astype(BF16)
    for tl in range(RS // CHUNK):
        rows = slice(tl * CHUNK, (tl + 1) * CHUNK)
        sc = (_dot_nt(qb[rows], kb[rows]) * dm).astype(BF16)
        oi_ref[rows, :] = _dot(sc, v[rows])
        kdt_ref[tl] = kd[rows].T.astype(BF16)


def _ret_s_proj(x, cos, sin, gpre, win):
    def wcol(width, first_block):
        return pl.BlockSpec((None, D, width), lambda h: (0, 0, first_block + h))
    def hcol(rows, width):
        return pl.BlockSpec((rows, width), lambda h: (0, h))
    return pl.pallas_call(
        _ret_s_proj_body,
        grid=(RET_H,),
        in_specs=[_const((RS, D)), _const((RS, 128)), _const((RS, 128)), _const((1, D)),
                  wcol(RET_DK, 0), wcol(RET_DK, NQ // RET_DK), wcol(RET_DV, 2 * NQ // RET_DV),
                  wcol(RET_DV, (2 * NQ + NV) // RET_DV)],
        out_specs=[hcol(RS, RET_DK),
                   pl.BlockSpec((RS // CHUNK, None, RET_DK, CHUNK), lambda h: (0, h, 0, 0)),
                   hcol(RS, RET_DV), hcol(RS, RET_DV), hcol(RS, RET_DV),
                   hcol(D, RET_DK), hcol(D, RET_DK), hcol(D, RET_DV), hcol(D, RET_DV)],
        out_shape=[jax.ShapeDtypeStruct((RS, NQ), F32),
                   jax.ShapeDtypeStruct((RS // CHUNK, RET_H, RET_DK, CHUNK), BF16),
                   jax.ShapeDtypeStruct((RS, NV), F32), jax.ShapeDtypeStruct((RS, NV), F32),
                   jax.ShapeDtypeStruct((RS, NV), F32),
                   jax.ShapeDtypeStruct((D, NQ), BF16), jax.ShapeDtypeStruct((D, NQ), BF16),
                   jax.ShapeDtypeStruct((D, NV), BF16), jax.ShapeDtypeStruct((D, NV), BF16)],
        scratch_shapes=[pltpu.VMEM((RS, D), BF16)],
        compiler_params=_params(("arbitrary",), 40),
        name="ret_s_proj",
    )(x, cos, sin, gpre, win, win, win, win)


def _ret_s_core_body(step, q_ref, kdt_ref, v_ref, oi_ref, s_ref, o_ref, so_ref):
    pair_in_tile = step % (CHUNK // 8)
    row8 = lax.broadcasted_iota(jnp.int32, (8, RET_DK), 0)
    row128 = lax.broadcasted_iota(jnp.int32, (CHUNK, RET_DV), 0)
    t8 = (lax.broadcasted_iota(jnp.int32, (8, RET_DV), 0) & (TS - 1)).astype(F32)
    for hd in range(RET_H):
        lg = _LOG_G[hd]
        decay_l = float(np.exp(np.float32(lg) * np.float32(TS)))
        q8 = q_ref[:, hd * RET_DK:(hd + 1) * RET_DK]
        v128 = v_ref[:, hd * RET_DV:(hd + 1) * RET_DV]
        kdt = kdt_ref[hd]
        inter = jnp.zeros((8, RET_DV), F32)
        for bi in range(2):
            s_prev = s_ref[bi, hd]
            qm = jnp.where((row8 >> 2) == bi, q8, 0.0).astype(BF16)
            inter = inter + _dot(qm, s_prev.astype(BF16))
            vm = jnp.where((row128 >> 2) == pair_in_tile * 2 + bi, v128, 0.0).astype(BF16)
            so_ref[bi, hd] = decay_l * s_prev + _dot(kdt, vm)
        cols = slice(hd * RET_DV, (hd + 1) * RET_DV)
        o_ref[:, cols] = oi_ref[:, cols] + inter * jnp.exp(lg * (t8 + 1.0))


def _ret_s_core_parts():
    ppt = CHUNK // 8
    s_spec = pl.BlockSpec((2, RET_H, RET_DK, RET_DV), lambda i: (i, 0, 0, 0))
    in_specs = [pl.BlockSpec((8, NQ), lambda i: (i, 0)),
                pl.BlockSpec((None, RET_H, RET_DK, CHUNK), lambda i: (i // ppt, 0, 0, 0)),
                pl.BlockSpec((CHUNK, NV), lambda i: (i // ppt, 0)),
                pl.BlockSpec((8, NV), lambda i: (i, 0)),
                s_spec]
    out_specs = [pl.BlockSpec((8, NV), lambda i: (i, 0)), s_spec]
    out_shape = [jax.ShapeDtypeStruct((RS, NV), F32), jax.ShapeDtypeStruct((NS, RET_H, RET_DK, RET_DV), F32)]
    return in_specs, out_specs, out_shape


def _host_ret_s_body(*refs, host_body, n_in, n_out):
    host_in, guest_in = refs[:n_in], refs[n_in:n_in + 5]
    k = n_in + 5
    host_out, guest_out, scratch = refs[k:k + n_out], refs[k + n_out:k + n_out + 2], refs[k + n_out + 2:]
    host_body(*host_in, *host_out, *scratch)
    step = pl.program_id(0) * pl.num_programs(1) + pl.program_id(1)
    _ret_s_core_body(step, *guest_in, *guest_out)


def _with_ret_s(name, host_body, nt, h_in, h_out, h_shape, scratch, host_args, guest_args):
    g_in, g_out, g_shape = _ret_s_core_parts()

    def on_grid(spec):
        return pl.BlockSpec(spec.block_shape, lambda b, t, m=spec.index_map: m(b * nt + t))

    return pl.pallas_call(
        functools.partial(_host_ret_s_body, host_body=host_body, n_in=len(h_in), n_out=len(h_out)),
        grid=(NB, nt),
        in_specs=h_in + [on_grid(sp) for sp in g_in],
        out_specs=h_out + [on_grid(sp) for sp in g_out],
        out_shape=h_shape + g_shape,
        scratch_shapes=scratch,
        compiler_params=_params(("arbitrary", "arbitrary"), 56),
        name=name,
    )(*host_args, *guest_args)


def _ret_s_out_body(x_ref, o_ref, g_ref, woutf_ref, gpost_ref, xo_ref, wout_ref, y_ref):
    wout_ref[...] = woutf_ref[...].astype(BF16)
    for hd in range(RET_H):
        cols = slice(hd * RET_DV, (hd + 1) * RET_DV)
        y_ref[:, cols] = _group_norm_gate(o_ref[:, cols], g_ref[:, cols]).astype(BF16)
    out = _dot(y_ref[...], wout_ref[...])
    xo_ref[...] = x_ref[...] + _rms(out, gpost_ref[...])


def _ret_s_out(x, o, g, wout, gpost):
    return pl.pallas_call(
        _ret_s_out_body,
        grid=(1,),
        in_specs=[_const((RS, D)), _const((RS, NV)), _const((RS, NV)), _layer_const(0, (NV, D)), _const((1, D))],
        out_specs=[_full((RS, D)), _full((NV, D))],
        out_shape=[jax.ShapeDtypeStruct((RS, D), F32), jax.ShapeDtypeStruct((NV, D), BF16)],
        scratch_shapes=[pltpu.VMEM((RS, NV), BF16)],
        compiler_params=_params(("arbitrary",), 48),
        name="ret_s_out",
    )(x, o, g, wout, gpost)


def _rope_tables(pos):
    half = RET_DK // 2
    inv = ROPE_BASE ** (-jnp.arange(half, dtype=F32) / half)
    ang = pos.astype(F32)[:, None] * inv[None, :]
    return jnp.cos(ang), jnp.sin(ang)


def _state_rows(st):
    lead, c = st.shape[:-3], st.shape[-1]
    n = len(lead)
    s = st.reshape(lead + (NS, 2, c // 128, 128))
    s = s.transpose(tuple(range(n)) + (n + 2, n + 1, n, n + 3))
    return s.reshape(lead + (c // 128 * 2, NS, 128))


def _from_state_rows(s):
    tiles = s.shape[0] // 2
    return s.reshape(tiles, 2, NS, 128).transpose(2, 1, 0, 3).reshape(NS, 2, tiles * 128)


def kernel(x_prompt, x_sample, mem_prompt, cache_mem_k, cache_mem_v, state_shortconv, state_retention, state_ffn_conv, norm_mix_pre, norm_mix_post, w_in_even, sgu_vnorm, sgu_w, sgu_b, conv_short, w_out_even, w_in_odd, w_out_odd, norm_x_pre, norm_x_post, norm_mem, w_xq, w_xk, w_xv, w_xo, norm_ffn_pre, norm_ffn_post, w_ffn_up, conv_ffn, w_ffn_down):
    row = lambda g: g.reshape(1, -1)

    xs = x_sample.reshape(RS, D)
    w4 = jnp.tril(sgu_w[0][:, :TS, :TS])
    gw = jnp.repeat(w4.transpose(1, 2, 0), CHUNK, axis=2)
    gb = jnp.repeat(sgu_b[0][:, :TS].T, CHUNK, axis=1)
    xs, sc_s, v_s = _even_s(xs, row(norm_mix_pre[0]), w_in_even, row(sgu_vnorm[0]), gw, gb, conv_short,
                            _state_rows(state_shortconv[0]), w_out_even, row(norm_mix_post[0]))
    cos_s, sin_s = _rope_tables(PAST + (jnp.arange(RS, dtype=jnp.int32) % TS))
    cache_k, cache_v = _head_view(cache_mem_k), _head_view(cache_mem_v)
    ffn_s_states, ffn_p_states, ffn_w = [], [], []

    def sample_ffn(l, xs):
        xs, ng, nv, wg_b, wv_b, wd_b = _ffn_s(l, xs, row(norm_ffn_pre[l]), w_ffn_up, conv_ffn, state_ffn_conv,
                                              w_ffn_down, row(norm_ffn_post[l]))
        ffn_s_states.append(jnp.concatenate([ng, nv], axis=-1))
        ffn_w.append((wg_b, wv_b, wd_b))
        return xs

    def prompt_ffn(l, xp):
        xp, st = _ffn_p(l, xp, row(norm_ffn_pre[l]), ffn_w[l][0], ffn_w[l][1], conv_ffn, ffn_w[l][2],
                        row(norm_ffn_post[l]))
        ffn_p_states.append(st[:, 6:8, :])
        return xp

    xs = _xattn_s(0, xs, row(norm_x_pre[0]), w_xq, cache_k, cache_v, w_xo, row(norm_x_post[0]))
    xs = sample_ffn(0, xs)
    q, kdt, v, g, oi, wq_o, wk_o, wv_o, wg_o = _ret_s_proj(xs, cos_s, sin_s, row(norm_mix_pre[1]), w_in_odd)

    mem_k, mem_v, mem_kb, mem_vb = _memkv(mem_prompt.reshape(NB * N_MEM, D), norm_mem.reshape(2, 1, D), w_xk, w_xv)
    xp = x_prompt.reshape(NB * SEQ, D)
    sb_full = jnp.repeat(sgu_b[0].T, CHUNK, axis=1)
    tm = NB * SEQ // (NS // 2)
    nt, e_in, e_out, e_shape, e_scratch = _even_p_parts(tm)
    xp, sc_p, o, ret_s_state = _with_ret_s(
        "even_p_ret_s", functools.partial(_even_p_body, tm=tm), nt, e_in, e_out, e_shape, e_scratch,
        (xp, row(norm_mix_pre[0]), w_in_even, row(sgu_vnorm[0]), sgu_w, sb_full, conv_short, w_out_even,
         row(norm_mix_post[0])), (q, kdt, v, oi, state_retention[0]))
    xs, wout_o = _ret_s_out(xs, o, g, w_out_odd, row(norm_mix_post[1]))

    xp = _xattn_p(0, xp, row(norm_x_pre[0]), w_xq, mem_kb, mem_vb, w_xo, row(norm_x_post[0]))
    xp = prompt_ffn(0, xp)
    cos_p, sin_p = _rope_tables(jnp.arange(SEQ, dtype=jnp.int32))
    xp, ret_p_state = _ret_p(xp, cos_p, sin_p, row(norm_mix_pre[1]), wq_o, wk_o, wv_o, wg_o, wout_o,
                             row(norm_mix_post[1]))
    xp, xs = _xattn_ps(1, xp, row(norm_x_pre[1]), w_xq, mem_kb, mem_vb, w_xo, row(norm_x_post[1]),
                       xs, row(norm_x_pre[1]), cache_k, cache_v, row(norm_x_post[1]))
    xs = sample_ffn(1, xs)
    xp = prompt_ffn(1, xp)

    return (xp.reshape(NB, SEQ, D), xs.reshape(NS, TS, D),
            _from_head_view(mem_k), _from_head_view(mem_v),
            sc_p[None, :, 6:8, :], _from_state_rows(sc_s)[None],
            v_s.reshape(1, NS, TS, SGU_W),
            ret_p_state[None], ret_s_state[None],
            jnp.stack(ffn_p_states), jnp.stack(ffn_s_states))
```

```python
import functools

import numpy as np
import jax
import jax.numpy as jnp
from jax import lax
from jax.experimental import pallas as pl
from jax.experimental.pallas import tpu as pltpu

F32 = jnp.float32
BF16 = jnp.bfloat16

D = 1024
SEQ = 2048
NB = 8
NS = 128
TS = 4
RS = NS * TS
PAST = 16384
CHUNK = 128
EPS = 1e-6
SGU_W = 512
SC_W = 512
RET_H = 4
RET_DK = 256
RET_DV = 512
NQ = RET_H * RET_DK
NV = RET_H * RET_DV
N_MEM = 256
XH = 4
XHD = 256
DFF = 2816
FC = 256
NFC = DFF // FC
ROPE_BASE = 10000.0

_LOG_G = [float(v) for v in np.log1p(-np.exp2(np.float32(-5.0) - np.arange(RET_H, dtype=np.float32))).astype(np.float32)]

_MIB = 1024 * 1024


def _rms(x, g):
    ms = jnp.mean(x * x, axis=-1, keepdims=True)
    return (x * lax.rsqrt(ms + EPS)) * g


_GELU_C = 0.7978845608028654
_LOG2E = 1.4426950408889634


def _gelu(x):
    k0 = -2.0 * _LOG2E * _GELU_C
    k1 = k0 * 0.044715
    return x * (1.0 / (1.0 + jnp.exp2(x * (k0 + k1 * (x * x)))))


def _dot(a, b):
    return jnp.dot(a, b, preferred_element_type=F32)


def _dot_nt(a, b):
    return lax.dot_general(a, b, (((1,), (1,)), ((), ())), preferred_element_type=F32)


def _const(shape):
    n = len(shape)
    return pl.BlockSpec(shape, lambda *_: (0,) * n, pipeline_mode=pl.Buffered(1))


def _layer_const(l, shape):
    n = len(shape)
    return pl.BlockSpec((None,) + tuple(shape), lambda *_: (l,) + (0,) * n, pipeline_mode=pl.Buffered(1))


def _full(shape):
    n = len(shape)
    return pl.BlockSpec(shape, lambda *_: (0,) * n)


def _params(sem, vmem_mib):
    return pltpu.CompilerParams(dimension_semantics=sem, vmem_limit_bytes=vmem_mib * _MIB)


def _conv_rows(u, prev_ref, w_ref, cols, sh_ref):
    r, c = u.shape
    ys = []
    for j in range(c // 128):
        cj = slice(cols.start + j * 128, cols.start + (j + 1) * 128)
        uj = u[:, j * 128:(j + 1) * 128]
        sh_ref[j, 0:8, :] = prev_ref[:, cj]
        sh_ref[j, 8:r + 8, :] = uj
        u1 = sh_ref[j, 7:r + 7, :]
        u2 = sh_ref[j, 6:r + 6, :]
        ys.append((w_ref[0:1, cj] * u2 + w_ref[1:2, cj] * u1) + w_ref[2:3, cj] * uj)
    return jnp.concatenate(ys, axis=1)


def _memkv_body(mem_ref, g_ref, wk_ref, wv_ref, k_ref, v_ref, kb_ref, vb_ref, *, nb):
    mn = _rms(mem_ref[...], g_ref[...]).astype(BF16)
    for w_ref, o_ref, ob_ref in ((wk_ref, k_ref, kb_ref), (wv_ref, v_ref, vb_ref)):
        y = _dot(mn, w_ref[...].astype(BF16))
        ob_ref[...] = y.astype(BF16)
        for b in range(nb):
            for hd in range(XH):
                for j in range(2):
                    c0 = hd * XHD + j * 128
                    o_ref[b, pl.ds(j * XH + hd, N_MEM, stride=8), :] = y[b * N_MEM:(b + 1) * N_MEM, c0:c0 + 128]


def _memkv(mem, g, wk, wv):
    rows = mem.shape[0]
    nb = 2
    tm = nb * N_MEM
    o_spec = pl.BlockSpec((None, nb, N_MEM * 8, 128), lambda l, i: (l, i, 0, 0))
    ob_spec = pl.BlockSpec((None, tm, D), lambda l, i: (l, i, 0))
    w_spec = pl.BlockSpec((None, D, D), lambda l, i: (l, 0, 0))
    return pl.pallas_call(
        functools.partial(_memkv_body, nb=nb),
        grid=(2, rows // tm),
        in_specs=[pl.BlockSpec((tm, D), lambda l, i: (i, 0)),
                  pl.BlockSpec((None, 1, D), lambda l, i: (l, 0, 0)),
                  w_spec, w_spec],
        out_specs=[o_spec, o_spec, ob_spec, ob_spec],
        out_shape=[jax.ShapeDtypeStruct((2, NB, N_MEM * 8, 128), F32)] * 2
        + [jax.ShapeDtypeStruct((2, rows, D), BF16)] * 2,
        compiler_params=_params(("arbitrary", "arbitrary"), 40),
        name="memkv",
    )(mem, g, wk, wv)


def _from_head_view(c):
    s = c.shape
    return c.reshape(s[0], s[1], N_MEM, 2, XH, 128).transpose(0, 1, 2, 4, 3, 5).reshape(s[0], s[1], N_MEM, XH, XHD)


def _even_front(x, gpre_ref, win, vn_ref):
    h = _rms(x, gpre_ref[...]).astype(BF16)
    u = _gelu(_dot(h, win(0, 512)))
    v = _rms(_gelu(_dot(h, win(512, 1024))), vn_ref[...])
    bg = _dot(h, win(1024, 1536))
    p = _dot(h, win(1536, 2048)) * _dot(h, win(2048, 2560))
    return u, v, bg, p


def _even_p_body(x_ref, gpre_ref, winf_ref, vn_ref, sw_ref, sb_ref, cw_ref, woutf_ref, gpost_ref,
                 xo_ref, st_ref, win_ref, wout_ref, carry_ref, cat_ref, sh_ref, *, tm):
    @pl.when((pl.program_id(0) == 0) & (pl.program_id(1) == 0))
    def _():
        win_ref[...] = winf_ref[...].astype(BF16)
        wout_ref[...] = woutf_ref[...].astype(BF16)

    @pl.when(pl.program_id(1) == 0)
    def _():
        carry_ref[...] = jnp.zeros_like(carry_ref)

    x = x_ref[...]
    u, v, bg, p = _even_front(x, gpre_ref, lambda c0, c1: win_ref[:, c0:c1], vn_ref)
    vb = v.astype(BF16)
    ri = lax.broadcasted_iota(jnp.int32, (CHUNK, CHUNK), 0)
    ci = lax.broadcasted_iota(jnp.int32, (CHUNK, CHUNK), 1)
    for g in range(4):
        cols = slice(g * 128, (g + 1) * 128)
        w = jnp.where(ri >= ci, sw_ref[g], 0.0).astype(BF16)
        for c in range(tm // CHUNK):
            rows = slice(c * CHUNK, (c + 1) * CHUNK)
            mixed = _dot(w, vb[rows, cols]) + sb_ref[:, cols]
            cat_ref[rows, cols] = (u[rows, cols] * mixed).astype(BF16)
    cz = _conv_rows(p, carry_ref, cw_ref, slice(0, SC_W), sh_ref)
    cat_ref[:, 512:1024] = (bg * cz).astype(BF16)
    carry_ref[...] = p[tm - 8:tm, :]
    st_ref[...] = p[tm - 8:tm, :]
    out = _dot(cat_ref[...], wout_ref[...])
    xo_ref[...] = x + _rms(out, gpost_ref[...])


def _even_p_parts(tm):
    nt = SEQ // tm
    row_spec = pl.BlockSpec((tm, D), lambda b, t: (b * nt + t, 0))
    in_specs = [row_spec, _const((1, D)), _layer_const(0, (D, 2560)), _const((1, SGU_W)),
                _layer_const(0, (4, CHUNK, CHUNK)), _const((CHUNK, SGU_W)), _layer_const(0, (3, SC_W)),
                _layer_const(0, (D, D)), _const((1, D))]
    out_specs = [row_spec, pl.BlockSpec((None, 8, SC_W), lambda b, t: (b, 0, 0))]
    out_shape = [jax.ShapeDtypeStruct((NB * SEQ, D), F32), jax.ShapeDtypeStruct((NB, 8, SC_W), F32)]
    scratch = [pltpu.VMEM((D, 2560), BF16), pltpu.VMEM((D, D), BF16), pltpu.VMEM((8, SC_W), F32),
               pltpu.VMEM((tm, D), BF16), pltpu.VMEM((SC_W // 128, tm + 8, 128), F32)]
    return nt, in_specs, out_specs, out_shape, scratch


def _even_s_body(x_ref, gpre_ref, win_ref, vn_ref, gw_ref, gb_ref, cw_ref, st_ref, wout_ref, gpost_ref,
                 xo_ref, ns_ref, v_ref, pan_ref):
    xt = _load_time_major(x_ref, pan_ref)
    u, v, bg, p = _even_front(xt, gpre_ref, lambda c0, c1: win_ref[:, c0:c1].astype(BF16), vn_ref)
    _store_batch_major(v_ref, v, pan_ref)
    vt = [v[t * NS:(t + 1) * NS, :] for t in range(TS)]
    mixed = []
    for t in range(TS):
        m = gb_ref[t:t + 1, :] + gw_ref[t, 0:1, :] * vt[0]
        for s in range(1, t + 1):
            m = m + gw_ref[t, s:s + 1, :] * vt[s]
        mixed.append(m)
    a = u * jnp.concatenate(mixed, axis=0)
    prev = [jnp.concatenate([st_ref[j * 2 + r] for j in range(SC_W // 128)], axis=1) for r in range(2)]
    cz = _conv_time_major(p, prev, cw_ref)
    for j in range(SC_W // 128):
        for r in range(2):
            ns_ref[j * 2 + r] = p[(TS - 2 + r) * NS:(TS - 1 + r) * NS, j * 128:(j + 1) * 128]
    cat = jnp.concatenate([a, bg * cz], axis=1).astype(BF16)
    out = _dot(cat, wout_ref[...].astype(BF16))
    _store_batch_major(xo_ref, xt + _rms(out, gpost_ref[...]), pan_ref)


def _even_s(x, gpre, win, vn, gw, gb, cw, st, wout, gpost):
    nst = SC_W // 128 * 2
    return pl.pallas_call(
        _even_s_body,
        grid=(1,),
        in_specs=[_const((RS, D)), _const((1, D)), _layer_const(0, (D, 2560)), _const((1, SGU_W)),
                  _const((TS, TS, SGU_W)), _const((TS, SGU_W)), _layer_const(0, (3, SC_W)), _const((nst, NS, 128)),
                  _layer_const(0, (D, D)), _const((1, D))],
        out_specs=[_full((RS, D)), _full((nst, NS, 128)), _full((RS, SGU_W))],
        out_shape=[jax.ShapeDtypeStruct((RS, D), F32), jax.ShapeDtypeStruct((nst, NS, 128), F32),
                   jax.ShapeDtypeStruct((RS, SGU_W), F32)],
        scratch_shapes=[pltpu.VMEM((D // 128, RS, 128), F32)],
        compiler_params=_params(("arbitrary",), 48),
        name="even_s",
    )(x, gpre, win, vn, gw, gb, cw, st, wout, gpost)


def _softmax_rows(s):
    m = jnp.max(s, axis=-1, keepdims=True)
    e = jnp.exp(s - m)
    return e * (1.0 / jnp.sum(e, axis=-1, keepdims=True))


def _xattn_p_body(x_ref, gpre_ref, wqf_ref, k_ref, v_ref, wof_ref, gpost_ref, xo_ref, wq_ref, wo_ref, q_ref, s_ref,
                  p_ref, o_ref):
    @pl.when((pl.program_id(0) == 0) & (pl.program_id(1) == 0))
    def _():
        wq_ref[...] = wqf_ref[...].astype(BF16)
        wo_ref[...] = wof_ref[...].astype(BF16)

    ng, gr = q_ref.shape[0], q_ref.shape[1]
    for a in range(ng):
        rows = slice(a * gr, (a + 1) * gr)
        h = _rms(x_ref[rows, :], gpre_ref[...]).astype(BF16)
        q_ref[a] = _dot(h, wq_ref[...]).astype(BF16)
        for hd in range(XH):
            cols = slice(hd * XHD, (hd + 1) * XHD)
            s_ref[a, hd] = _dot_nt(q_ref[a, :, cols], k_ref[:, cols]) * (XHD ** -0.5)
        p_ref[a] = _softmax_rows(s_ref[a]).astype(BF16)
        for hd in range(XH):
            cols = slice(hd * XHD, (hd + 1) * XHD)
            o_ref[a, :, cols] = _dot(p_ref[a, hd], v_ref[:, cols]).astype(BF16)
        out = _dot(o_ref[a], wo_ref[...])
        xo_ref[rows, :] = x_ref[rows, :] + _rms(out, gpost_ref[...])


def _xattn_p_parts(l, tm, groups=1):
    nt = SEQ // tm
    gr = tm // groups
    row_spec = pl.BlockSpec((tm, D), lambda b, t: (b * nt + t, 0))
    kv_spec = pl.BlockSpec((None, N_MEM, D), lambda b, t: (l, b, 0))
    in_specs = [row_spec, _const((1, D)), _layer_const(l, (D, D)), kv_spec, kv_spec, _layer_const(l, (D, D)),
                _const((1, D))]
    scratch = [pltpu.VMEM((D, D), BF16), pltpu.VMEM((D, D), BF16), pltpu.VMEM((groups, gr, D), BF16),
               pltpu.VMEM((groups, XH, gr, N_MEM), F32), pltpu.VMEM((groups, XH, gr, N_MEM), BF16),
               pltpu.VMEM((groups, gr, D), BF16)]
    return nt, row_spec, in_specs, scratch


def _xattn_p(l, x, gpre, wq, kb, vb, wo, gpost):
    nt, row_spec, in_specs, scratch = _xattn_p_parts(l, 1024, groups=2)
    return pl.pallas_call(
        _xattn_p_body,
        grid=(NB, nt),
        in_specs=in_specs,
        out_specs=row_spec,
        out_shape=jax.ShapeDtypeStruct((NB * SEQ, D), F32),
        scratch_shapes=scratch,
        compiler_params=_params(("arbitrary", "arbitrary"), 48),
        name="xattn_p",
    )(x, gpre, wq, kb, vb, wo, gpost)


def _head_view(c):
    s = c.shape
    return c.reshape(s[0], s[1], N_MEM, XH, 2, 128).transpose(0, 1, 2, 4, 3, 5).reshape(s[0], s[1], N_MEM * 8, 128)


def _head_rows(ref, b, hd):
    halves = [ref[b, pl.ds(j * XH + hd, N_MEM, stride=8), :] for j in range(2)]
    return jnp.concatenate(halves, axis=1).astype(BF16)


def _xattn_s_body(x_ref, gpre_ref, wq_ref, k_ref, v_ref, wo_ref, gpost_ref, xo_ref, q_ref, o_ref, s_ref, *, bb):
    _xattn_s_step(pl.program_id(0), pl.num_programs(0), x_ref, gpre_ref, wq_ref, k_ref, v_ref, wo_ref, gpost_ref,
                  xo_ref, q_ref, o_ref, s_ref, bb=bb)


def _xattn_s_step(step, nsteps, x_ref, gpre_ref, wq_ref, k_ref, v_ref, wo_ref, gpost_ref, xo_ref, q_ref, o_ref, s_ref,
                  *, bb):
    @pl.when(step == 0)
    def _():
        h = _rms(x_ref[...], gpre_ref[...]).astype(BF16)
        q_ref[...] = _dot(h, wq_ref[...].astype(BF16))

    first = (lax.broadcasted_iota(jnp.int32, (8, XHD), 0) >> 2) == 0
    groups = [(pi, hd, bi) for pi in range(bb // 2) for hd in range(XH) for bi in range(2)]
    row0 = [pl.multiple_of((step * (bb // 2) + pi) * 8, 8) for pi in range(bb // 2)]
    for gi, (pi, hd, bi) in enumerate(groups):
        q8 = q_ref[pl.ds(row0[pi], 8), hd * XHD:(hd + 1) * XHD].astype(BF16)
        kh = _head_rows(k_ref, pi * 2 + bi, hd)
        s_ref[gi * 8:(gi + 1) * 8, :] = _dot_nt(q8, kh) * (XHD ** -0.5)
    s_ref[...] = _softmax_rows(s_ref[...])
    for gi, (pi, hd, bi) in enumerate(groups):
        if bi == 1:
            continue
        pv = [_dot(s_ref[(gi + b) * 8:(gi + b + 1) * 8, :].astype(BF16), _head_rows(v_ref, pi * 2 + b, hd))
              for b in range(2)]
        o_ref[pl.ds(row0[pi], 8), hd * XHD:(hd + 1) * XHD] = jnp.where(first, pv[0], pv[1])

    @pl.when(step == nsteps - 1)
    def _():
        out = _dot(o_ref[...].astype(BF16), wo_ref[...].astype(BF16))
        xo_ref[...] = x_ref[...] + _rms(out, gpost_ref[...])


def _xattn_ps_body(x_ref, gpre_ref, wqf_ref, k_ref, v_ref, wof_ref, gpost_ref,
                   xs_ref, gpres_ref, ks_ref, vs_ref, gposts_ref,
                   xo_ref, xso_ref,
                   wq_ref, wo_ref, q_ref, s_ref, p_ref, o_ref, qs_ref, os_ref, ss_ref, *, bb):
    _xattn_p_body(x_ref, gpre_ref, wqf_ref, k_ref, v_ref, wof_ref, gpost_ref, xo_ref, wq_ref, wo_ref, q_ref, s_ref,
                  p_ref, o_ref)
    step = pl.program_id(0) * pl.num_programs(1) + pl.program_id(1)
    _xattn_s_step(step, pl.num_programs(0) * pl.num_programs(1), xs_ref, gpres_ref, wq_ref, ks_ref, vs_ref, wo_ref,
                  gposts_ref, xso_ref, qs_ref, os_ref, ss_ref, bb=bb)


def _xattn_ps(l, x, gpre, wq, kb, vb, wo, gpost, xs, gpres, ks, vs, gposts):
    bb = 2
    steps = NS // bb
    tm = NB * SEQ // steps
    nt, row_spec, h_in, h_scratch = _xattn_p_parts(l, tm)
    kv_spec = pl.BlockSpec((None, bb, N_MEM * 8, 128), lambda b, t: (l, b * nt + t, 0, 0))
    return pl.pallas_call(
        functools.partial(_xattn_ps_body, bb=bb),
        grid=(NB, nt),
        in_specs=h_in + [_const((RS, D)), _const((1, D)), kv_spec, kv_spec, _const((1, D))],
        out_specs=[row_spec, _full((RS, D))],
        out_shape=[jax.ShapeDtypeStruct((NB * SEQ, D), F32), jax.ShapeDtypeStruct((RS, D), F32)],
        scratch_shapes=h_scratch + [pltpu.VMEM((RS, D), F32), pltpu.VMEM((RS, D), F32),
                                    pltpu.VMEM((bb * XH * 8, N_MEM), F32)],
        compiler_params=_params(("arbitrary", "arbitrary"), 56),
        name="xattn_ps",
    )(x, gpre, wq, kb, vb, wo, gpost, xs, gpres, ks, vs, gposts)


def _xattn_s(l, x, gpre, wq, k, v, wo, gpost):
    bb = 4
    kv_spec = pl.BlockSpec((None, bb, N_MEM * 8, 128), lambda i: (l, i, 0, 0))
    return pl.pallas_call(
        functools.partial(_xattn_s_body, bb=bb),
        grid=(NS // bb,),
        in_specs=[_const((RS, D)), _const((1, D)), _layer_const(l, (D, D)), kv_spec, kv_spec, _layer_const(l, (D, D)),
                  _const((1, D))],
        out_specs=_full((RS, D)),
        out_shape=jax.ShapeDtypeStruct((RS, D), F32),
        scratch_shapes=[pltpu.VMEM((RS, D), F32), pltpu.VMEM((RS, D), F32), pltpu.VMEM((bb * XH * 8, N_MEM), F32)],
        compiler_params=_params(("arbitrary",), 48),
        name="xattn_s",
    )(x, gpre, wq, k, v, wo, gpost)


def _ffn_p_body(x_ref, gpre_ref, wg_ref, wv_ref, cw_ref, wdn_ref, gpost_ref, xo_ref, st_ref, carry_ref, h_ref, act_ref,
                *sh_refs, tm):
    @pl.when(pl.program_id(1) == 0)
    def _():
        carry_ref[...] = jnp.zeros_like(carry_ref)

    h_ref[...] = _rms(x_ref[...], gpre_ref[...]).astype(BF16)
    for c in range(NFC):
        halves = []
        for i, (base, w_ref) in enumerate(((0, wg_ref), (DFF, wv_ref))):
            cols = slice(base + c * FC, base + (c + 1) * FC)
            up = _dot(h_ref[...], w_ref[:, c * FC:(c + 1) * FC])
            halves.append(_conv_rows(up, carry_ref, cw_ref, cols, sh_refs[(2 * c + i) % len(sh_refs)]))
            carry_ref[:, cols] = up[tm - 8:tm, :]
            st_ref[:, cols] = up[tm - 8:tm, :]
        act_ref[:, c * FC:(c + 1) * FC] = (_gelu(halves[0]) * halves[1]).astype(BF16)
    out = _dot(act_ref[...], wdn_ref[...])
    xo_ref[...] = x_ref[...] + _rms(out, gpost_ref[...])


def _ffn_p_parts(l, tm):
    nt = SEQ // tm
    row_spec = pl.BlockSpec((tm, D), lambda b, t: (b * nt + t, 0))
    in_specs = [row_spec, _const((1, D)), _const((D, DFF)), _const((D, DFF)), _layer_const(l, (3, 2 * DFF)),
                _const((DFF, D)), _const((1, D))]
    out_specs = [row_spec, pl.BlockSpec((None, 8, 2 * DFF), lambda b, t: (b, 0, 0))]
    out_shape = [jax.ShapeDtypeStruct((NB * SEQ, D), F32), jax.ShapeDtypeStruct((NB, 8, 2 * DFF), F32)]
    scratch = ([pltpu.VMEM((8, 2 * DFF), F32), pltpu.VMEM((tm, D), BF16), pltpu.VMEM((tm, DFF), BF16)]
               + [pltpu.VMEM((FC // 128, tm + 8, 128), F32)] * 4)
    return nt, in_specs, out_specs, out_shape, scratch


def _ffn_p(l, x, gpre, wg, wv, cw, wdn, gpost):
    tm = 512
    nt, in_specs, out_specs, out_shape, scratch = _ffn_p_parts(l, tm)
    return pl.pallas_call(
        functools.partial(_ffn_p_body, tm=tm),
        grid=(NB, nt),
        in_specs=in_specs,
        out_specs=out_specs,
        out_shape=out_shape,
        scratch_shapes=scratch,
        compiler_params=_params(("parallel", "arbitrary"), 52),
        name="ffn_p",
    )(x, gpre, wg, wv, cw, wdn, gpost)


def _load_time_major(x_ref, pan_ref):
    n = x_ref.shape[1] // 128
    for p in range(n):
        pan_ref[p] = x_ref[:, p * 128:(p + 1) * 128]
    return jnp.concatenate(
        [jnp.concatenate([pan_ref[p, pl.ds(t, NS, stride=TS), :] for p in range(n)], axis=1) for t in range(TS)],
        axis=0)


def _store_batch_major(o_ref, y, pan_ref):
    n = o_ref.shape[1] // 128
    for p in range(n):
        for t in range(TS):
            pan_ref[p, pl.ds(t, NS, stride=TS), :] = y[t * NS:(t + 1) * NS, p * 128:(p + 1) * 128]
    for p in range(n):
        o_ref[:, p * 128:(p + 1) * 128] = pan_ref[p]


def _conv_time_major(u, prev, w_ref):
    blocks = list(prev) + [u[t * NS:(t + 1) * NS, :] for t in range(TS)]
    w0, w1, w2 = w_ref[0:1, :], w_ref[1:2, :], w_ref[2:3, :]
    return jnp.concatenate([(w0 * blocks[t] + w1 * blocks[t + 1]) + w2 * blocks[t + 2] for t in range(TS)], axis=0)


def _ffn_s_body(x_ref, gpre_ref, wg_ref, wv_ref, cg_ref, cv_ref, sg_ref, sv_ref, wdn_ref, gpost_ref,
                xo_ref, ng_ref, nv_ref, wgb_ref, wvb_ref, wdb_ref, xt_ref, h_ref, acc_ref, pan_ref):
    c = pl.program_id(0)

    @pl.when(c == 0)
    def _():
        xt = _load_time_major(x_ref, pan_ref)
        xt_ref[...] = xt
        h_ref[...] = _rms(xt, gpre_ref[...]).astype(BF16)
        acc_ref[...] = jnp.zeros_like(acc_ref)

    wgb_ref[...] = wg_ref[...].astype(BF16)
    wvb_ref[...] = wv_ref[...].astype(BF16)
    wdb_ref[...] = wdn_ref[...].astype(BF16)
    ys = []
    for wb_ref, cw_ref, s_ref, n_ref in ((wgb_ref, cg_ref, sg_ref, ng_ref), (wvb_ref, cv_ref, sv_ref, nv_ref)):
        u = _dot(h_ref[...], wb_ref[...])
        ys.append(_conv_time_major(u, [s_ref[:, r, :] for r in range(2)], cw_ref))
        for r in range(2):
            n_ref[:, r, :] = u[(TS - 2 + r) * NS:(TS - 1 + r) * NS, :]
    act = (_gelu(ys[0]) * ys[1]).astype(BF16)
    acc_ref[...] += _dot(act, wdb_ref[...])

    @pl.when(c == pl.num_programs(0) - 1)
    def _():
        _store_batch_major(xo_ref, xt_ref[...] + _rms(acc_ref[...], gpost_ref[...]), pan_ref)


def _ffn_s(l, x, gpre, wup, cw, st, wdn, gpost):
    gate = lambda *lead: pl.BlockSpec((None,) + lead + (FC,), lambda c: (l,) + (0,) * len(lead) + (c,))
    val = lambda *lead: pl.BlockSpec((None,) + lead + (FC,), lambda c: (l,) + (0,) * len(lead) + (NFC + c,))
    s_out = pl.BlockSpec((NS, 2, FC), lambda c: (0, 0, c))
    wb_out = pl.BlockSpec((D, FC), lambda c: (0, c))
    return pl.pallas_call(
        _ffn_s_body,
        grid=(NFC,),
        in_specs=[_const((RS, D)), _const((1, D)), gate(D), val(D), gate(3), val(3), gate(NS, 2), val(NS, 2),
                  pl.BlockSpec((None, FC, D), lambda c: (l, c, 0)), _const((1, D))],
        out_specs=[_full((RS, D)), s_out, s_out, wb_out, wb_out, pl.BlockSpec((FC, D), lambda c: (c, 0))],
        out_shape=[jax.ShapeDtypeStruct((RS, D), F32),
                   jax.ShapeDtypeStruct((NS, 2, DFF), F32), jax.ShapeDtypeStruct((NS, 2, DFF), F32),
                   jax.ShapeDtypeStruct((D, DFF), BF16), jax.ShapeDtypeStruct((D, DFF), BF16),
                   jax.ShapeDtypeStruct((DFF, D), BF16)],
        scratch_shapes=[pltpu.VMEM((RS, D), F32), pltpu.VMEM((RS, D), BF16), pltpu.VMEM((RS, D), F32),
                        pltpu.VMEM((D // 128, RS, 128), F32)],
        compiler_params=_params(("arbitrary",), 40),
        name="ffn_s",
    )(x, gpre, wup, wup, cw, cw, st, st, wdn, gpost)


def _rope_head(z, cos, sin, base):
    x1 = z[:, base:base + 128]
    x2 = z[:, base + 128:base + 256]
    return x1 * cos - x2 * sin, x1 * sin + x2 * cos


def _group_norm_gate(o, g):
    mu = jnp.mean(o, axis=-1, keepdims=True)
    d = o - mu
    var = jnp.mean(d * d, axis=-1, keepdims=True)
    return (g * jax.nn.sigmoid(g)) * (d * lax.rsqrt(var + EPS))


RCH = 256


def _ret_p_body(x_ref, cos_ref, sin_ref, gpre_ref, wq_ref, wk_ref, wv_ref, wg_ref, wout_ref, gpost_ref, xo_ref, s_ref,
                h_ref, q_ref, k_ref, kd_ref, v_ref, g_ref, y_ref, dm_ref, qd_ref, kdec_ref, *, tm):
    first = (pl.program_id(0) == 0) & (pl.program_id(1) == 0)

    @pl.when(first)
    def _():
        i = lax.broadcasted_iota(jnp.int32, (RCH, RCH), 0).astype(F32)
        j = lax.broadcasted_iota(jnp.int32, (RCH, RCH), 1).astype(F32)
        i1 = lax.broadcasted_iota(jnp.int32, (RCH, 128), 0).astype(F32)
        for hd in range(RET_H):
            lg = _LOG_G[hd]
            dm_ref[hd] = jnp.where(i >= j, jnp.exp(lg * jnp.maximum(i - j, 0.0)), 0.0)
            qd_ref[hd] = jnp.exp(lg * (i1 + 1.0))
            kdec_ref[hd] = jnp.exp(lg * (RCH - 1.0 - i1))

    @pl.when(pl.program_id(1) == 0)
    def _():
        s_ref[...] = jnp.zeros_like(s_ref)

    h_ref[...] = _rms(x_ref[...], gpre_ref[...]).astype(BF16)
    cos = cos_ref[...]
    sin = sin_ref[...]
    nchunk = tm // RCH
    for hd in range(RET_H):
        base = hd * RET_DK
        zq = _dot(h_ref[...], wq_ref[:, base:base + RET_DK])
        zk = _dot(h_ref[...], wk_ref[:, base:base + RET_DK])
        q1, q2 = _rope_head(zq, cos, sin, 0)
        k1, k2 = _rope_head(zk, cos, sin, 0)
        q_ref[:, base:base + 128] = (q1 * (RET_DK ** -0.5)).astype(BF16)
        q_ref[:, base + 128:base + 256] = (q2 * (RET_DK ** -0.5)).astype(BF16)
        k_ref[:, base:base + 128] = k1.astype(BF16)
        k_ref[:, base + 128:base + 256] = k2.astype(BF16)
        kdec = jnp.concatenate([kdec_ref[hd]] * nchunk, axis=0)
        kd_ref[:, base:base + 128] = k1 * kdec
        kd_ref[:, base + 128:base + 256] = k2 * kdec
        vcols = slice(hd * RET_DV, (hd + 1) * RET_DV)
        v_ref[:, vcols] = _dot(h_ref[...], wv_ref[:, vcols]).astype(BF16)
        g_ref[:, vcols] = _dot(h_ref[...], wg_ref[:, vcols])
    for hd in range(RET_H):
        kcols = slice(hd * RET_DK, (hd + 1) * RET_DK)
        vcols = slice(hd * RET_DV, (hd + 1) * RET_DV)
        decay_l = float(np.exp(np.float32(_LOG_G[hd]) * np.float32(RCH)))
        qd = qd_ref[hd]
        qd4 = jnp.concatenate([qd, qd, qd, qd], axis=1)
        for c in range(nchunk):
            rows = slice(c * RCH, (c + 1) * RCH)
            qc = q_ref[rows, kcols]
            vc = v_ref[rows, vcols]
            s_prev = s_ref[hd]
            sc = (_dot_nt(qc, k_ref[rows, kcols]) * dm_ref[hd]).astype(BF16)
            o = _dot(sc, vc) + _dot(qc, s_prev.astype(BF16)) * qd4
            kdt = kd_ref[rows, kcols].T.astype(BF16)
            s_ref[hd] = decay_l * s_prev + _dot(kdt, vc)
            y_ref[rows, vcols] = _group_norm_gate(o, g_ref[rows, vcols]).astype(BF16)
    out = _dot(y_ref[...], wout_ref[...])
    xo_ref[...] = x_ref[...] + _rms(out, gpost_ref[...])


def _ret_p(x, cos, sin, gpre, wq, wk, wv, wg, wout, gpost):
    tm = 512
    nt = SEQ // tm
    row_spec = pl.BlockSpec((tm, D), lambda b, t: (b * nt + t, 0))
    cs_spec = pl.BlockSpec((tm, 128), lambda b, t: (t, 0))
    return pl.pallas_call(
        functools.partial(_ret_p_body, tm=tm),
        grid=(NB, nt),
        in_specs=[row_spec, cs_spec, cs_spec, _const((1, D)), _const((D, NQ)), _const((D, NQ)), _const((D, NV)),
                  _const((D, NV)), _const((NV, D)), _const((1, D))],
        out_specs=[row_spec, pl.BlockSpec((None, RET_H, RET_DK, RET_DV), lambda b, t: (b, 0, 0, 0))],
        out_shape=[jax.ShapeDtypeStruct((NB * SEQ, D), F32), jax.ShapeDtypeStruct((NB, RET_H, RET_DK, RET_DV), F32)],
        scratch_shapes=[pltpu.VMEM((tm, D), BF16), pltpu.VMEM((tm, NQ), BF16), pltpu.VMEM((tm, NQ), BF16),
                        pltpu.VMEM((tm, NQ), F32), pltpu.VMEM((tm, NV), BF16), pltpu.VMEM((tm, NV), F32),
                        pltpu.VMEM((tm, NV), BF16), pltpu.VMEM((RET_H, RCH, RCH), F32),
                        pltpu.VMEM((RET_H, RCH, 128), F32), pltpu.VMEM((RET_H, RCH, 128), F32)],
        compiler_params=_params(("arbitrary", "arbitrary"), 56),
        name="ret_p",
    )(x, cos, sin, gpre, wq, wk, wv, wg, wout, gpost)


def _ret_s_proj_body(x_ref, cos_ref, sin_ref, gpre_ref, wqf_ref, wkf_ref, wvf_ref, wgf_ref,
                     q_ref, kdt_ref, v_ref, g_ref, oi_ref, wq_ref, wk_ref, wv_ref, wg_ref, h_ref):
    hd = pl.program_id(0)

    @pl.when(hd == 0)
    def _():
        h_ref[...] = _rms(x_ref[...], gpre_ref[...]).astype(BF16)

    for wf_ref, wb_ref in ((wqf_ref, wq_ref), (wkf_ref, wk_ref), (wvf_ref, wv_ref), (wgf_ref, wg_ref)):
        wb_ref[...] = wf_ref[...].astype(BF16)
    lg = jnp.where(hd == 0, _LOG_G[0], jnp.where(hd == 1, _LOG_G[1], jnp.where(hd == 2, _LOG_G[2], _LOG_G[3])))
    h = h_ref[...]
    cos = cos_ref[...]
    sin = sin_ref[...]
    ti = (lax.broadcasted_iota(jnp.int32, (RS, 128), 0) & (TS - 1)).astype(F32)
    i = lax.broadcasted_iota(jnp.int32, (CHUNK, CHUNK), 0)
    j = lax.broadcasted_iota(jnp.int32, (CHUNK, CHUNK), 1)
    same = ((i >> 2) == (j >> 2)) & (i >= j)
    dm = jnp.where(same, jnp.exp(lg * jnp.maximum(i - j, 0).astype(F32)), 0.0)
    q1, q2 = _rope_head(_dot(h, wq_ref[...]), cos, sin, 0)
    k1, k2 = _rope_head(_dot(h, wk_ref[...]), cos, sin, 0)
    q = jnp.concatenate([q1, q2], axis=1) * (RET_DK ** -0.5)
    k = jnp.concatenate([k1, k2], axis=1)
    q_ref[...] = q
    kdec = jnp.exp(lg * (TS - 1.0 - ti))
    kd = k * jnp.concatenate([kdec, kdec], axis=1)
    vf = _dot(h, wv_ref[...])
    v_ref[...] = vf
    v = vf.astype(BF16)
    g_ref[...] = _dot(h, wg_ref[...])
    qb = q.astype(BF16)
    kb = k.astype(BF16)
    for tl in range(RS // CHUNK):
        rows = slice(tl * CHUNK, (tl + 1) * CHUNK)
        sc = (_dot_nt(qb[rows], kb[rows]) * dm).astype(BF16)
        oi_ref[rows, :] = _dot(sc, v[rows])
        kdt_ref[tl] = kd[rows].T.astype(BF16)


def _ret_s_proj(x, cos, sin, gpre, win):
    def wcol(width, first_block):
        return pl.BlockSpec((None, D, width), lambda h: (0, 0, first_block + h))
    def hcol(rows, width):
        return pl.BlockSpec((rows, width), lambda h: (0, h))
    return pl.pallas_call(
        _ret_s_proj_body,
        grid=(RET_H,),
        in_specs=[_const((RS, D)), _const((RS, 128)), _const((RS, 128)), _const((1, D)),
                  wcol(RET_DK, 0), wcol(RET_DK, NQ // RET_DK), wcol(RET_DV, 2 * NQ // RET_DV),
                  wcol(RET_DV, (2 * NQ + NV) // RET_DV)],
        out_specs=[hcol(RS, RET_DK),
                   pl.BlockSpec((RS // CHUNK, None, RET_DK, CHUNK), lambda h: (0, h, 0, 0)),
                   hcol(RS, RET_DV), hcol(RS, RET_DV), hcol(RS, RET_DV),
                   hcol(D, RET_DK), hcol(D, RET_DK), hcol(D, RET_DV), hcol(D, RET_DV)],
        out_shape=[jax.ShapeDtypeStruct((RS, NQ), F32),
                   jax.ShapeDtypeStruct((RS // CHUNK, RET_H, RET_DK, CHUNK), BF16),
                   jax.ShapeDtypeStruct((RS, NV), F32), jax.ShapeDtypeStruct((RS, NV), F32),
                   jax.ShapeDtypeStruct((RS, NV), F32),
                   jax.ShapeDtypeStruct((D, NQ), BF16), jax.ShapeDtypeStruct((D, NQ), BF16),
                   jax.ShapeDtypeStruct((D, NV), BF16), jax.ShapeDtypeStruct((D, NV), BF16)],
        scratch_shapes=[pltpu.VMEM((RS, D), BF16)],
        compiler_params=_params(("arbitrary",), 40),
        name="ret_s_proj",
    )(x, cos, sin, gpre, win, win, win, win)


def _ret_s_core_body(step, q_ref, kdt_ref, v_ref, oi_ref, s_ref, o_ref, so_ref):
    pair_in_tile = step % (CHUNK // 8)
    row8 = lax.broadcasted_iota(jnp.int32, (8, RET_DK), 0)
    row128 = lax.broadcasted_iota(jnp.int32, (CHUNK, RET_DV), 0)
    t8 = (lax.broadcasted_iota(jnp.int32, (8, RET_DV), 0) & (TS - 1)).astype(F32)
    for hd in range(RET_H):
        lg = _LOG_G[hd]
        decay_l = float(np.exp(np.float32(lg) * np.float32(TS)))
        q8 = q_ref[:, hd * RET_DK:(hd + 1) * RET_DK]
        v128 = v_ref[:, hd * RET_DV:(hd + 1) * RET_DV]
        kdt = kdt_ref[hd]
        inter = jnp.zeros((8, RET_DV), F32)
        for bi in range(2):
            s_prev = s_ref[bi, hd]
            qm = jnp.where((row8 >> 2) == bi, q8, 0.0).astype(BF16)
            inter = inter + _dot(qm, s_prev.astype(BF16))
            vm = jnp.where((row128 >> 2) == pair_in_tile * 2 + bi, v128, 0.0).astype(BF16)
            so_ref[bi, hd] = decay_l * s_prev + _dot(kdt, vm)
        cols = slice(hd * RET_DV, (hd + 1) * RET_DV)
        o_ref[:, cols] = oi_ref[:, cols] + inter * jnp.exp(lg * (t8 + 1.0))


def _ret_s_core_parts():
    ppt = CHUNK // 8
    s_spec = pl.BlockSpec((2, RET_H, RET_DK, RET_DV), lambda i: (i, 0, 0, 0))
    in_specs = [pl.BlockSpec((8, NQ), lambda i: (i, 0)),
                pl.BlockSpec((None, RET_H, RET_DK, CHUNK), lambda i: (i // ppt, 0, 0, 0)),
                pl.BlockSpec((CHUNK, NV), lambda i: (i // ppt, 0)),
                pl.BlockSpec((8, NV), lambda i: (i, 0)),
                s_spec]
    out_specs = [pl.BlockSpec((8, NV), lambda i: (i, 0)), s_spec]
    out_shape = [jax.ShapeDtypeStruct((RS, NV), F32), jax.ShapeDtypeStruct((NS, RET_H, RET_DK, RET_DV), F32)]
    return in_specs, out_specs, out_shape


def _host_ret_s_body(*refs, host_body, n_in, n_out):
    host_in, guest_in = refs[:n_in], refs[n_in:n_in + 5]
    k = n_in + 5
    host_out, guest_out, scratch = refs[k:k + n_out], refs[k + n_out:k + n_out + 2], refs[k + n_out + 2:]
    host_body(*host_in, *host_out, *scratch)
    step = pl.program_id(0) * pl.num_programs(1) + pl.program_id(1)
    _ret_s_core_body(step, *guest_in, *guest_out)


def _with_ret_s(name, host_body, nt, h_in, h_out, h_shape, scratch, host_args, guest_args):
    g_in, g_out, g_shape = _ret_s_core_parts()

    def on_grid(spec):
        return pl.BlockSpec(spec.block_shape, lambda b, t, m=spec.index_map: m(b * nt + t))

    return pl.pallas_call(
        functools.partial(_host_ret_s_body, host_body=host_body, n_in=len(h_in), n_out=len(h_out)),
        grid=(NB, nt),
        in_specs=h_in + [on_grid(sp) for sp in g_in],
        out_specs=h_out + [on_grid(sp) for sp in g_out],
        out_shape=h_shape + g_shape,
        scratch_shapes=scratch,
        compiler_params=_params(("arbitrary", "arbitrary"), 56),
        name=name,
    )(*host_args, *guest_args)


def _ret_s_out_body(x_ref, o_ref, g_ref, woutf_ref, gpost_ref, xo_ref, wout_ref, y_ref):
    wout_ref[...] = woutf_ref[...].astype(BF16)
    for hd in range(RET_H):
        cols = slice(hd * RET_DV, (hd + 1) * RET_DV)
        y_ref[:, cols] = _group_norm_gate(o_ref[:, cols], g_ref[:, cols]).astype(BF16)
    out = _dot(y_ref[...], wout_ref[...])
    xo_ref[...] = x_ref[...] + _rms(out, gpost_ref[...])


def _ret_s_out(x, o, g, wout, gpost):
    return pl.pallas_call(
        _ret_s_out_body,
        grid=(1,),
        in_specs=[_const((RS, D)), _const((RS, NV)), _const((RS, NV)), _layer_const(0, (NV, D)), _const((1, D))],
        out_specs=[_full((RS, D)), _full((NV, D))],
        out_shape=[jax.ShapeDtypeStruct((RS, D), F32), jax.ShapeDtypeStruct((NV, D), BF16)],
        scratch_shapes=[pltpu.VMEM((RS, NV), BF16)],
        compiler_params=_params(("arbitrary",), 48),
        name="ret_s_out",
    )(x, o, g, wout, gpost)


def _rope_tables(pos):
    half = RET_DK // 2
    inv = ROPE_BASE ** (-jnp.arange(half, dtype=F32) / half)
    ang = pos.astype(F32)[:, None] * inv[None, :]
    return jnp.cos(ang), jnp.sin(ang)


def _state_rows(st):
    lead, c = st.shape[:-3], st.shape[-1]
    n = len(lead)
    s = st.reshape(lead + (NS, 2, c // 128, 128))
    s = s.transpose(tuple(range(n)) + (n + 2, n + 1, n, n + 3))
    return s.reshape(lead + (c // 128 * 2, NS, 128))


def _from_state_rows(s):
    tiles = s.shape[0] // 2
    return s.reshape(tiles, 2, NS, 128).transpose(2, 1, 0, 3).reshape(NS, 2, tiles * 128)


def kernel(x_prompt, x_sample, mem_prompt, cache_mem_k, cache_mem_v, state_shortconv, state_retention, state_ffn_conv, norm_mix_pre, norm_mix_post, w_in_even, sgu_vnorm, sgu_w, sgu_b, conv_short, w_out_even, w_in_odd, w_out_odd, norm_x_pre, norm_x_post, norm_mem, w_xq, w_xk, w_xv, w_xo, norm_ffn_pre, norm_ffn_post, w_ffn_up, conv_ffn, w_ffn_down):
    row = lambda g: g.reshape(1, -1)

    xs = x_sample.reshape(RS, D)
    w4 = jnp.tril(sgu_w[0][:, :TS, :TS])
    gw = jnp.repeat(w4.transpose(1, 2, 0), CHUNK, axis=2)
    gb = jnp.repeat(sgu_b[0][:, :TS].T, CHUNK, axis=1)
    xs, sc_s, v_s = _even_s(xs, row(norm_mix_pre[0]), w_in_even, row(sgu_vnorm[0]), gw, gb, conv_short,
                            _state_rows(state_shortconv[0]), w_out_even, row(norm_mix_post[0]))
    cos_s, sin_s = _rope_tables(PAST + (jnp.arange(RS, dtype=jnp.int32) % TS))
    cache_k, cache_v = _head_view(cache_mem_k), _head_view(cache_mem_v)
    ffn_s_states, ffn_p_states, ffn_w = [], [], []

    def sample_ffn(l, xs):
        xs, ng, nv, wg_b, wv_b, wd_b = _ffn_s(l, xs, row(norm_ffn_pre[l]), w_ffn_up, conv_ffn, state_ffn_conv,
                                              w_ffn_down, row(norm_ffn_post[l]))
        ffn_s_states.append(jnp.concatenate([ng, nv], axis=-1))
        ffn_w.append((wg_b, wv_b, wd_b))
        return xs

    def prompt_ffn(l, xp):
        xp, st = _ffn_p(l, xp, row(norm_ffn_pre[l]), ffn_w[l][0], ffn_w[l][1], conv_ffn, ffn_w[l][2],
                        row(norm_ffn_post[l]))
        ffn_p_states.append(st[:, 6:8, :])
        return xp

    xs = _xattn_s(0, xs, row(norm_x_pre[0]), w_xq, cache_k, cache_v, w_xo, row(norm_x_post[0]))
    xs = sample_ffn(0, xs)
    q, kdt, v, g, oi, wq_o, wk_o, wv_o, wg_o = _ret_s_proj(xs, cos_s, sin_s, row(norm_mix_pre[1]), w_in_odd)

    mem_k, mem_v, mem_kb, mem_vb = _memkv(mem_prompt.reshape(NB * N_MEM, D), norm_mem.reshape(2, 1, D), w_xk, w_xv)
    xp = x_prompt.reshape(NB * SEQ, D)
    sb_full = jnp.repeat(sgu_b[0].T, CHUNK, axis=1)
    tm = NB * SEQ // (NS // 2)
    nt, e_in, e_out, e_shape, e_scratch = _even_p_parts(tm)
    xp, sc_p, o, ret_s_state = _with_ret_s(
        "even_p_ret_s", functools.partial(_even_p_body, tm=tm), nt, e_in, e_out, e_shape, e_scratch,
        (xp, row(norm_mix_pre[0]), w_in_even, row(sgu_vnorm[0]), sgu_w, sb_full, conv_short, w_out_even,
         row(norm_mix_post[0])), (q, kdt, v, oi, state_retention[0]))
    xs, wout_o = _ret_s_out(xs, o, g, w_out_odd, row(norm_mix_post[1]))

    xp = _xattn_p(0, xp, row(norm_x_pre[0]), w_xq, mem_kb, mem_vb, w_xo, row(norm_x_post[0]))
    xp = prompt_ffn(0, xp)
    cos_p, sin_p = _rope_tables(jnp.arange(SEQ, dtype=jnp.int32))
    xp, ret_p_state = _ret_p(xp, cos_p, sin_p, row(norm_mix_pre[1]), wq_o, wk_o, wv_o, wg_o, wout_o,
                             row(norm_mix_post[1]))
    xp, xs = _xattn_ps(1, xp, row(norm_x_pre[1]), w_xq, mem_kb, mem_vb, w_xo, row(norm_x_post[1]),
                       xs, row(norm_x_pre[1]), cache_k, cache_v, row(norm_x_post[1]))
    xs = sample_ffn(1, xs)
    xp = prompt_ffn(1, xp)

    return (xp.reshape(NB, SEQ, D), xs.reshape(NS, TS, D),
            _from_head_view(mem_k), _from_head_view(mem_v),
            sc_p[None, :, 6:8, :], _from_state_rows(sc_s)[None],
            v_s.reshape(1, NS, TS, SGU_W),
            ret_p_state[None], ret_s_state[None],
            jnp.stack(ffn_p_states), jnp.stack(ffn_s_states))
```
